```python
import math
import jax, jax.numpy as jnp
from jax import lax
import numpy as np

D_MODEL = 1024
BATCH = 8
SEQ = 2048
DEPTH = 1
DEC_BATCH = 128
DEC_SEQ = 8
PAST_LEN = 16384
PAGE_SIZE = 128

SSM_EXPAND = 2
SSM_D_INNER = SSM_EXPAND * D_MODEL
SSM_HEAD_DIM = 64
SSM_HEADS = SSM_D_INNER // SSM_HEAD_DIM
SSM_GROUPS = 4
SSM_STATE = 128
SSM_CONV = 4
SSM_CHUNK = 128
SSM_CONV_DIM = SSM_D_INNER + 2 * SSM_GROUPS * SSM_STATE
RWKV_HEAD_DIM = 64
RWKV_DIM = D_MODEL
RWKV_HEADS = RWKV_DIM // RWKV_HEAD_DIM
W_LORA = 64
A_LORA = 64
G_LORA = 128
RWKV_SHIFT_DIM = 3 * RWKV_DIM + W_LORA + A_LORA + G_LORA
D_FF = 2816
FFN_CONV = 3
IN_DIM = SSM_D_INNER + SSM_CONV_DIM + SSM_HEADS + RWKV_SHIFT_DIM + 2 * D_MODEL
NORM_EPS = 1e-5
GN_EPS = 64e-5

kernel_name = 'hybrid_ssd_rwkv7_convglu_step'


def _rmsnorm(x, g):
    xf = x.astype(jnp.float32)
    y = xf * lax.rsqrt(jnp.mean(xf * xf, axis=-1, keepdims=True) + NORM_EPS)
    return (y * g.astype(jnp.float32)).astype(x.dtype)


def _causal_dwconv(u, buf, w, b):
    L = u.shape[1]
    K = w.shape[0]
    full = jnp.concatenate([buf.astype(u.dtype), u], axis=1)
    out = full[:, 0:L] * w[0]
    for j in range(1, K):
        out = out + full[:, j:j + L] * w[j]
    return out + b, full[:, full.shape[1] - (K - 1):]


def _token_shift(u, buf, mu):
    prev = jnp.concatenate([buf[:, None].astype(u.dtype), u[:, :-1]], axis=1)
    return u + (prev - u) * mu, u[:, -1]


def _ssd(xs, dt, A, Bm, Cm, h0):
    b, L, H, P = xs.shape
    G, N = Bm.shape[2], Bm.shape[3]
    Hg = H // G
    l = math.gcd(L, SSM_CHUNK)
    c = L // l
    xdt = (xs * dt[..., None]).reshape(b, c, l, G, Hg, P)
    cs = jnp.cumsum((dt * A).reshape(b, c, l, G, Hg), axis=2)
    Bc = Bm.reshape(b, c, l, G, N)
    Cc = Cm.reshape(b, c, l, G, N)
    causal = jnp.tril(jnp.ones((l, l), dtype=bool))
    seg = cs[:, :, :, None] - cs[:, :, None, :]
    decay_ls = jnp.exp(jnp.where(causal[None, None, :, :, None, None], seg, -jnp.inf))
    cb = jnp.einsum('bclgn,bcsgn->bclsg', Cc, Bc)
    y_diag = jnp.einsum('bclsgh,bcsghp->bclghp', cb[..., None] * decay_ls, xdt)
    to_end = jnp.exp(cs[:, :, -1:] - cs)
    chunk_states = jnp.einsum('bclgn,bclghp->bcghpn', Bc, xdt * to_end[..., None])
    chunk_decay = jnp.exp(cs[:, :, -1])

    def step(h, inp):
        st, dec = inp
        return h * dec[..., None, None] + st, h

    h_last, h_prev = lax.scan(step, h0.reshape(b, G, Hg, P, N),
                              (jnp.moveaxis(chunk_states, 1, 0), jnp.moveaxis(chunk_decay, 1, 0)))
    h_prev = jnp.moveaxis(h_prev, 0, 1)
    y_off = jnp.einsum('bclgn,bcghpn->bclghp', Cc, h_prev) * jnp.exp(cs)[..., None]
    y = (y_diag + y_off).reshape(b, L, H, P)
    return y, h_last.reshape(b, H, P, N)


def _rwkv7_scan(r, decay, k, v, kk, a, S0):
    def step(S, inp):
        r_t, w_t, k_t, v_t, kk_t, a_t = inp
        sa = jnp.einsum('bhvk,bhk->bhv', S, -kk_t)
        S = (S * w_t[:, :, None, :] + sa[..., None] * (kk_t * a_t)[:, :, None, :]
             + v_t[..., None] * k_t[:, :, None, :])
        return S, jnp.einsum('bhvk,bhk->bhv', S, r_t)

    xs = tuple(jnp.moveaxis(t, 1, 0) for t in (r, decay, k, v, kk, a))
    S, ys = lax.scan(step, S0, xs)
    return jnp.moveaxis(ys, 0, 1), S


def _layer(x, conv_buf, ssm_state, shift_buf, wkv_state, ffn_buf, lp):
    f32 = jnp.float32
    b, L, _ = x.shape
    G, N = SSM_GROUPS, SSM_STATE
    H, K = RWKV_HEADS, RWKV_HEAD_DIM
    h = _rmsnorm(x, lp['norm1_g'])
    proj = h @ lp['w_in']
    o1 = SSM_D_INNER
    o2 = o1 + SSM_CONV_DIM
    o3 = o2 + SSM_HEADS
    o4 = o3 + RWKV_SHIFT_DIM
    z, xbc, dt_raw, rw, gates_raw = jnp.split(proj, [o1, o2, o3, o4], axis=-1)

    xbc_c, new_conv = _causal_dwconv(xbc, conv_buf, lp['ssm_conv_w'], lp['ssm_conv_b'])
    xbc_c = jax.nn.silu(xbc_c.astype(f32))
    xm, Bm, Cm = jnp.split(xbc_c, [SSM_D_INNER, SSM_D_INNER + G * N], axis=-1)
    dt = jax.nn.softplus(dt_raw.astype(f32) + lp['ssm_dt_bias'].astype(f32))
    A = -jnp.exp(lp['ssm_a_log'].astype(f32))
    xm4 = xm.reshape(b, L, SSM_HEADS, SSM_HEAD_DIM)
    ya, new_ssm = _ssd(xm4, dt, A, Bm.reshape(b, L, G, N), Cm.reshape(b, L, G, N),
                       ssm_state.astype(f32))
    ya = ya + lp['ssm_d'].astype(f32)[:, None] * xm4
    ya = ya.reshape(b, L, G, SSM_D_INNER // G) * jax.nn.silu(z.astype(f32)).reshape(b, L, G, SSM_D_INNER // G)
    ya = ya * lax.rsqrt(jnp.mean(ya * ya, axis=-1, keepdims=True) + NORM_EPS)
    ya = ya.reshape(b, L, SSM_D_INNER) * lp['ssm_norm_g'].astype(f32)
    u_a = ya.astype(x.dtype) @ lp['w_branch_a']

    rw_mix, new_shift = _token_shift(rw, shift_buf, lp['rwkv_mu'])
    r, k, v, wl, al, gl = jnp.split(
        rw_mix.astype(f32),
        [RWKV_DIM, 2 * RWKV_DIM, 3 * RWKV_DIM, 3 * RWKV_DIM + W_LORA, 3 * RWKV_DIM + W_LORA + A_LORA],
        axis=-1)
    wlog = -jax.nn.softplus(-(lp['rwkv_w0'] + jnp.tanh(wl) @ lp['rwkv_w_up'])) - 0.5
    decay = jnp.exp(-jnp.exp(wlog.astype(f32)))
    a = jax.nn.sigmoid(lp['rwkv_a0'] + al @ lp['rwkv_a_up']).astype(f32)
    g = (jax.nn.sigmoid(gl) @ lp['rwkv_g_up']).astype(f32)
    hd = lambda t: t.reshape(b, L, H, K)
    kk = hd(k * lp['rwkv_k_k']).astype(f32)
    kk = kk / jnp.maximum(jnp.sqrt(jnp.sum(kk * kk, axis=-1, keepdims=True)), 1e-12)
    k = (k * (1.0 + (a - 1.0) * lp['rwkv_k_a'])).astype(f32)
    r4, k4, v4, a4 = hd(r), hd(k), hd(v), hd(a)
    yb, new_wkv = _rwkv7_scan(r4, hd(decay), k4, v4, kk, a4, wkv_state.astype(f32))
    mu = jnp.mean(yb, axis=-1, keepdims=True)
    var = jnp.mean((yb - mu) ** 2, axis=-1, keepdims=True)
    yb = ((yb - mu) * lax.rsqrt(var + GN_EPS) * lp['rwkv_ln_w'].astype(f32).reshape(H, K)
          + lp['rwkv_ln_b'].astype(f32).reshape(H, K))
    yb = yb + jnp.sum(r4 * k4 * lp['rwkv_r_k'].astype(f32), axis=-1, keepdims=True) * v4
    yb = yb.reshape(b, L, RWKV_DIM) * g
    u_b = yb.astype(x.dtype) @ lp['w_branch_b']

    gates = jax.nn.sigmoid(gates_raw.astype(f32))
    ga, gb = jnp.split(gates, 2, axis=-1)
    m = (ga * u_a.astype(f32) + gb * u_b.astype(f32)).astype(x.dtype)
    x = x + m @ lp['w_out']

    h2 = _rmsnorm(x, lp['norm2_g'])
    up = h2 @ lp['ffn_w_up']
    ug, uv = jnp.split(up, 2, axis=-1)
    ug, new_ffn = _causal_dwconv(ug, ffn_buf, lp['ffn_conv_w'], lp['ffn_conv_b'])
    x = x + (jax.nn.silu(ug) * uv) @ lp['ffn_w_down']
    return x, (new_conv, new_ssm, new_shift, new_wkv, new_ffn)


def setup_inputs(seed: int = 0) -> dict:
    key = jax.random.key(seed)
    ks = iter(jax.random.split(key, 48))
    f32 = jnp.float32

    def nrm(shape, scale):
        return scale * jax.random.normal(next(ks), shape, f32)

    def unif(shape, lo, hi):
        return jax.random.uniform(next(ks), shape, f32, minval=lo, maxval=hi)

    Dp = DEPTH
    dt0 = jnp.exp(unif((Dp, SSM_HEADS), math.log(1e-3), math.log(1e-1)))
    return {
        'x_prompt': nrm((BATCH, SEQ, D_MODEL), 1.0),
        'x_sample': nrm((DEC_BATCH, DEC_SEQ, D_MODEL), 1.0),
        'state_ssm_conv': nrm((Dp, DEC_BATCH, SSM_CONV - 1, SSM_CONV_DIM), 1.0),
        'state_ssm': nrm((Dp, DEC_BATCH, SSM_HEADS, SSM_HEAD_DIM, SSM_STATE), 0.1),
        'state_rwkv_shift': nrm((Dp, DEC_BATCH, RWKV_SHIFT_DIM), 1.0),
        'state_rwkv': nrm((Dp, DEC_BATCH, RWKV_HEADS, RWKV_HEAD_DIM, RWKV_HEAD_DIM), 0.1),
        'state_ffn_conv': nrm((Dp, DEC_BATCH, FFN_CONV - 1, D_FF), 1.0),
        'norm1_g': 1.0 + nrm((Dp, D_MODEL), 0.02),
        'w_in': nrm((Dp, D_MODEL, IN_DIM), D_MODEL ** -0.5),
        'ssm_conv_w': nrm((Dp, SSM_CONV, SSM_CONV_DIM), SSM_CONV ** -0.5),
        'ssm_conv_b': nrm((Dp, SSM_CONV_DIM), 0.02),
        'ssm_dt_bias': dt0 + jnp.log(-jnp.expm1(-dt0)),
        'ssm_a_log': jnp.log(unif((Dp, SSM_HEADS), 1.0, 16.0)),
        'ssm_d': 1.0 + nrm((Dp, SSM_HEADS), 0.1),
        'ssm_norm_g': 1.0 + nrm((Dp, SSM_D_INNER), 0.02),
        'w_branch_a': nrm((Dp, SSM_D_INNER, D_MODEL), SSM_D_INNER ** -0.5),
        'rwkv_mu': unif((Dp, RWKV_SHIFT_DIM), 0.0, 1.0),
        'rwkv_w0': -0.6 + nrm((Dp, RWKV_DIM), 0.3),
        'rwkv_w_up': nrm((Dp, W_LORA, RWKV_DIM), 0.1 * W_LORA ** -0.5),
        'rwkv_a0': nrm((Dp, RWKV_DIM), 0.1),
        'rwkv_a_up': nrm((Dp, A_LORA, RWKV_DIM), 0.1 * A_LORA ** -0.5),
        'rwkv_g_up': nrm((Dp, G_LORA, RWKV_DIM), G_LORA ** -0.5),
        'rwkv_k_k': 0.85 + nrm((Dp, RWKV_DIM), 0.02),
        'rwkv_k_a': 1.0 + nrm((Dp, RWKV_DIM), 0.02),
        'rwkv_r_k': nrm((Dp, RWKV_HEADS, RWKV_HEAD_DIM), 0.1),
        'rwkv_ln_w': 1.0 + nrm((Dp, RWKV_DIM), 0.02),
        'rwkv_ln_b': nrm((Dp, RWKV_DIM), 0.02),
        'w_branch_b': nrm((Dp, RWKV_DIM, D_MODEL), RWKV_DIM ** -0.5),
        'w_out': nrm((Dp, D_MODEL, D_MODEL), D_MODEL ** -0.5),
        'norm2_g': 1.0 + nrm((Dp, D_MODEL), 0.02),
        'ffn_w_up': nrm((Dp, D_MODEL, 2 * D_FF), D_MODEL ** -0.5),
        'ffn_conv_w': nrm((Dp, FFN_CONV, D_FF), FFN_CONV ** -0.5),
        'ffn_conv_b': nrm((Dp, D_FF), 0.02),
        'ffn_w_down': nrm((Dp, D_FF, D_MODEL), D_FF ** -0.5),
        'final_g': 1.0 + nrm((D_MODEL,), 0.02),
    }


def reference(x_prompt, x_sample, state_ssm_conv, state_ssm, state_rwkv_shift, state_rwkv,
              state_ffn_conv, norm1_g, w_in, ssm_conv_w, ssm_conv_b, ssm_dt_bias, ssm_a_log,
              ssm_d, ssm_norm_g, w_branch_a, rwkv_mu, rwkv_w0, rwkv_w_up, rwkv_a0, rwkv_a_up,
              rwkv_g_up, rwkv_k_k, rwkv_k_a, rwkv_r_k, rwkv_ln_w, rwkv_ln_b, w_branch_b, w_out,
              norm2_g, ffn_w_up, ffn_conv_w, ffn_conv_b, ffn_w_down, final_g):
    xp, xs = x_prompt, x_sample
    bp = xp.shape[0]
    new_p = ([], [], [], [], [])
    new_s = ([], [], [], [], [])
    for i in range(DEPTH):
        lp = {
            'norm1_g': norm1_g[i], 'w_in': w_in[i], 'ssm_conv_w': ssm_conv_w[i],
            'ssm_conv_b': ssm_conv_b[i], 'ssm_dt_bias': ssm_dt_bias[i], 'ssm_a_log': ssm_a_log[i],
            'ssm_d': ssm_d[i], 'ssm_norm_g': ssm_norm_g[i], 'w_branch_a': w_branch_a[i],
            'rwkv_mu': rwkv_mu[i], 'rwkv_w0': rwkv_w0[i], 'rwkv_w_up': rwkv_w_up[i],
            'rwkv_a0': rwkv_a0[i], 'rwkv_a_up': rwkv_a_up[i], 'rwkv_g_up': rwkv_g_up[i],
            'rwkv_k_k': rwkv_k_k[i], 'rwkv_k_a': rwkv_k_a[i], 'rwkv_r_k': rwkv_r_k[i],
            'rwkv_ln_w': rwkv_ln_w[i], 'rwkv_ln_b': rwkv_ln_b[i], 'w_branch_b': w_branch_b[i],
            'w_out': w_out[i], 'norm2_g': norm2_g[i], 'ffn_w_up': ffn_w_up[i],
            'ffn_conv_w': ffn_conv_w[i], 'ffn_conv_b': ffn_conv_b[i], 'ffn_w_down': ffn_w_down[i],
        }
        xp, sp = _layer(
            xp,
            jnp.zeros((bp, SSM_CONV - 1, SSM_CONV_DIM), xp.dtype),
            jnp.zeros((bp, SSM_HEADS, SSM_HEAD_DIM, SSM_STATE), jnp.float32),
            jnp.zeros((bp, RWKV_SHIFT_DIM), xp.dtype),
            jnp.zeros((bp, RWKV_HEADS, RWKV_HEAD_DIM, RWKV_HEAD_DIM), jnp.float32),
            jnp.zeros((bp, FFN_CONV - 1, D_FF), xp.dtype),
            lp)
        xs, ss = _layer(xs, state_ssm_conv[i], state_ssm[i], state_rwkv_shift[i], state_rwkv[i],
                        state_ffn_conv[i], lp)
        for j in range(5):
            new_p[j].append(sp[j])
            new_s[j].append(ss[j])
    y_prompt = _rmsnorm(xp, final_g)
    y_sample = _rmsnorm(xs, final_g)
    return (y_prompt, y_sample,
            jnp.stack(new_p[0]), jnp.stack(new_p[1]), jnp.stack(new_p[2]), jnp.stack(new_p[3]), jnp.stack(new_p[4]),
            jnp.stack(new_s[0]), jnp.stack(new_s[1]), jnp.stack(new_s[2]), jnp.stack(new_s[3]), jnp.stack(new_s[4]))
```

```python
import functools

import jax
import jax.numpy as jnp
from jax import lax
from jax.experimental import pallas as pl
from jax.experimental.pallas import tpu as pltpu

F32 = jnp.float32
BF16 = jnp.bfloat16

NORM_EPS = 1e-5
GN_EPS = 64e-5

LANES = 128
SUBLANES = 8
ROW_TILE = 256
SSD_CHUNK = 128
WKV_TBLOCK = 256
WKV_SAMPLE_SEQS = 4
VMEM_LIMIT = 56 * 1024 * 1024


def _dot(a, b):
    return jnp.dot(a, b, preferred_element_type=F32)


def _split(x, n):
    parts = []
    rem = x
    for i in range(n):
        p = rem.astype(BF16)
        parts.append(p)
        if i + 1 < n:
            rem = rem - p.astype(F32)
    return parts


def _dot_split_lhs(x, m, n):
    acc = None
    for p in _split(x, n):
        d = _dot(p, m)
        acc = d if acc is None else acc + d
    return acc


def _dot_split_rhs(m, x, n):
    acc = None
    for p in _split(x, n):
        d = _dot(m, p)
        acc = d if acc is None else acc + d
    return acc


def _sigmoid(x):
    return 1.0 / (1.0 + jnp.exp(-x))


def _silu(x):
    return x * _sigmoid(x)


def _softplus(x):
    return jnp.maximum(x, 0.0) + jnp.log1p(jnp.exp(-jnp.abs(x)))


def _rmsnorm(x, g, eps):
    ms = jnp.mean(x * x, axis=-1, keepdims=True)
    return x * lax.rsqrt(ms + eps) * g


def _shift_rows(u, hist, seg, j):
    rows = u.shape[0]
    ru = pltpu.roll(u, j, 0)
    if seg == SUBLANES:
        rh = pltpu.roll(hist, (rows - SUBLANES + j) % rows, 0)
        pos = lax.broadcasted_iota(jnp.int32, u.shape, 0) % SUBLANES
        return jnp.where(pos < j, rh, ru)
    assert seg == rows
    rh = pltpu.roll(hist, j, 0)
    pos = lax.broadcasted_iota(jnp.int32, rh.shape, 0)
    top = jnp.where(pos < j, rh, ru[:SUBLANES])
    return jnp.concatenate([top, ru[SUBLANES:]], axis=0)


def _causal_conv(u, hist, seg, w_ref, b_ref):
    taps = w_ref.shape[0]
    acc = u * w_ref[taps - 1:taps, :] + b_ref[...]
    for j in range(1, taps):
        acc = acc + _shift_rows(u, hist, seg, j) * w_ref[taps - 1 - j:taps - j, :]
    return acc


def _load_hist(hist_ref, carry_ref, seg):
    if seg == SUBLANES:
        nseq, _, c = hist_ref.shape
        return hist_ref[...].reshape(nseq * SUBLANES, c)

    @pl.when(pl.program_id(1) == 0)
    def _():
        carry_ref[...] = hist_ref[0]

    return carry_ref[...]


def _store_tail(u, tail_ref, carry_ref, seg):
    if seg == SUBLANES:
        tail_ref[...] = u.reshape(tail_ref.shape)
    else:
        last = u[u.shape[0] - SUBLANES:]
        carry_ref[...] = last
        tail_ref[0] = last


def _ssm_in_body(x_ref, hist_ref, g1_ref, wz_ref, wx_ref, wdt_ref, cw_ref, cb_ref, dtb_ref,
                 z_ref, xc_ref, dt_ref, tail_ref, carry_ref, *, seg):
    h = _rmsnorm(x_ref[...], g1_ref[...], NORM_EPS).astype(BF16)
    z_ref[...] = _dot(h, wz_ref[...]).astype(z_ref.dtype)
    u = _dot(h, wx_ref[...])
    hist = _load_hist(hist_ref, carry_ref, seg)
    xc_ref[...] = _silu(_causal_conv(u, hist, seg, cw_ref, cb_ref))
    _store_tail(u, tail_ref, carry_ref, seg)
    dt_ref[...] = _softplus(_dot(h, wdt_ref[...]) + dtb_ref[...])


def _rwkv_in_body(x_ref, hist_ref, g1_ref, wrw_ref, wg_ref, mu_ref, wlora_ref, w0_ref, a0_ref,
                  kk_ref, ka_ref, seg_ref, exp_ref,
                  r_out, lw_out, k_out, v_out, kkn_out, bb_out, g_out, gates_out, tail_ref,
                  carry_ref, *, seg, dim):
    h = _rmsnorm(x_ref[...], g1_ref[...], NORM_EPS).astype(BF16)
    gates_out[...] = _sigmoid(_dot(h, wg_ref[...])).astype(gates_out.dtype)
    u = _dot(h, wrw_ref[...])
    hist = _load_hist(hist_ref, carry_ref, seg)
    prev = _shift_rows(u, hist, seg, 1)
    _store_tail(u, tail_ref, carry_ref, seg)
    mix = u + (prev - u) * mu_ref[...]
    r = mix[:, 0:dim]
    k = mix[:, dim:2 * dim]
    v = mix[:, 2 * dim:3 * dim]
    low = mix[:, 3 * dim:]
    lane = lax.broadcasted_iota(jnp.int32, low.shape, 1)
    lo_w = low.shape[1] // 4
    act = jnp.where(lane < lo_w, jnp.tanh(low), jnp.where(lane < 2 * lo_w, low, _sigmoid(low)))
    up = _dot(act.astype(BF16), wlora_ref[...])
    wlog = -_softplus(-(w0_ref[...] + up[:, 0:dim])) - 0.5
    a = _sigmoid(a0_ref[...] + up[:, dim:2 * dim])
    kkr = k * kk_ref[...]
    ss = _dot_split_lhs(kkr * kkr, seg_ref[...], 2)
    inv = 1.0 / jnp.maximum(jnp.sqrt(ss), 1e-12)
    kkn = kkr * _dot_split_lhs(inv, exp_ref[...], 3)
    r_out[...] = r
    lw_out[...] = -jnp.exp(wlog)
    k_out[...] = k * (1.0 + (a - 1.0) * ka_ref[...])
    v_out[...] = v
    kkn_out[...] = kkn
    bb_out[...] = kkn * a
    g_out[...] = up[:, 2 * dim:3 * dim].astype(g_out.dtype)


def _ssd_body(xm_ref, b_ref, c_ref, dt_ref, z_ref, st_ref, alog_ref, dexp_ref, ng_ref,
              tri_ref, e_ref, e2_ref, y_ref, so_ref, ht_ref, *, rows, heads):
    chunk = pl.program_id(2)
    n_chunks = pl.num_programs(2)
    R = SSD_CHUNK

    def pad(v):
        if rows == R:
            return v
        return jnp.concatenate([v, jnp.zeros((R - rows, v.shape[1]), v.dtype)], axis=0)

    xm = pad(xm_ref[...])
    bm = pad(b_ref[...])
    cm = pad(c_ref[...])
    dt = pad(dt_ref[...])
    z = pad(z_ref[...].astype(F32))
    width = xm.shape[1]
    state_n = bm.shape[1]

    @pl.when(chunk == 0)
    def _():
        ht_ref[...] = st_ref[0].reshape(width, state_n).T

    a = dt * (-jnp.exp(alog_ref[...]))
    cs = _dot_split_rhs(tri_ref[...], a, 3)
    expand = e_ref[...]
    dt_e = _dot_split_lhs(dt, expand, 3)
    cs_e = _dot_split_lhs(cs, expand, 3)
    cl_e = cs_e[R - 1:R, :]
    xdt = xm * dt_e
    xs = xdt * jnp.exp(cl_e - cs_e)
    bt = bm.T.astype(BF16)
    cmb = cm.astype(BF16)
    st_new = _dot(bt, xs.astype(BF16))
    ht = ht_ref[...]
    y_off = _dot(cmb, ht.astype(BF16)) * jnp.exp(cs_e)
    cb = _dot(cmb, bt)
    cs_t = cs.T
    cs_e2 = _dot_split_lhs(cs, e2_ref[...], 3)
    row = lax.broadcasted_iota(jnp.int32, (R, R), 0)
    col = lax.broadcasted_iota(jnp.int32, (R, R), 1)
    causal = row >= col
    lane = lax.broadcasted_iota(jnp.int32, (R, LANES), 1)
    p_dim = width // heads
    ys = []
    for j in range(heads // 2):
        ms = []
        for hh in (2 * j, 2 * j + 1):
            seg_ = cs_e2[:, hh * R:(hh + 1) * R] - cs_t[hh:hh + 1, :]
            ms.append(jnp.where(causal, cb * jnp.exp(jnp.where(causal, seg_, 0.0)), 0.0).astype(BF16))
        lhs = jnp.concatenate(ms, axis=1)
        xp = xdt[:, j * LANES:(j + 1) * LANES]
        rhs = jnp.concatenate([jnp.where(lane < p_dim, xp, 0.0), jnp.where(lane >= p_dim, xp, 0.0)],
                              axis=0).astype(BF16)
        ys.append(_dot(lhs, rhs))
    y = jnp.concatenate(ys, axis=1) + y_off
    ht_ref[...] = ht * jnp.exp(cl_e) + st_new

    y = y + dexp_ref[...] * xm
    yz = y * _silu(z)
    yn = yz * lax.rsqrt(jnp.mean(yz * yz, axis=-1, keepdims=True) + NORM_EPS) * ng_ref[...]
    y_ref[...] = yn[:rows].astype(y_ref.dtype)

    @pl.when(chunk == n_chunks - 1)
    def _():
        so_ref[0] = ht_ref[...].T.reshape(so_ref.shape[1:])


def _wkv_body(r_ref, lw_ref, k_ref, v_ref, kk_ref, bb_ref, g_ref, s0_ref, rk_ref, lnw_ref, lnb_ref,
              bo_ref, o_ref, so_ref, s_ref, y_ref, *, head_dim):
    nseq, steps, dim = r_ref.shape
    npair = dim // LANES
    block_ones = bo_ref[...]

    @pl.when(pl.program_id(1) == 0)
    def _():
        s_ref[...] = s0_ref[...]

    rowi = lax.broadcasted_iota(jnp.int32, (head_dim, LANES), 0)
    lanei = lax.broadcasted_iota(jnp.int32, (head_dim, LANES), 1)
    diag = (rowi == lanei % head_dim).astype(F32)

    def step8(t8, carry):
        base = pl.multiple_of(t8 * SUBLANES, SUBLANES)
        rows8 = pl.ds(base, SUBLANES)
        for si in range(nseq):
            for p in range(npair):
                c = si * npair + p
                sl = slice(p * LANES, (p + 1) * LANES)
                kk8 = kk_ref[si, rows8, sl]
                w8 = jnp.exp(lw_ref[si, rows8, sl])
                bb8 = bb_ref[si, rows8, sl]
                k8 = k_ref[si, rows8, sl]
                v8 = v_ref[si, rows8, sl]
                r8 = r_ref[si, rows8, sl]
                s = s_ref[c]
                ys = []
                for j in range(SUBLANES):
                    row = lambda a: a[j:j + 1, :]
                    sa = _dot((s * row(kk8)).astype(BF16), block_ones)
                    vb = _dot((diag * row(v8)).astype(BF16), block_ones)
                    s = s * row(w8) - sa * row(bb8) + vb * row(k8)
                    yb = _dot((s * row(r8)).astype(BF16), block_ones)
                    ys.append(jnp.sum(yb * diag, axis=0, keepdims=True))
                s_ref[c] = s
                y_ref[si, rows8, sl] = jnp.concatenate(ys, axis=0)
        return carry

    lax.fori_loop(0, steps // SUBLANES, step8, 0)

    inv_n = 1.0 / head_dim
    for si in range(nseq):
        for p in range(npair):
            sl = slice(p * LANES, (p + 1) * LANES)
            y = y_ref[si, :, sl]
            mu = _dot_split_lhs(y, block_ones, 2) * inv_n
            d = y - mu
            var = _dot_split_lhs(d * d, block_ones, 2) * inv_n
            yn = d * lax.rsqrt(var + GN_EPS) * lnw_ref[:, sl] + lnb_ref[:, sl]
            bonus = _dot_split_lhs(r_ref[si, :, sl] * k_ref[si, :, sl] * rk_ref[:, sl], block_ones, 2)
            out = (yn + bonus * v_ref[si, :, sl]) * g_ref[si, :, sl].astype(F32)
            o_ref[si, :, sl] = out.astype(o_ref.dtype)
    so_ref[...] = s_ref[...]


def _post_body(x_ref, ya_ref, yb_ref, gt_ref, hist_ref, wa_ref, wb_ref, wo_ref, g2_ref, wup_ref,
               cw_ref, cb_ref, wdn_ref, gf_ref, o_ref, tail_ref, carry_ref, *, seg, final_norm):
    dm = x_ref.shape[1]
    ua = _dot(ya_ref[...].astype(BF16), wa_ref[...])
    ub = _dot(yb_ref[...].astype(BF16), wb_ref[...])
    gates = gt_ref[...].astype(F32)
    m = (gates[:, :dm] * ua + gates[:, dm:] * ub).astype(BF16)
    x1 = x_ref[...] + _dot(m, wo_ref[...])
    h2 = _rmsnorm(x1, g2_ref[...], NORM_EPS).astype(BF16)
    up = _dot(h2, wup_ref[...])
    dff = up.shape[1] // 2
    ug = up[:, :dff]
    hist = _load_hist(hist_ref, carry_ref, seg)
    ugc = _causal_conv(ug, hist, seg, cw_ref, cb_ref)
    _store_tail(ug, tail_ref, carry_ref, seg)
    act = (_silu(ugc) * up[:, dff:]).astype(BF16)
    x2 = x1 + _dot(act, wdn_ref[...])
    if final_norm:
        x2 = _rmsnorm(x2, gf_ref[...], NORM_EPS)
    o_ref[...] = x2


def _const_spec(shape):
    nd = len(shape)
    return pl.BlockSpec(shape, lambda *_: (0,) * nd, pipeline_mode=pl.Buffered(1))


def _params(sem):
    return pltpu.CompilerParams(dimension_semantics=sem, vmem_limit_bytes=VMEM_LIMIT)


def _token_tiling(nseq, seqlen):
    if seqlen == SUBLANES:
        per = ROW_TILE // SUBLANES
        assert nseq % per == 0
        return per, SUBLANES, (nseq // per, 1)
    assert seqlen % ROW_TILE == 0
    return 1, ROW_TILE, (nseq, seqlen // ROW_TILE)


def _row_spec(cols, lt):
    return pl.BlockSpec((ROW_TILE, cols), lambda i, l: (i * lt + l, 0))


def _hist_spec(per, cols):
    return pl.BlockSpec((per, SUBLANES, cols), lambda i, l: (i, 0, 0))


def _ssm_in(x, hist, p, nseq, seqlen, act_dtype):
    per, seg, grid = _token_tiling(nseq, seqlen)
    t, dm = x.shape
    dz = p['wz'].shape[1]
    dc = p['wx'].shape[1]
    dd = p['wdt'].shape[1]
    lt = grid[1]
    return pl.pallas_call(
        functools.partial(_ssm_in_body, seg=seg),
        grid=grid,
        in_specs=[_row_spec(dm, lt), _hist_spec(per, dc), _const_spec((1, dm)),
                  _const_spec(p['wz'].shape), _const_spec(p['wx'].shape), _const_spec(p['wdt'].shape),
                  _const_spec(p['conv_w'].shape), _const_spec((1, dc)), _const_spec((1, dd))],
        out_specs=[_row_spec(dz, lt), _row_spec(dc, lt), _row_spec(dd, lt), _hist_spec(per, dc)],
        out_shape=[jax.ShapeDtypeStruct((t, dz), act_dtype), jax.ShapeDtypeStruct((t, dc), F32),
                   jax.ShapeDtypeStruct((t, dd), F32), jax.ShapeDtypeStruct((nseq, SUBLANES, dc), F32)],
        scratch_shapes=[pltpu.VMEM((SUBLANES, dc), F32)],
        compiler_params=_params(("arbitrary", "arbitrary")),
        name="ssm_in",
    )(x, hist, p['norm1_g'], p['wz'], p['wx'], p['wdt'], p['conv_w'], p['conv_b'], p['dt_bias'])


def _rwkv_in(x, hist, p, nseq, seqlen, act_dtype):
    per, seg, grid = _token_tiling(nseq, seqlen)
    t, dm = x.shape
    dim = p['w0'].shape[1]
    drw = p['wrw'].shape[1]
    dg = p['wg'].shape[1]
    lt = grid[1]
    f32_out = jax.ShapeDtypeStruct((t, dim), F32)
    return pl.pallas_call(
        functools.partial(_rwkv_in_body, seg=seg, dim=dim),
        grid=grid,
        in_specs=[_row_spec(dm, lt), _hist_spec(per, drw), _const_spec((1, dm)),
                  _const_spec(p['wrw'].shape), _const_spec(p['wg'].shape), _const_spec((1, drw)),
                  _const_spec(p['wlora'].shape), _const_spec((1, dim)), _const_spec((1, dim)),
                  _const_spec((1, dim)), _const_spec((1, dim)),
                  _const_spec(p['head_sum'].shape), _const_spec(p['head_expand'].shape)],
        out_specs=[_row_spec(dim, lt)] * 7 + [_row_spec(dg, lt), _hist_spec(per, drw)],
        out_shape=[f32_out] * 6 + [jax.ShapeDtypeStruct((t, dim), act_dtype),
                                   jax.ShapeDtypeStruct((t, dg), BF16),
                                   jax.ShapeDtypeStruct((nseq, SUBLANES, drw), F32)],
        scratch_shapes=[pltpu.VMEM((SUBLANES, drw), F32)],
        compiler_params=_params(("arbitrary", "arbitrary")),
        name="rwkv_in",
    )(x, hist, p['norm1_g'], p['wrw'], p['wg'], p['mu'], p['wlora'], p['w0'], p['a0'],
      p['k_k'], p['k_a'], p['head_sum'], p['head_expand'])


def _ssd(xc, dt, z, state, p, nseq, seqlen, act_dtype):
    t = xc.shape[0]
    _, n_heads, p_dim, state_n = state.shape
    groups = dt.shape[1] // LANES
    heads = n_heads // groups
    width = heads * p_dim
    d_inner = groups * width
    rows = SSD_CHUNK if seqlen % SSD_CHUNK == 0 else seqlen
    assert seqlen % rows == 0 and (rows == SSD_CHUNK or seqlen == rows == SUBLANES)
    nc = seqlen // rows
    assert state_n == LANES and width % LANES == 0 and p_dim * 2 == LANES
    b_blk = d_inner // state_n
    return pl.pallas_call(
        functools.partial(_ssd_body, rows=rows, heads=heads),
        grid=(nseq, groups, nc),
        in_specs=[pl.BlockSpec((rows, width), lambda b, g, c: (b * nc + c, g)),
                  pl.BlockSpec((rows, state_n), lambda b, g, c: (b * nc + c, b_blk + g)),
                  pl.BlockSpec((rows, state_n), lambda b, g, c: (b * nc + c, b_blk + groups + g)),
                  pl.BlockSpec((rows, LANES), lambda b, g, c: (b * nc + c, g)),
                  pl.BlockSpec((rows, width), lambda b, g, c: (b * nc + c, g)),
                  pl.BlockSpec((1, heads, p_dim, state_n), lambda b, g, c: (b, g, 0, 0)),
                  pl.BlockSpec((1, LANES), lambda b, g, c: (0, g)),
                  pl.BlockSpec((1, width), lambda b, g, c: (0, g)),
                  pl.BlockSpec((1, width), lambda b, g, c: (0, g)),
                  _const_spec(p['tri'].shape), _const_spec(p['ssd_expand'].shape),
                  _const_spec(p['ssd_expand2'].shape)],
        out_specs=[pl.BlockSpec((rows, width), lambda b, g, c: (b * nc + c, g)),
                   pl.BlockSpec((1, heads, p_dim, state_n), lambda b, g, c: (b, g, 0, 0))],
        out_shape=[jax.ShapeDtypeStruct((t, d_inner), act_dtype),
                   jax.ShapeDtypeStruct(state.shape, F32)],
        scratch_shapes=[pltpu.VMEM((state_n, width), F32)],
        compiler_params=_params(("arbitrary", "arbitrary", "arbitrary")),
        name="ssd",
    )(xc, xc, xc, dt, z, state, p['a_log'], p['d_exp'], p['ssm_norm_g'],
      p['tri'], p['ssd_expand'], p['ssd_expand2'])


def _wkv(r, lw, k, v, kk, bb, g, state, p, nseq, seqlen, act_dtype):
    dim = r.shape[1]
    head_dim = state.shape[1]
    npair = dim // LANES
    if seqlen == SUBLANES:
        per, steps = WKV_SAMPLE_SEQS, SUBLANES
    else:
        per, steps = 1, WKV_TBLOCK
    assert nseq % per == 0 and seqlen % steps == 0
    grid = (nseq // per, seqlen // steps)
    seq_spec = pl.BlockSpec((per, steps, dim), lambda i, tb: (i, tb, 0))
    st_spec = pl.BlockSpec((per * npair, head_dim, LANES), lambda i, tb: (i, 0, 0))
    as3 = lambda a: a.reshape(nseq, seqlen, dim)
    out, s_out = pl.pallas_call(
        functools.partial(_wkv_body, head_dim=head_dim),
        grid=grid,
        in_specs=[seq_spec] * 7 + [st_spec, _const_spec((1, dim)), _const_spec((1, dim)),
                                   _const_spec((1, dim)), _const_spec((LANES, LANES))],
        out_specs=[seq_spec, st_spec],
        out_shape=[jax.ShapeDtypeStruct((nseq, seqlen, dim), act_dtype),
                   jax.ShapeDtypeStruct(state.shape, F32)],
        scratch_shapes=[pltpu.VMEM((per * npair, head_dim, LANES), F32),
                        pltpu.VMEM((per, steps, dim), F32)],
        compiler_params=_params(("arbitrary", "arbitrary")),
        name="wkv",
    )(as3(r), as3(lw), as3(k), as3(v), as3(kk), as3(bb), as3(g), state,
      p['r_k'], p['ln_w'], p['ln_b'], p['block_ones'])
    return out.reshape(nseq * seqlen, dim), s_out


def _post(x, ya, yb, gates, hist, p, nseq, seqlen, final_norm):
    per, seg, grid = _token_tiling(nseq, seqlen)
    t, dm = x.shape
    dff = p['wdn'].shape[0]
    lt = grid[1]
    return pl.pallas_call(
        functools.partial(_post_body, seg=seg, final_norm=final_norm),
        grid=grid,
        in_specs=[_row_spec(dm, lt), _row_spec(ya.shape[1], lt), _row_spec(yb.shape[1], lt),
                  _row_spec(gates.shape[1], lt), _hist_spec(per, dff),
                  _const_spec(p['wa'].shape), _const_spec(p['wb'].shape), _const_spec(p['wo'].shape),
                  _const_spec((1, dm)), _const_spec(p['wup'].shape), _const_spec(p['ffn_conv_w'].shape),
                  _const_spec((1, dff)), _const_spec(p['wdn'].shape), _const_spec((1, dm))],
        out_specs=[_row_spec(dm, lt), _hist_spec(per, dff)],
        out_shape=[jax.ShapeDtypeStruct((t, dm), F32), jax.ShapeDtypeStruct((nseq, SUBLANES, dff), F32)],
        scratch_shapes=[pltpu.VMEM((SUBLANES, dff), F32)],
        compiler_params=_params(("arbitrary", "arbitrary")),
        name="post",
    )(x, ya, yb, gates, hist, p['wa'], p['wb'], p['wo'], p['norm2_g'], p['wup'], p['ffn_conv_w'],
      p['ffn_conv_b'], p['wdn'], p['final_g'])


def _prep_layer(w, dims):
    d_inner, conv_dim, n_heads, shift_dim, groups, rwkv_dim, rwkv_heads = dims
    row = lambda a: a.reshape(1, -1).astype(F32)
    w_in = w['w_in']
    o1 = d_inner
    o2 = o1 + conv_dim
    o3 = o2 + n_heads
    o4 = o3 + shift_dim
    hpg = n_heads // groups

    def per_group(a):
        lead = a.shape[:-1]
        a = a.reshape(lead + (groups, hpg))
        a = jnp.pad(a, [(0, 0)] * len(lead) + [(0, 0), (0, LANES - hpg)])
        return a.reshape(lead + (groups * LANES,))

    p_dim = d_inner // n_heads
    head_dim = rwkv_dim // rwkv_heads
    lora_w = w['rwkv_w_up'].shape[0]
    lora_a = w['rwkv_a_up'].shape[0]
    lora_g = w['rwkv_g_up'].shape[0]
    assert lora_w == lora_a and lora_g == 2 * lora_w
    wlora = jnp.zeros((lora_w + lora_a + lora_g, 3 * rwkv_dim), F32)
    wlora = wlora.at[:lora_w, :rwkv_dim].set(w['rwkv_w_up'])
    wlora = wlora.at[lora_w:lora_w + lora_a, rwkv_dim:2 * rwkv_dim].set(w['rwkv_a_up'])
    wlora = wlora.at[lora_w + lora_a:, 2 * rwkv_dim:].set(w['rwkv_g_up'])

    ch = jnp.arange(rwkv_dim) // head_dim
    head_sum = (ch[:, None] == jnp.arange(LANES)[None, :]).astype(BF16)
    li = jnp.arange(LANES)
    width = hpg * p_dim
    ssd_expand = (li[:, None] == (jnp.arange(width) // p_dim)[None, :]).astype(BF16)
    ssd_expand2 = (li[:, None] == (jnp.arange(hpg * SSD_CHUNK) // SSD_CHUNK)[None, :]).astype(BF16)
    ci = jnp.arange(SSD_CHUNK)
    tri = (ci[:, None] >= ci[None, :]).astype(BF16)
    block_ones = ((li[:, None] // head_dim) == (li[None, :] // head_dim)).astype(BF16)

    return {
        'norm1_g': row(w['norm1_g']),
        'wz': w_in[:, :o1].astype(BF16),
        'wx': w_in[:, o1:o2].astype(BF16),
        'wdt': per_group(w_in[:, o2:o3]).astype(BF16),
        'wrw': w_in[:, o3:o4].astype(BF16),
        'wg': w_in[:, o4:].astype(BF16),
        'conv_w': w['ssm_conv_w'].astype(F32),
        'conv_b': row(w['ssm_conv_b']),
        'dt_bias': row(per_group(w['ssm_dt_bias'])),
        'a_log': row(per_group(w['ssm_a_log'])),
        'd_exp': row(jnp.repeat(w['ssm_d'], p_dim)),
        'ssm_norm_g': row(w['ssm_norm_g']),
        'wa': w['w_branch_a'].astype(BF16),
        'mu': row(w['rwkv_mu']),
        'wlora': wlora.astype(BF16),
        'w0': row(w['rwkv_w0']),
        'a0': row(w['rwkv_a0']),
        'k_k': row(w['rwkv_k_k']),
        'k_a': row(w['rwkv_k_a']),
        'r_k': row(w['rwkv_r_k']),
        'ln_w': row(w['rwkv_ln_w']),
        'ln_b': row(w['rwkv_ln_b']),
        'wb': w['w_branch_b'].astype(BF16),
        'wo': w['w_out'].astype(BF16),
        'norm2_g': row(w['norm2_g']),
        'wup': w['ffn_w_up'].astype(BF16),
        'ffn_conv_w': w['ffn_conv_w'].astype(F32),
        'ffn_conv_b': row(w['ffn_conv_b']),
        'wdn': w['ffn_w_down'].astype(BF16),
        'head_sum': head_sum,
        'head_expand': head_sum.T,
        'ssd_expand': ssd_expand,
        'ssd_expand2': ssd_expand2,
        'tri': tri,
        'block_ones': block_ones,
    }


def _hist8(state_rows):
    k = state_rows.shape[1]
    return jnp.pad(state_rows.astype(F32), ((0, 0), (SUBLANES - k, 0), (0, 0)))


def _pair_heads(state):
    nseq, nh, dv, dk = state.shape
    s = state.astype(F32).reshape(nseq, nh // 2, 2, dv, dk)
    return jnp.transpose(s, (0, 1, 3, 2, 4)).reshape(nseq * (nh // 2), dv, 2 * dk)


def _unpair_heads(state, nseq):
    npair, dv, dk2 = state.shape[0] // nseq, state.shape[1], state.shape[2]
    s = state.reshape(nseq, npair, dv, 2, dk2 // 2)
    return jnp.transpose(s, (0, 1, 3, 2, 4)).reshape(nseq, npair * 2, dv, dk2 // 2)


def _layer(x, conv_buf, ssm_state, shift_buf, wkv_state, ffn_buf, p, final_norm):
    nseq, seqlen, dm = x.shape
    act_dtype = BF16 if seqlen % 16 == 0 else F32
    xf = x.reshape(nseq * seqlen, dm)
    z, xc, dt, conv_tail = _ssm_in(xf, _hist8(conv_buf), p, nseq, seqlen, act_dtype)
    r, lw, k, v, kk, bb, g, gates, shift_tail = _rwkv_in(xf, _hist8(shift_buf[:, None]), p, nseq, seqlen,
                                                         act_dtype)
    ya, new_ssm = _ssd(xc, dt, z, ssm_state.astype(F32), p, nseq, seqlen, act_dtype)
    yb, wkv_pairs = _wkv(r, lw, k, v, kk, bb, g, _pair_heads(wkv_state), p, nseq, seqlen, act_dtype)
    out, ffn_tail = _post(xf, ya, yb, gates, _hist8(ffn_buf), p, nseq, seqlen, final_norm)
    new_conv = conv_tail[:, SUBLANES - conv_buf.shape[1]:]
    new_shift = shift_tail[:, SUBLANES - 1]
    new_ffn = ffn_tail[:, SUBLANES - ffn_buf.shape[1]:]
    new_wkv = _unpair_heads(wkv_pairs, nseq)
    return out.reshape(nseq, seqlen, dm), (new_conv, new_ssm, new_shift, new_wkv, new_ffn)


_LAYER_WEIGHTS = ('norm1_g', 'w_in', 'ssm_conv_w', 'ssm_conv_b', 'ssm_dt_bias', 'ssm_a_log', 'ssm_d',
                  'ssm_norm_g', 'w_branch_a', 'rwkv_mu', 'rwkv_w0', 'rwkv_w_up', 'rwkv_a0', 'rwkv_a_up',
                  'rwkv_g_up', 'rwkv_k_k', 'rwkv_k_a', 'rwkv_r_k', 'rwkv_ln_w', 'rwkv_ln_b', 'w_branch_b',
                  'w_out', 'norm2_g', 'ffn_w_up', 'ffn_conv_w', 'ffn_conv_b', 'ffn_w_down')


def kernel(x_prompt, x_sample, state_ssm_conv, state_ssm, state_rwkv_shift, state_rwkv, state_ffn_conv,
           norm1_g, w_in, ssm_conv_w, ssm_conv_b, ssm_dt_bias, ssm_a_log, ssm_d, ssm_norm_g, w_branch_a,
           rwkv_mu, rwkv_w0, rwkv_w_up, rwkv_a0, rwkv_a_up, rwkv_g_up, rwkv_k_k, rwkv_k_a, rwkv_r_k,
           rwkv_ln_w, rwkv_ln_b, w_branch_b, w_out, norm2_g, ffn_w_up, ffn_conv_w, ffn_conv_b, ffn_w_down,
           final_g):
    stacked = dict(zip(_LAYER_WEIGHTS, (
        norm1_g, w_in, ssm_conv_w, ssm_conv_b, ssm_dt_bias, ssm_a_log, ssm_d, ssm_norm_g, w_branch_a,
        rwkv_mu, rwkv_w0, rwkv_w_up, rwkv_a0, rwkv_a_up, rwkv_g_up, rwkv_k_k, rwkv_k_a, rwkv_r_k,
        rwkv_ln_w, rwkv_ln_b, w_branch_b, w_out, norm2_g, ffn_w_up, ffn_conv_w, ffn_conv_b, ffn_w_down)))
    depth = w_in.shape[0]
    _, _, n_heads, _, _ = state_ssm.shape
    conv_dim = state_ssm_conv.shape[-1]
    d_inner = w_branch_a.shape[1]
    shift_dim = state_rwkv_shift.shape[-1]
    rwkv_heads = state_rwkv.shape[2]
    rwkv_dim = w_branch_b.shape[1]
    groups = (conv_dim - d_inner) // (2 * state_ssm.shape[-1])
    dims = (d_inner, conv_dim, n_heads, shift_dim, groups, rwkv_dim, rwkv_heads)

    xp, xs = x_prompt, x_sample
    bp = xp.shape[0]
    new_p = ([], [], [], [], [])
    new_s = ([], [], [], [], [])
    for i in range(depth):
        p = _prep_layer({name: a[i] for name, a in stacked.items()}, dims)
        p['final_g'] = final_g.reshape(1, -1).astype(F32)
        last = i == depth - 1
        xp, sp = _layer(
            xp,
            jnp.zeros((bp,) + state_ssm_conv.shape[2:], F32),
            jnp.zeros((bp,) + state_ssm.shape[2:], F32),
            jnp.zeros((bp,) + state_rwkv_shift.shape[2:], F32),
            jnp.zeros((bp,) + state_rwkv.shape[2:], F32),
            jnp.zeros((bp,) + state_ffn_conv.shape[2:], F32),
            p, last)
        xs, ss = _layer(xs, state_ssm_conv[i], state_ssm[i], state_rwkv_shift[i], state_rwkv[i],
                        state_ffn_conv[i], p, last)
        for j in range(5):
            new_p[j].append(sp[j])
            new_s[j].append(ss[j])
    return (xp, xs,
            jnp.stack(new_p[0]), jnp.stack(new_p[1]), jnp.stack(new_p[2]), jnp.stack(new_p[3]),
            jnp.stack(new_p[4]),
            jnp.stack(new_s[0]), jnp.stack(new_s[1]), jnp.stack(new_s[2]), jnp.stack(new_s[3]),
            jnp.stack(new_s[4]))
```

```python
import functools

import jax
import jax.numpy as jnp
from jax import lax
from jax.experimental import pallas as pl
from jax.experimental.pallas import tpu as pltpu

F32 = jnp.float32
BF16 = jnp.bfloat16

NORM_EPS = 1e-5
GN_EPS = 64e-5

LANES = 128
SUBLANES = 8
ROW_TILE = 256
SSD_CHUNK = 128
WKV_TBLOCK = 256
WKV_PROMPT_SEQS = 2
WKV_SAMPLE_SEQS = 4
VMEM_LIMIT = 56 * 1024 * 1024


def _dot(a, b):
    return jnp.dot(a, b, preferred_element_type=F32)


def _split(x, n):
    parts = []
    rem = x
    for i in range(n):
        p = rem.astype(BF16)
        parts.append(p)
        if i + 1 < n:
            rem = rem - p.astype(F32)
    return parts


def _dot_split_lhs(x, m, n):
    acc = None
    for p in _split(x, n):
        d = _dot(p, m)
        acc = d if acc is None else acc + d
    return acc


def _dot_split_rhs(m, x, n):
    acc = None
    for p in _split(x, n):
        d = _dot(m, p)
        acc = d if acc is None else acc + d
    return acc


def _sigmoid(x):
    return 1.0 / (1.0 + jnp.exp(-x))


def _silu(x):
    return x * _sigmoid(x)


def _softplus(x):
    return jnp.maximum(x, 0.0) + jnp.log1p(jnp.exp(-jnp.abs(x)))


def _rmsnorm(x, g, eps):
    ms = jnp.mean(x * x, axis=-1, keepdims=True)
    return x * lax.rsqrt(ms + eps) * g


def _shift_rows(u, hist, seg, j):
    rows = u.shape[0]
    ru = pltpu.roll(u, j, 0)
    if seg == SUBLANES:
        rh = pltpu.roll(hist, (rows - SUBLANES + j) % rows, 0)
        pos = lax.broadcasted_iota(jnp.int32, u.shape, 0) % SUBLANES
        return jnp.where(pos < j, rh, ru)
    assert seg == rows
    rh = pltpu.roll(hist, j, 0)
    pos = lax.broadcasted_iota(jnp.int32, rh.shape, 0)
    top = jnp.where(pos < j, rh, ru[:SUBLANES])
    return jnp.concatenate([top, ru[SUBLANES:]], axis=0)


def _causal_conv(u, hist, seg, w_ref, b_ref):
    taps = w_ref.shape[0]
    acc = u * w_ref[taps - 1:taps, :] + b_ref[...]
    for j in range(1, taps):
        acc = acc + _shift_rows(u, hist, seg, j) * w_ref[taps - 1 - j:taps - j, :]
    return acc


def _load_hist(hist_ref, carry_ref, seg):
    if seg == SUBLANES:
        nseq, _, c = hist_ref.shape
        return hist_ref[...].reshape(nseq * SUBLANES, c)

    @pl.when(pl.program_id(1) == 0)
    def _():
        carry_ref[...] = hist_ref[0]

    return carry_ref[...]


def _store_tail(u, tail_ref, carry_ref, seg):
    if seg == SUBLANES:
        tail_ref[...] = u.reshape(tail_ref.shape)
    else:
        last = u[u.shape[0] - SUBLANES:]
        carry_ref[...] = last
        tail_ref[0] = last


def _ssm_in_body(x_ref, hist_ref, g1_ref, wz_ref, wx_ref, wdt_ref, cw_ref, cb_ref, dtb_ref,
                 z_ref, xc_ref, dt_ref, tail_ref, carry_ref, *, seg):
    h = _rmsnorm(x_ref[...], g1_ref[...], NORM_EPS).astype(BF16)
    z_ref[...] = _dot(h, wz_ref[...]).astype(z_ref.dtype)
    u = _dot(h, wx_ref[...])
    hist = _load_hist(hist_ref, carry_ref, seg)
    xc_ref[...] = _silu(_causal_conv(u, hist, seg, cw_ref, cb_ref))
    _store_tail(u, tail_ref, carry_ref, seg)
    dt_ref[...] = _softplus(_dot(h, wdt_ref[...]) + dtb_ref[...])


def _rwkv_in_body(x_ref, hist_ref, g1_ref, wrw_ref, wg_ref, mu_ref, wlora_ref, w0_ref, a0_ref,
                  kk_ref, ka_ref, seg_ref, exp_ref,
                  r_out, lw_out, k_out, v_out, kkn_out, bb_out, g_out, gates_out, tail_ref,
                  carry_ref, *, seg, dim):
    h = _rmsnorm(x_ref[...], g1_ref[...], NORM_EPS).astype(BF16)
    gates_out[...] = _sigmoid(_dot(h, wg_ref[...])).astype(gates_out.dtype)
    u = _dot(h, wrw_ref[...])
    hist = _load_hist(hist_ref, carry_ref, seg)
    prev = _shift_rows(u, hist, seg, 1)
    _store_tail(u, tail_ref, carry_ref, seg)
    mix = u + (prev - u) * mu_ref[...]
    r = mix[:, 0:dim]
    k = mix[:, dim:2 * dim]
    v = mix[:, 2 * dim:3 * dim]
    low = mix[:, 3 * dim:]
    lane = lax.broadcasted_iota(jnp.int32, low.shape, 1)
    lo_w = low.shape[1] // 4
    act = jnp.where(lane < lo_w, jnp.tanh(low), jnp.where(lane < 2 * lo_w, low, _sigmoid(low)))
    up = _dot(act.astype(BF16), wlora_ref[...])
    wlog = -_softplus(-(w0_ref[...] + up[:, 0:dim])) - 0.5
    a = _sigmoid(a0_ref[...] + up[:, dim:2 * dim])
    kkr = k * kk_ref[...]
    ss = _dot_split_lhs(kkr * kkr, seg_ref[...], 2)
    inv = 1.0 / jnp.maximum(jnp.sqrt(ss), 1e-12)
    kkn = kkr * _dot_split_lhs(inv, exp_ref[...], 3)
    r_out[...] = r
    lw_out[...] = -jnp.exp(wlog)
    k_out[...] = k * (1.0 + (a - 1.0) * ka_ref[...])
    v_out[...] = v
    kkn_out[...] = kkn
    bb_out[...] = kkn * a
    g_out[...] = up[:, 2 * dim:3 * dim].astype(g_out.dtype)


def _ssd_body(xm_ref, b_ref, c_ref, dt_ref, z_ref, st_ref, alog_ref, dexp_ref, ng_ref,
              tri_ref, e_ref, e2_ref, y_ref, so_ref, ht_ref, *, rows, heads):
    chunk = pl.program_id(2)
    n_chunks = pl.num_programs(2)
    R = SSD_CHUNK

    def pad(v):
        if rows == R:
            return v
        return jnp.concatenate([v, jnp.zeros((R - rows, v.shape[1]), v.dtype)], axis=0)

    xm = pad(xm_ref[...])
    bm = pad(b_ref[...])
    cm = pad(c_ref[...])
    dt = pad(dt_ref[...])
    z = pad(z_ref[...].astype(F32))
    width = xm.shape[1]
    state_n = bm.shape[1]

    @pl.when(chunk == 0)
    def _():
        ht_ref[...] = st_ref[0].reshape(width, state_n).T

    a = dt * (-jnp.exp(alog_ref[...]))
    cs = _dot_split_rhs(tri_ref[...], a, 3)
    expand = e_ref[...]
    dt_e = _dot_split_lhs(dt, expand, 3)
    cs_e = _dot_split_lhs(cs, expand, 3)
    cl_e = cs_e[R - 1:R, :]
    xdt = xm * dt_e
    xs = xdt * jnp.exp(cl_e - cs_e)
    bt = bm.T.astype(BF16)
    cmb = cm.astype(BF16)
    st_new = _dot(bt, xs.astype(BF16))
    ht = ht_ref[...]
    y_off = _dot(cmb, ht.astype(BF16)) * jnp.exp(cs_e)
    cb = _dot(cmb, bt)
    cs_t = cs.T
    cs_e2 = _dot_split_lhs(cs, e2_ref[...], 3)
    row = lax.broadcasted_iota(jnp.int32, (R, R), 0)
    col = lax.broadcasted_iota(jnp.int32, (R, R), 1)
    causal = row >= col
    lane = lax.broadcasted_iota(jnp.int32, (R, LANES), 1)
    p_dim = width // heads
    ys = []
    for j in range(heads // 2):
        ms = []
        for hh in (2 * j, 2 * j + 1):
            seg_ = cs_e2[:, hh * R:(hh + 1) * R] - cs_t[hh:hh + 1, :]
            ms.append(jnp.where(causal, cb * jnp.exp(jnp.where(causal, seg_, 0.0)), 0.0).astype(BF16))
        lhs = jnp.concatenate(ms, axis=1)
        xp = xdt[:, j * LANES:(j + 1) * LANES]
        rhs = jnp.concatenate([jnp.where(lane < p_dim, xp, 0.0), jnp.where(lane >= p_dim, xp, 0.0)],
                              axis=0).astype(BF16)
        ys.append(_dot(lhs, rhs))
    y = jnp.concatenate(ys, axis=1) + y_off
    ht_ref[...] = ht * jnp.exp(cl_e) + st_new

    y = y + dexp_ref[...] * xm
    yz = y * _silu(z)
    yn = yz * lax.rsqrt(jnp.mean(yz * yz, axis=-1, keepdims=True) + NORM_EPS) * ng_ref[...]
    y_ref[...] = yn[:rows].astype(y_ref.dtype)

    @pl.when(chunk == n_chunks - 1)
    def _():
        so_ref[0] = ht_ref[...].T.reshape(so_ref.shape[1:])


def _dot_nt(a, b):
    return lax.dot_general(a, b, (((1,), (1,)), ((), ())), preferred_element_type=F32)


def _wkv_body(r_ref, lw_ref, k_ref, v_ref, kk_ref, bb_ref, g_ref, s0_ref, rk_ref, lnw_ref, lnb_ref,
              bo_ref, o_ref, so_ref, s_ref, y_ref, *, head_dim):
    nseq, steps, dim = r_ref.shape
    npair = dim // LANES
    block_ones = bo_ref[...]

    @pl.when(pl.program_id(1) == 0)
    def _():
        s_ref[...] = s0_ref[...]

    lane8 = lax.broadcasted_iota(jnp.int32, (SUBLANES, LANES), 1)
    head0 = lane8 < head_dim
    step16 = lax.broadcasted_iota(jnp.int32, (2 * SUBLANES, LANES), 0) % SUBLANES
    rowi = lax.broadcasted_iota(jnp.int32, (head_dim, LANES), 0)
    lanei = lax.broadcasted_iota(jnp.int32, (head_dim, LANES), 1)
    diag = (rowi == lanei % head_dim).astype(BF16)
    reps = head_dim // (2 * SUBLANES)

    def by_head(a8):
        return jnp.concatenate([jnp.where(head0, a8, 0.0), jnp.where(head0, 0.0, a8)], axis=0)

    def bcast16(row):
        return jnp.broadcast_to(row, (2 * SUBLANES, LANES)).astype(BF16)

    def step8(t8, carry):
        base = pl.multiple_of(t8 * SUBLANES, SUBLANES)
        rows8 = pl.ds(base, SUBLANES)
        chains = []
        for si in range(nseq):
            for p in range(npair):
                sl = slice(p * LANES, (p + 1) * LANES)
                v16 = by_head(v_ref[si, rows8, sl]).astype(BF16)
                chains.append(dict(
                    kk16=by_head(kk_ref[si, rows8, sl]),
                    w8=jnp.exp(lw_ref[si, rows8, sl]),
                    bb8=bb_ref[si, rows8, sl],
                    k16=by_head(k_ref[si, rows8, sl]),
                    r16=by_head(r_ref[si, rows8, sl]),
                    vt16=_dot_nt(diag, v16).astype(BF16),
                    s=s_ref[si * npair + p],
                    y=jnp.zeros((2 * SUBLANES, head_dim), F32),
                ))
        for ch in chains:
            ch['sb'] = ch['s'].astype(BF16)
        for j in range(SUBLANES):
            for ch in chains:
                kk0 = bcast16(ch['kk16'][j:j + 1])
                kk1 = bcast16(ch['kk16'][SUBLANES + j:SUBLANES + j + 1])
                kkw = jnp.concatenate([kk0] * reps + [kk1] * reps, axis=0)
                ch['sa'] = _dot_nt(ch['sb'], kkw)
                kj = jnp.where(step16 == j, ch['k16'], 0.0).astype(BF16)
                ch['vk'] = _dot(ch['vt16'], kj)
            for ch in chains:
                s = ch['s'] * ch['w8'][j:j + 1] - ch['sa'] * ch['bb8'][j:j + 1] + ch['vk']
                ch['s'] = s
                ch['sb'] = s.astype(BF16)
                rj = jnp.where(step16 == j, ch['r16'], 0.0).astype(BF16)
                ch['y'] = ch['y'] + _dot_nt(rj, ch['sb'])
        for c, ch in enumerate(chains):
            s_ref[c] = ch['s']
            y_ref[c, 0, rows8, :] = ch['y'][:SUBLANES]
            y_ref[c, 1, rows8, :] = ch['y'][SUBLANES:]
        return carry

    lax.fori_loop(0, steps // SUBLANES, step8, 0)

    inv_n = 1.0 / head_dim
    for si in range(nseq):
        for p in range(npair):
            c = si * npair + p
            sl = slice(p * LANES, (p + 1) * LANES)
            y = jnp.concatenate([y_ref[c, 0], y_ref[c, 1]], axis=1)
            mu = _dot_split_lhs(y, block_ones, 2) * inv_n
            d = y - mu
            var = _dot_split_lhs(d * d, block_ones, 2) * inv_n
            yn = d * lax.rsqrt(var + GN_EPS) * lnw_ref[:, sl] + lnb_ref[:, sl]
            bonus = _dot_split_lhs(r_ref[si, :, sl] * k_ref[si, :, sl] * rk_ref[:, sl], block_ones, 2)
            out = (yn + bonus * v_ref[si, :, sl]) * g_ref[si, :, sl].astype(F32)
            o_ref[si, :, sl] = out.astype(o_ref.dtype)
    so_ref[...] = s_ref[...]


def _post_body(x_ref, ya_ref, yb_ref, gt_ref, hist_ref, wa_ref, wb_ref, wo_ref, g2_ref, wup_ref,
               cw_ref, cb_ref, wdn_ref, gf_ref, o_ref, tail_ref, carry_ref, *, seg, final_norm):
    dm = x_ref.shape[1]
    ua = _dot(ya_ref[...].astype(BF16), wa_ref[...])
    ub = _dot(yb_ref[...].astype(BF16), wb_ref[...])
    gates = gt_ref[...].astype(F32)
    m = (gates[:, :dm] * ua + gates[:, dm:] * ub).astype(BF16)
    x1 = x_ref[...] + _dot(m, wo_ref[...])
    h2 = _rmsnorm(x1, g2_ref[...], NORM_EPS).astype(BF16)
    up = _dot(h2, wup_ref[...])
    dff = up.shape[1] // 2
    ug = up[:, :dff]
    hist = _load_hist(hist_ref, carry_ref, seg)
    ugc = _causal_conv(ug, hist, seg, cw_ref, cb_ref)
    _store_tail(ug, tail_ref, carry_ref, seg)
    act = (_silu(ugc) * up[:, dff:]).astype(BF16)
    x2 = x1 + _dot(act, wdn_ref[...])
    if final_norm:
        x2 = _rmsnorm(x2, gf_ref[...], NORM_EPS)
    o_ref[...] = x2


def _const_spec(shape):
    nd = len(shape)
    return pl.BlockSpec(shape, lambda *_: (0,) * nd, pipeline_mode=pl.Buffered(1))


def _params(sem):
    return pltpu.CompilerParams(dimension_semantics=sem, vmem_limit_bytes=VMEM_LIMIT)


def _token_tiling(nseq, seqlen):
    if seqlen == SUBLANES:
        per = ROW_TILE // SUBLANES
        assert nseq % per == 0
        return per, SUBLANES, (nseq // per, 1)
    assert seqlen % ROW_TILE == 0
    return 1, ROW_TILE, (nseq, seqlen // ROW_TILE)


def _row_spec(cols, lt):
    return pl.BlockSpec((ROW_TILE, cols), lambda i, l: (i * lt + l, 0))


def _hist_spec(per, cols):
    return pl.BlockSpec((per, SUBLANES, cols), lambda i, l: (i, 0, 0))


def _ssm_in(x, hist, p, nseq, seqlen, act_dtype):
    per, seg, grid = _token_tiling(nseq, seqlen)
    t, dm = x.shape
    dz = p['wz'].shape[1]
    dc = p['wx'].shape[1]
    dd = p['wdt'].shape[1]
    lt = grid[1]
    return pl.pallas_call(
        functools.partial(_ssm_in_body, seg=seg),
        grid=grid,
        in_specs=[_row_spec(dm, lt), _hist_spec(per, dc), _const_spec((1, dm)),
                  _const_spec(p['wz'].shape), _const_spec(p['wx'].shape), _const_spec(p['wdt'].shape),
                  _const_spec(p['conv_w'].shape), _const_spec((1, dc)), _const_spec((1, dd))],
        out_specs=[_row_spec(dz, lt), _row_spec(dc, lt), _row_spec(dd, lt), _hist_spec(per, dc)],
        out_shape=[jax.ShapeDtypeStruct((t, dz), act_dtype), jax.ShapeDtypeStruct((t, dc), F32),
                   jax.ShapeDtypeStruct((t, dd), F32), jax.ShapeDtypeStruct((nseq, SUBLANES, dc), F32)],
        scratch_shapes=[pltpu.VMEM((SUBLANES, dc), F32)],
        compiler_params=_params(("arbitrary", "arbitrary")),
        name="ssm_in",
    )(x, hist, p['norm1_g'], p['wz'], p['wx'], p['wdt'], p['conv_w'], p['conv_b'], p['dt_bias'])


def _rwkv_in(x, hist, p, nseq, seqlen, act_dtype):
    per, seg, grid = _token_tiling(nseq, seqlen)
    t, dm = x.shape
    dim = p['w0'].shape[1]
    drw = p['wrw'].shape[1]
    dg = p['wg'].shape[1]
    lt = grid[1]
    f32_out = jax.ShapeDtypeStruct((t, dim), F32)
    return pl.pallas_call(
        functools.partial(_rwkv_in_body, seg=seg, dim=dim),
        grid=grid,
        in_specs=[_row_spec(dm, lt), _hist_spec(per, drw), _const_spec((1, dm)),
                  _const_spec(p['wrw'].shape), _const_spec(p['wg'].shape), _const_spec((1, drw)),
                  _const_spec(p['wlora'].shape), _const_spec((1, dim)), _const_spec((1, dim)),
                  _const_spec((1, dim)), _const_spec((1, dim)),
                  _const_spec(p['head_sum'].shape), _const_spec(p['head_expand'].shape)],
        out_specs=[_row_spec(dim, lt)] * 7 + [_row_spec(dg, lt), _hist_spec(per, drw)],
        out_shape=[f32_out] * 6 + [jax.ShapeDtypeStruct((t, dim), act_dtype),
                                   jax.ShapeDtypeStruct((t, dg), BF16),
                                   jax.ShapeDtypeStruct((nseq, SUBLANES, drw), F32)],
        scratch_shapes=[pltpu.VMEM((SUBLANES, drw), F32)],
        compiler_params=_params(("arbitrary", "arbitrary")),
        name="rwkv_in",
    )(x, hist, p['norm1_g'], p['wrw'], p['wg'], p['mu'], p['wlora'], p['w0'], p['a0'],
      p['k_k'], p['k_a'], p['head_sum'], p['head_expand'])


def _ssd(xc, dt, z, state, p, nseq, seqlen, act_dtype):
    t = xc.shape[0]
    _, n_heads, p_dim, state_n = state.shape
    groups = dt.shape[1] // LANES
    heads = n_heads // groups
    width = heads * p_dim
    d_inner = groups * width
    rows = SSD_CHUNK if seqlen % SSD_CHUNK == 0 else seqlen
    assert seqlen % rows == 0 and (rows == SSD_CHUNK or seqlen == rows == SUBLANES)
    nc = seqlen // rows
    assert state_n == LANES and width % LANES == 0 and p_dim * 2 == LANES
    b_blk = d_inner // state_n
    return pl.pallas_call(
        functools.partial(_ssd_body, rows=rows, heads=heads),
        grid=(nseq, groups, nc),
        in_specs=[pl.BlockSpec((rows, width), lambda b, g, c: (b * nc + c, g)),
                  pl.BlockSpec((rows, state_n), lambda b, g, c: (b * nc + c, b_blk + g)),
                  pl.BlockSpec((rows, state_n), lambda b, g, c: (b * nc + c, b_blk + groups + g)),
                  pl.BlockSpec((rows, LANES), lambda b, g, c: (b * nc + c, g)),
                  pl.BlockSpec((rows, width), lambda b, g, c: (b * nc + c, g)),
                  pl.BlockSpec((1, heads, p_dim, state_n), lambda b, g, c: (b, g, 0, 0)),
                  pl.BlockSpec((1, LANES), lambda b, g, c: (0, g)),
                  pl.BlockSpec((1, width), lambda b, g, c: (0, g)),
                  pl.BlockSpec((1, width), lambda b, g, c: (0, g)),
                  _const_spec(p['tri'].shape), _const_spec(p['ssd_expand'].shape),
                  _const_spec(p['ssd_expand2'].shape)],
        out_specs=[pl.BlockSpec((rows, width), lambda b, g, c: (b * nc + c, g)),
                   pl.BlockSpec((1, heads, p_dim, state_n), lambda b, g, c: (b, g, 0, 0))],
        out_shape=[jax.ShapeDtypeStruct((t, d_inner), act_dtype),
                   jax.ShapeDtypeStruct(state.shape, F32)],
        scratch_shapes=[pltpu.VMEM((state_n, width), F32)],
        compiler_params=_params(("arbitrary", "arbitrary", "arbitrary")),
        name="ssd",
    )(xc, xc, xc, dt, z, state, p['a_log'], p['d_exp'], p['ssm_norm_g'],
      p['tri'], p['ssd_expand'], p['ssd_expand2'])


def _wkv(r, lw, k, v, kk, bb, g, state, p, nseq, seqlen, act_dtype):
    dim = r.shape[1]
    head_dim = state.shape[1]
    npair = dim // LANES
    if seqlen == SUBLANES:
        per, steps = WKV_SAMPLE_SEQS, SUBLANES
    else:
        per, steps = (WKV_PROMPT_SEQS if nseq % WKV_PROMPT_SEQS == 0 else 1), WKV_TBLOCK
    assert nseq % per == 0 and seqlen % steps == 0
    grid = (nseq // per, seqlen // steps)
    seq_spec = pl.BlockSpec((per, steps, dim), lambda i, tb: (i, tb, 0))
    st_spec = pl.BlockSpec((per * npair, head_dim, LANES), lambda i, tb: (i, 0, 0))
    as3 = lambda a: a.reshape(nseq, seqlen, dim)
    out, s_out = pl.pallas_call(
        functools.partial(_wkv_body, head_dim=head_dim),
        grid=grid,
        in_specs=[seq_spec] * 7 + [st_spec, _const_spec((1, dim)), _const_spec((1, dim)),
                                   _const_spec((1, dim)), _const_spec((LANES, LANES))],
        out_specs=[seq_spec, st_spec],
        out_shape=[jax.ShapeDtypeStruct((nseq, seqlen, dim), act_dtype),
                   jax.ShapeDtypeStruct(state.shape, F32)],
        scratch_shapes=[pltpu.VMEM((per * npair, head_dim, LANES), F32),
                        pltpu.VMEM((per * npair, 2, steps, head_dim), F32)],
        compiler_params=_params(("arbitrary", "arbitrary")),
        name="wkv",
    )(as3(r), as3(lw), as3(k), as3(v), as3(kk), as3(bb), as3(g), state,
      p['r_k'], p['ln_w'], p['ln_b'], p['block_ones'])
    return out.reshape(nseq * seqlen, dim), s_out


def _post(x, ya, yb, gates, hist, p, nseq, seqlen, final_norm):
    per, seg, grid = _token_tiling(nseq, seqlen)
    t, dm = x.shape
    dff = p['wdn'].shape[0]
    lt = grid[1]
    return pl.pallas_call(
        functools.partial(_post_body, seg=seg, final_norm=final_norm),
        grid=grid,
        in_specs=[_row_spec(dm, lt), _row_spec(ya.shape[1], lt), _row_spec(yb.shape[1], lt),
                  _row_spec(gates.shape[1], lt), _hist_spec(per, dff),
                  _const_spec(p['wa'].shape), _const_spec(p['wb'].shape), _const_spec(p['wo'].shape),
                  _const_spec((1, dm)), _const_spec(p['wup'].shape), _const_spec(p['ffn_conv_w'].shape),
                  _const_spec((1, dff)), _const_spec(p['wdn'].shape), _const_spec((1, dm))],
        out_specs=[_row_spec(dm, lt), _hist_spec(per, dff)],
        out_shape=[jax.ShapeDtypeStruct((t, dm), F32), jax.ShapeDtypeStruct((nseq, SUBLANES, dff), F32)],
        scratch_shapes=[pltpu.VMEM((SUBLANES, dff), F32)],
        compiler_params=_params(("arbitrary", "arbitrary")),
        name="post",
    )(x, ya, yb, gates, hist, p['wa'], p['wb'], p['wo'], p['norm2_g'], p['wup'], p['ffn_conv_w'],
      p['ffn_conv_b'], p['wdn'], p['final_g'])


def _prep_layer(w, dims):
    d_inner, conv_dim, n_heads, shift_dim, groups, rwkv_dim, rwkv_heads = dims
    row = lambda a: a.reshape(1, -1).astype(F32)
    w_in = w['w_in']
    o1 = d_inner
    o2 = o1 + conv_dim
    o3 = o2 + n_heads
    o4 = o3 + shift_dim
    hpg = n_heads // groups

    def per_group(a):
        lead = a.shape[:-1]
        a = a.reshape(lead + (groups, hpg))
        a = jnp.pad(a, [(0, 0)] * len(lead) + [(0, 0), (0, LANES - hpg)])
        return a.reshape(lead + (groups * LANES,))

    p_dim = d_inner // n_heads
    head_dim = rwkv_dim // rwkv_heads
    lora_w = w['rwkv_w_up'].shape[0]
    lora_a = w['rwkv_a_up'].shape[0]
    lora_g = w['rwkv_g_up'].shape[0]
    assert lora_w == lora_a and lora_g == 2 * lora_w
    wlora = jnp.zeros((lora_w + lora_a + lora_g, 3 * rwkv_dim), F32)
    wlora = wlora.at[:lora_w, :rwkv_dim].set(w['rwkv_w_up'])
    wlora = wlora.at[lora_w:lora_w + lora_a, rwkv_dim:2 * rwkv_dim].set(w['rwkv_a_up'])
    wlora = wlora.at[lora_w + lora_a:, 2 * rwkv_dim:].set(w['rwkv_g_up'])

    ch = jnp.arange(rwkv_dim) // head_dim
    head_sum = (ch[:, None] == jnp.arange(LANES)[None, :]).astype(BF16)
    li = jnp.arange(LANES)
    width = hpg * p_dim
    ssd_expand = (li[:, None] == (jnp.arange(width) // p_dim)[None, :]).astype(BF16)
    ssd_expand2 = (li[:, None] == (jnp.arange(hpg * SSD_CHUNK) // SSD_CHUNK)[None, :]).astype(BF16)
    ci = jnp.arange(SSD_CHUNK)
    tri = (ci[:, None] >= ci[None, :]).astype(BF16)
    block_ones = ((li[:, None] // head_dim) == (li[None, :] // head_dim)).astype(BF16)

    return {
        'norm1_g': row(w['norm1_g']),
        'wz': w_in[:, :o1].astype(BF16),
        'wx': w_in[:, o1:o2].astype(BF16),
        'wdt': per_group(w_in[:, o2:o3]).astype(BF16),
        'wrw': w_in[:, o3:o4].astype(BF16),
        'wg': w_in[:, o4:].astype(BF16),
        'conv_w': w['ssm_conv_w'].astype(F32),
        'conv_b': row(w['ssm_conv_b']),
        'dt_bias': row(per_group(w['ssm_dt_bias'])),
        'a_log': row(per_group(w['ssm_a_log'])),
        'd_exp': row(jnp.repeat(w['ssm_d'], p_dim)),
        'ssm_norm_g': row(w['ssm_norm_g']),
        'wa': w['w_branch_a'].astype(BF16),
        'mu': row(w['rwkv_mu']),
        'wlora': wlora.astype(BF16),
        'w0': row(w['rwkv_w0']),
        'a0': row(w['rwkv_a0']),
        'k_k': row(w['rwkv_k_k']),
        'k_a': row(w['rwkv_k_a']),
        'r_k': row(w['rwkv_r_k']),
        'ln_w': row(w['rwkv_ln_w']),
        'ln_b': row(w['rwkv_ln_b']),
        'wb': w['w_branch_b'].astype(BF16),
        'wo': w['w_out'].astype(BF16),
        'norm2_g': row(w['norm2_g']),
        'wup': w['ffn_w_up'].astype(BF16),
        'ffn_conv_w': w['ffn_conv_w'].astype(F32),
        'ffn_conv_b': row(w['ffn_conv_b']),
        'wdn': w['ffn_w_down'].astype(BF16),
        'head_sum': head_sum,
        'head_expand': head_sum.T,
        'ssd_expand': ssd_expand,
        'ssd_expand2': ssd_expand2,
        'tri': tri,
        'block_ones': block_ones,
    }


def _hist8(state_rows):
    k = state_rows.shape[1]
    return jnp.pad(state_rows.astype(F32), ((0, 0), (SUBLANES - k, 0), (0, 0)))


def _pair_heads(state):
    nseq, nh, dv, dk = state.shape
    s = state.astype(F32).reshape(nseq, nh // 2, 2, dv, dk)
    return jnp.transpose(s, (0, 1, 3, 2, 4)).reshape(nseq * (nh // 2), dv, 2 * dk)


def _unpair_heads(state, nseq):
    npair, dv, dk2 = state.shape[0] // nseq, state.shape[1], state.shape[2]
    s = state.reshape(nseq, npair, dv, 2, dk2 // 2)
    return jnp.transpose(s, (0, 1, 3, 2, 4)).reshape(nseq, npair * 2, dv, dk2 // 2)


def _layer(x, conv_buf, ssm_state, shift_buf, wkv_state, ffn_buf, p, final_norm):
    nseq, seqlen, dm = x.shape
    act_dtype = BF16 if seqlen % 16 == 0 else F32
    xf = x.reshape(nseq * seqlen, dm)
    z, xc, dt, conv_tail = _ssm_in(xf, _hist8(conv_buf), p, nseq, seqlen, act_dtype)
    r, lw, k, v, kk, bb, g, gates, shift_tail = _rwkv_in(xf, _hist8(shift_buf[:, None]), p, nseq, seqlen,
                                                         act_dtype)
    ya, new_ssm = _ssd(xc, dt, z, ssm_state.astype(F32), p, nseq, seqlen, act_dtype)
    yb, wkv_pairs = _wkv(r, lw, k, v, kk, bb, g, _pair_heads(wkv_state), p, nseq, seqlen, act_dtype)
    out, ffn_tail = _post(xf, ya, yb, gates, _hist8(ffn_buf), p, nseq, seqlen, final_norm)
    new_conv = conv_tail[:, SUBLANES - conv_buf.shape[1]:]
    new_shift = shift_tail[:, SUBLANES - 1]
    new_ffn = ffn_tail[:, SUBLANES - ffn_buf.shape[1]:]
    new_wkv = _unpair_heads(wkv_pairs, nseq)
    return out.reshape(nseq, seqlen, dm), (new_conv, new_ssm, new_shift, new_wkv, new_ffn)


_LAYER_WEIGHTS = ('norm1_g', 'w_in', 'ssm_conv_w', 'ssm_conv_b', 'ssm_dt_bias', 'ssm_a_log', 'ssm_d',
                  'ssm_norm_g', 'w_branch_a', 'rwkv_mu', 'rwkv_w0', 'rwkv_w_up', 'rwkv_a0', 'rwkv_a_up',
                  'rwkv_g_up', 'rwkv_k_k', 'rwkv_k_a', 'rwkv_r_k', 'rwkv_ln_w', 'rwkv_ln_b', 'w_branch_b',
                  'w_out', 'norm2_g', 'ffn_w_up', 'ffn_conv_w', 'ffn_conv_b', 'ffn_w_down')


def kernel(x_prompt, x_sample, state_ssm_conv, state_ssm, state_rwkv_shift, state_rwkv, state_ffn_conv,
           norm1_g, w_in, ssm_conv_w, ssm_conv_b, ssm_dt_bias, ssm_a_log, ssm_d, ssm_norm_g, w_branch_a,
           rwkv_mu, rwkv_w0, rwkv_w_up, rwkv_a0, rwkv_a_up, rwkv_g_up, rwkv_k_k, rwkv_k_a, rwkv_r_k,
           rwkv_ln_w, rwkv_ln_b, w_branch_b, w_out, norm2_g, ffn_w_up, ffn_conv_w, ffn_conv_b, ffn_w_down,
           final_g):
    stacked = dict(zip(_LAYER_WEIGHTS, (
        norm1_g, w_in, ssm_conv_w, ssm_conv_b, ssm_dt_bias, ssm_a_log, ssm_d, ssm_norm_g, w_branch_a,
        rwkv_mu, rwkv_w0, rwkv_w_up, rwkv_a0, rwkv_a_up, rwkv_g_up, rwkv_k_k, rwkv_k_a, rwkv_r_k,
        rwkv_ln_w, rwkv_ln_b, w_branch_b, w_out, norm2_g, ffn_w_up, ffn_conv_w, ffn_conv_b, ffn_w_down)))
    depth = w_in.shape[0]
    _, _, n_heads, _, _ = state_ssm.shape
    conv_dim = state_ssm_conv.shape[-1]
    d_inner = w_branch_a.shape[1]
    shift_dim = state_rwkv_shift.shape[-1]
    rwkv_heads = state_rwkv.shape[2]
    rwkv_dim = w_branch_b.shape[1]
    groups = (conv_dim - d_inner) // (2 * state_ssm.shape[-1])
    dims = (d_inner, conv_dim, n_heads, shift_dim, groups, rwkv_dim, rwkv_heads)

    xp, xs = x_prompt, x_sample
    bp = xp.shape[0]
    new_p = ([], [], [], [], [])
    new_s = ([], [], [], [], [])
    for i in range(depth):
        p = _prep_layer({name: a[i] for name, a in stacked.items()}, dims)
        p['final_g'] = final_g.reshape(1, -1).astype(F32)
        last = i == depth - 1
        xp, sp = _layer(
            xp,
            jnp.zeros((bp,) + state_ssm_conv.shape[2:], F32),
            jnp.zeros((bp,) + state_ssm.shape[2:], F32),
            jnp.zeros((bp,) + state_rwkv_shift.shape[2:], F32),
            jnp.zeros((bp,) + state_rwkv.shape[2:], F32),
            jnp.zeros((bp,) + state_ffn_conv.shape[2:], F32),
            p, last)
        xs, ss = _layer(xs, state_ssm_conv[i], state_ssm[i], state_rwkv_shift[i], state_rwkv[i],
                        state_ffn_conv[i], p, last)
        for j in range(5):
            new_p[j].append(sp[j])
            new_s[j].append(ss[j])
    return (xp, xs,
            jnp.stack(new_p[0]), jnp.stack(new_p[1]), jnp.stack(new_p[2]), jnp.stack(new_p[3]),
            jnp.stack(new_p[4]),
            jnp.stack(new_s[0]), jnp.stack(new_s[1]), jnp.stack(new_s[2]), jnp.stack(new_s[3]),
            jnp.stack(new_s[4]))
```

```python
import functools

import jax
import jax.numpy as jnp
from jax import lax
from jax.experimental import pallas as pl
from jax.experimental.pallas import tpu as pltpu

F32 = jnp.float32
BF16 = jnp.bfloat16

NORM_EPS = 1e-5
GN_EPS = 64e-5

LANES = 128
SUBLANES = 8
ROW_TILE = 256
SSD_CHUNK = 128
WKV_TBLOCK = 64
WKV_PROMPT_SEQS = 2
WKV_SAMPLE_SEQS = 4
VMEM_LIMIT = 56 * 1024 * 1024


def _dot(a, b):
    return jnp.dot(a, b, preferred_element_type=F32)


def _split(x, n):
    parts = []
    rem = x
    for i in range(n):
        p = rem.astype(BF16)
        parts.append(p)
        if i + 1 < n:
            rem = rem - p.astype(F32)
    return parts


def _dot_split_lhs(x, m, n):
    acc = None
    for p in _split(x, n):
        d = _dot(p, m)
        acc = d if acc is None else acc + d
    return acc


def _dot_split_rhs(m, x, n):
    acc = None
    for p in _split(x, n):
        d = _dot(m, p)
        acc = d if acc is None else acc + d
    return acc


def _sigmoid(x):
    return 1.0 / (1.0 + jnp.exp(-x))


def _silu(x):
    return x * _sigmoid(x)


def _softplus(x):
    return jnp.maximum(x, 0.0) + jnp.log1p(jnp.exp(-jnp.abs(x)))


def _rmsnorm(x, g, eps):
    ms = jnp.mean(x * x, axis=-1, keepdims=True)
    return x * lax.rsqrt(ms + eps) * g


def _shift_rows(u, hist, seg, j):
    rows = u.shape[0]
    ru = pltpu.roll(u, j, 0)
    if seg == SUBLANES:
        rh = pltpu.roll(hist, (rows - SUBLANES + j) % rows, 0)
        pos = lax.broadcasted_iota(jnp.int32, u.shape, 0) % SUBLANES
        return jnp.where(pos < j, rh, ru)
    assert seg == rows
    rh = pltpu.roll(hist, j, 0)
    pos = lax.broadcasted_iota(jnp.int32, rh.shape, 0)
    top = jnp.where(pos < j, rh, ru[:SUBLANES])
    return jnp.concatenate([top, ru[SUBLANES:]], axis=0)


def _causal_conv(u, hist, seg, w_ref, b_ref):
    taps = w_ref.shape[0]
    acc = u * w_ref[taps - 1:taps, :] + b_ref[...]
    for j in range(1, taps):
        acc = acc + _shift_rows(u, hist, seg, j) * w_ref[taps - 1 - j:taps - j, :]
    return acc


def _load_hist(hist_ref, carry_ref, seg):
    if seg == SUBLANES:
        nseq, _, c = hist_ref.shape
        return hist_ref[...].reshape(nseq * SUBLANES, c)

    @pl.when(pl.program_id(1) == 0)
    def _():
        carry_ref[...] = hist_ref[0]

    return carry_ref[...]


def _store_tail(u, tail_ref, carry_ref, seg):
    if seg == SUBLANES:
        tail_ref[...] = u.reshape(tail_ref.shape)
    else:
        last = u[u.shape[0] - SUBLANES:]
        carry_ref[...] = last
        tail_ref[0] = last


def _ssm_in_body(x_ref, hist_ref, g1_ref, wz_ref, wx_ref, wdt_ref, cw_ref, cb_ref, dtb_ref,
                 z_ref, xc_ref, dt_ref, tail_ref, carry_ref, *, seg):
    h = _rmsnorm(x_ref[...], g1_ref[...], NORM_EPS).astype(BF16)
    z_ref[...] = _dot(h, wz_ref[...]).astype(z_ref.dtype)
    u = _dot(h, wx_ref[...])
    hist = _load_hist(hist_ref, carry_ref, seg)
    xc_ref[...] = _silu(_causal_conv(u, hist, seg, cw_ref, cb_ref))
    _store_tail(u, tail_ref, carry_ref, seg)
    dt_ref[...] = _softplus(_dot(h, wdt_ref[...]) + dtb_ref[...])


def _rwkv_in_body(x_ref, hist_ref, g1_ref, wrw_ref, wg_ref, mu_ref, wlora_ref, w0_ref, a0_ref,
                  kk_ref, ka_ref, seg_ref, exp_ref,
                  r_out, lw_out, k_out, v_out, kkn_out, bb_out, g_out, gates_out, tail_ref,
                  carry_ref, *, seg, dim):
    h = _rmsnorm(x_ref[...], g1_ref[...], NORM_EPS).astype(BF16)
    gates_out[...] = _sigmoid(_dot(h, wg_ref[...])).astype(gates_out.dtype)
    u = _dot(h, wrw_ref[...])
    hist = _load_hist(hist_ref, carry_ref, seg)
    prev = _shift_rows(u, hist, seg, 1)
    _store_tail(u, tail_ref, carry_ref, seg)
    mix = u + (prev - u) * mu_ref[...]
    r = mix[:, 0:dim]
    k = mix[:, dim:2 * dim]
    v = mix[:, 2 * dim:3 * dim]
    low = mix[:, 3 * dim:]
    lane = lax.broadcasted_iota(jnp.int32, low.shape, 1)
    lo_w = low.shape[1] // 4
    act = jnp.where(lane < lo_w, jnp.tanh(low), jnp.where(lane < 2 * lo_w, low, _sigmoid(low)))
    up = _dot(act.astype(BF16), wlora_ref[...])
    wlog = -_softplus(-(w0_ref[...] + up[:, 0:dim])) - 0.5
    a = _sigmoid(a0_ref[...] + up[:, dim:2 * dim])
    kkr = k * kk_ref[...]
    ss = _dot_split_lhs(kkr * kkr, seg_ref[...], 2)
    inv = 1.0 / jnp.maximum(jnp.sqrt(ss), 1e-12)
    kkn = kkr * _dot_split_lhs(inv, exp_ref[...], 3)
    r_out[...] = r
    lw_out[...] = -jnp.exp(wlog)
    k_out[...] = k * (1.0 + (a - 1.0) * ka_ref[...])
    v_out[...] = v
    kkn_out[...] = kkn
    bb_out[...] = kkn * a
    g_out[...] = up[:, 2 * dim:3 * dim].astype(g_out.dtype)


def _ssd_body(xm_ref, b_ref, c_ref, dt_ref, z_ref, st_ref, alog_ref, dexp_ref, ng_ref,
              tri_ref, e_ref, e2_ref, y_ref, so_ref, ht_ref, *, rows, heads):
    chunk = pl.program_id(2)
    n_chunks = pl.num_programs(2)
    R = SSD_CHUNK

    def pad(v):
        if rows == R:
            return v
        return jnp.concatenate([v, jnp.zeros((R - rows, v.shape[1]), v.dtype)], axis=0)

    xm = pad(xm_ref[...])
    bm = pad(b_ref[...])
    cm = pad(c_ref[...])
    dt = pad(dt_ref[...])
    z = pad(z_ref[...].astype(F32))
    width = xm.shape[1]
    state_n = bm.shape[1]

    @pl.when(chunk == 0)
    def _():
        ht_ref[...] = st_ref[0].reshape(width, state_n).T

    a = dt * (-jnp.exp(alog_ref[...]))
    cs = _dot_split_rhs(tri_ref[...], a, 3)
    expand = e_ref[...]
    dt_e = _dot_split_lhs(dt, expand, 3)
    cs_e = _dot_split_lhs(cs, expand, 3)
    cl_e = cs_e[R - 1:R, :]
    xdt = xm * dt_e
    xs = xdt * jnp.exp(cl_e - cs_e)
    bt = bm.T.astype(BF16)
    cmb = cm.astype(BF16)
    st_new = _dot(bt, xs.astype(BF16))
    ht = ht_ref[...]
    y_off = _dot(cmb, ht.astype(BF16)) * jnp.exp(cs_e)
    cb = _dot(cmb, bt)
    cs_t = cs.T
    cs_e2 = _dot_split_lhs(cs, e2_ref[...], 3)
    row = lax.broadcasted_iota(jnp.int32, (R, R), 0)
    col = lax.broadcasted_iota(jnp.int32, (R, R), 1)
    causal = row >= col
    lane = lax.broadcasted_iota(jnp.int32, (R, LANES), 1)
    p_dim = width // heads
    ys = []
    for j in range(heads // 2):
        ms = []
        for hh in (2 * j, 2 * j + 1):
            seg_ = cs_e2[:, hh * R:(hh + 1) * R] - cs_t[hh:hh + 1, :]
            ms.append(jnp.where(causal, cb * jnp.exp(jnp.where(causal, seg_, 0.0)), 0.0).astype(BF16))
        lhs = jnp.concatenate(ms, axis=1)
        xp = xdt[:, j * LANES:(j + 1) * LANES]
        rhs = jnp.concatenate([jnp.where(lane < p_dim, xp, 0.0), jnp.where(lane >= p_dim, xp, 0.0)],
                              axis=0).astype(BF16)
        ys.append(_dot(lhs, rhs))
    y = jnp.concatenate(ys, axis=1) + y_off
    ht_ref[...] = ht * jnp.exp(cl_e) + st_new

    y = y + dexp_ref[...] * xm
    yz = y * _silu(z)
    yn = yz * lax.rsqrt(jnp.mean(yz * yz, axis=-1, keepdims=True) + NORM_EPS) * ng_ref[...]
    y_ref[...] = yn[:rows].astype(y_ref.dtype)

    @pl.when(chunk == n_chunks - 1)
    def _():
        so_ref[0] = ht_ref[...].T.reshape(so_ref.shape[1:])


def _wkv_body(r_ref, lw_ref, k_ref, v_ref, kk_ref, bb_ref, g_ref, s0_ref, rk_ref, lnw_ref, lnb_ref,
              bo_ref, tri_ref, o_ref, so_ref, s_ref, *, head_dim):
    nseq, steps, dim = r_ref.shape
    npair = dim // LANES
    nchunk = steps // SUBLANES
    block_ones = bo_ref[...]
    tri = tri_ref[...]

    @pl.when(pl.program_id(1) == 0)
    def _():
        s_ref[...] = s0_ref[...]

    lane8 = lax.broadcasted_iota(jnp.int32, (SUBLANES, LANES), 1)
    row8 = lax.broadcasted_iota(jnp.int32, (SUBLANES, LANES), 0)
    head0 = lane8 < head_dim
    rowi = lax.broadcasted_iota(jnp.int32, (head_dim, LANES), 0)
    lanei = lax.broadcasted_iota(jnp.int32, (head_dim, LANES), 1)
    diag = (rowi == lanei % head_dim).astype(F32)

    def other_head(a):
        return pltpu.roll(a, head_dim, 1)

    def rows(a, c):
        return a[c * SUBLANES:(c + 1) * SUBLANES]

    def bc(tile, i):
        return jnp.broadcast_to(tile[i:i + 1], (SUBLANES, LANES))

    chains = []
    for si in range(nseq):
        for p in range(npair):
            sl = slice(p * LANES, (p + 1) * LANES)
            lw = lw_ref[si, :, sl]
            cum = _dot_split_rhs(tri, lw, 3)
            p_in = jnp.exp(cum)
            p_inv = jnp.exp(-cum)
            at = kk_ref[si, :, sl] * jnp.exp(cum - lw)
            rt = r_ref[si, :, sl] * p_in
            bt = bb_ref[si, :, sl] * p_inv
            kt = k_ref[si, :, sl] * p_inv
            at_o, rt_o, bt_o, kt_o = other_head(at), other_head(rt), other_head(bt), other_head(kt)
            x4 = jnp.concatenate([rows(a, c) for c in range(nchunk) for a in (bt, bt_o, kt, kt_o)], axis=0)
            if x4.shape[0] < LANES:
                x4 = jnp.concatenate([x4, jnp.zeros((LANES - x4.shape[0], LANES), F32)], axis=0)
            xt = x4.T[:head_dim].astype(BF16)
            chains.append(dict(si=si, sl=sl, at=at, rt=rt, bt=bt, kt=kt, at_o=at_o, rt_o=rt_o, xt=xt,
                               v=v_ref[si, :, sl], p_end=p_in[steps - 1:steps], s=s_ref[si * npair + p],
                               ys=[]))

    for c in range(nchunk):
        for ch in chains:
            at_c, rt_c, bt_c, kt_c = rows(ch['at'], c), rows(ch['rt'], c), rows(ch['bt'], c), rows(ch['kt'], c)
            tiles = []
            for i in range(SUBLANES):
                am = jnp.where(row8 > i, at_c, 0.0)
                rm = jnp.where(row8 >= i, rt_c, 0.0)
                bi, ki = bc(bt_c, i), bc(kt_c, i)
                tiles += [am * bi, am * ki, rm * bi, rm * ki]
            coef = _dot(jnp.concatenate(tiles, axis=0).astype(BF16), block_ones)
            ch['coef'] = [rows(coef, n) for n in range(4 * SUBLANES)]
            v_c = rows(ch['v'], c)
            va = None
            yv = None
            for i in range(SUBLANES):
                vi = bc(v_c, i)
                t_ak = ch['coef'][4 * i + 1] * vi
                t_rk = ch['coef'][4 * i + 3] * vi
                va = t_ak if va is None else va + t_ak
                yv = t_rk if yv is None else yv + t_rk
            ch['va'], ch['yv'], ch['v_c'] = va, yv, v_c
            lhs = jnp.concatenate([at_c[:, :head_dim], rows(ch['at_o'], c)[:, :head_dim],
                                   rt_c[:, :head_dim], rows(ch['rt_o'], c)[:, :head_dim]], axis=0)
            ch['lhs'] = lhs.astype(BF16)
        for ch in chains:
            ch['g'] = _dot(ch['lhs'], ch['s'].astype(BF16))
        for ch in chains:
            g = ch['g']
            g_a = jnp.where(head0, rows(g, 0), rows(g, 1))
            g_r = jnp.where(head0, rows(g, 2), rows(g, 3))
            sa = g_a + ch['va']
            y = g_r + ch['yv']
            for i in range(SUBLANES):
                sai = bc(sa, i)
                if i + 1 < SUBLANES:
                    sa = sa - ch['coef'][4 * i] * sai
                y = y - ch['coef'][4 * i + 2] * sai
            ch['ys'].append(y)
            v_c = ch['v_c']
            wd = jnp.concatenate([jnp.where(head0, -sa, 0.0), jnp.where(head0, 0.0, -sa),
                                  jnp.where(head0, v_c, 0.0), jnp.where(head0, 0.0, v_c)], axis=0)
            ch['wd'] = wd.astype(BF16)
        for ch in chains:
            cols = ch['xt'][:, c * 4 * SUBLANES:(c + 1) * 4 * SUBLANES]
            ch['s'] = ch['s'] + _dot(cols, ch['wd'])

    inv_n = 1.0 / head_dim
    for n, ch in enumerate(chains):
        si, sl = ch['si'], ch['sl']
        p_col = _dot_split_lhs(diag * ch['p_end'], block_ones, 3)
        s_new = ch['s'] * p_col
        s_ref[n] = s_new
        so_ref[n] = s_new
        y = jnp.concatenate(ch['ys'], axis=0)
        mu = _dot_split_lhs(y, block_ones, 2) * inv_n
        d = y - mu
        var = _dot_split_lhs(d * d, block_ones, 2) * inv_n
        yn = d * lax.rsqrt(var + GN_EPS) * lnw_ref[:, sl] + lnb_ref[:, sl]
        bonus = _dot_split_lhs(r_ref[si, :, sl] * k_ref[si, :, sl] * rk_ref[:, sl], block_ones, 2)
        out = (yn + bonus * ch['v']) * g_ref[si, :, sl].astype(F32)
        o_ref[si, :, sl] = out.astype(o_ref.dtype)


def _post_body(x_ref, ya_ref, yb_ref, gt_ref, hist_ref, wa_ref, wb_ref, wo_ref, g2_ref, wup_ref,
               cw_ref, cb_ref, wdn_ref, gf_ref, o_ref, tail_ref, carry_ref, *, seg, final_norm):
    dm = x_ref.shape[1]
    ua = _dot(ya_ref[...].astype(BF16), wa_ref[...])
    ub = _dot(yb_ref[...].astype(BF16), wb_ref[...])
    gates = gt_ref[...].astype(F32)
    m = (gates[:, :dm] * ua + gates[:, dm:] * ub).astype(BF16)
    x1 = x_ref[...] + _dot(m, wo_ref[...])
    h2 = _rmsnorm(x1, g2_ref[...], NORM_EPS).astype(BF16)
    up = _dot(h2, wup_ref[...])
    dff = up.shape[1] // 2
    ug = up[:, :dff]
    hist = _load_hist(hist_ref, carry_ref, seg)
    ugc = _causal_conv(ug, hist, seg, cw_ref, cb_ref)
    _store_tail(ug, tail_ref, carry_ref, seg)
    act = (_silu(ugc) * up[:, dff:]).astype(BF16)
    x2 = x1 + _dot(act, wdn_ref[...])
    if final_norm:
        x2 = _rmsnorm(x2, gf_ref[...], NORM_EPS)
    o_ref[...] = x2


def _const_spec(shape):
    nd = len(shape)
    return pl.BlockSpec(shape, lambda *_: (0,) * nd, pipeline_mode=pl.Buffered(1))


def _params(sem):
    return pltpu.CompilerParams(dimension_semantics=sem, vmem_limit_bytes=VMEM_LIMIT)


def _token_tiling(nseq, seqlen):
    if seqlen == SUBLANES:
        per = ROW_TILE // SUBLANES
        assert nseq % per == 0
        return per, SUBLANES, (nseq // per, 1)
    assert seqlen % ROW_TILE == 0
    return 1, ROW_TILE, (nseq, seqlen // ROW_TILE)


def _row_spec(cols, lt):
    return pl.BlockSpec((ROW_TILE, cols), lambda i, l: (i * lt + l, 0))


def _hist_spec(per, cols):
    return pl.BlockSpec((per, SUBLANES, cols), lambda i, l: (i, 0, 0))


def _ssm_in(x, hist, p, nseq, seqlen, act_dtype):
    per, seg, grid = _token_tiling(nseq, seqlen)
    t, dm = x.shape
    dz = p['wz'].shape[1]
    dc = p['wx'].shape[1]
    dd = p['wdt'].shape[1]
    lt = grid[1]
    return pl.pallas_call(
        functools.partial(_ssm_in_body, seg=seg),
        grid=grid,
        in_specs=[_row_spec(dm, lt), _hist_spec(per, dc), _const_spec((1, dm)),
                  _const_spec(p['wz'].shape), _const_spec(p['wx'].shape), _const_spec(p['wdt'].shape),
                  _const_spec(p['conv_w'].shape), _const_spec((1, dc)), _const_spec((1, dd))],
        out_specs=[_row_spec(dz, lt), _row_spec(dc, lt), _row_spec(dd, lt), _hist_spec(per, dc)],
        out_shape=[jax.ShapeDtypeStruct((t, dz), act_dtype), jax.ShapeDtypeStruct((t, dc), F32),
                   jax.ShapeDtypeStruct((t, dd), F32), jax.ShapeDtypeStruct((nseq, SUBLANES, dc), F32)],
        scratch_shapes=[pltpu.VMEM((SUBLANES, dc), F32)],
        compiler_params=_params(("arbitrary", "arbitrary")),
        name="ssm_in",
    )(x, hist, p['norm1_g'], p['wz'], p['wx'], p['wdt'], p['conv_w'], p['conv_b'], p['dt_bias'])


def _rwkv_in(x, hist, p, nseq, seqlen, act_dtype):
    per, seg, grid = _token_tiling(nseq, seqlen)
    t, dm = x.shape
    dim = p['w0'].shape[1]
    drw = p['wrw'].shape[1]
    dg = p['wg'].shape[1]
    lt = grid[1]
    f32_out = jax.ShapeDtypeStruct((t, dim), F32)
    return pl.pallas_call(
        functools.partial(_rwkv_in_body, seg=seg, dim=dim),
        grid=grid,
        in_specs=[_row_spec(dm, lt), _hist_spec(per, drw), _const_spec((1, dm)),
                  _const_spec(p['wrw'].shape), _const_spec(p['wg'].shape), _const_spec((1, drw)),
                  _const_spec(p['wlora'].shape), _const_spec((1, dim)), _const_spec((1, dim)),
                  _const_spec((1, dim)), _const_spec((1, dim)),
                  _const_spec(p['head_sum'].shape), _const_spec(p['head_expand'].shape)],
        out_specs=[_row_spec(dim, lt)] * 7 + [_row_spec(dg, lt), _hist_spec(per, drw)],
        out_shape=[f32_out] * 6 + [jax.ShapeDtypeStruct((t, dim), act_dtype),
                                   jax.ShapeDtypeStruct((t, dg), BF16),
                                   jax.ShapeDtypeStruct((nseq, SUBLANES, drw), F32)],
        scratch_shapes=[pltpu.VMEM((SUBLANES, drw), F32)],
        compiler_params=_params(("arbitrary", "arbitrary")),
        name="rwkv_in",
    )(x, hist, p['norm1_g'], p['wrw'], p['wg'], p['mu'], p['wlora'], p['w0'], p['a0'],
      p['k_k'], p['k_a'], p['head_sum'], p['head_expand'])


def _ssd(xc, dt, z, state, p, nseq, seqlen, act_dtype):
    t = xc.shape[0]
    _, n_heads, p_dim, state_n = state.shape
    groups = dt.shape[1] // LANES
    heads = n_heads // groups
    width = heads * p_dim
    d_inner = groups * width
    rows = SSD_CHUNK if seqlen % SSD_CHUNK == 0 else seqlen
    assert seqlen % rows == 0 and (rows == SSD_CHUNK or seqlen == rows == SUBLANES)
    nc = seqlen // rows
    assert state_n == LANES and width % LANES == 0 and p_dim * 2 == LANES
    b_blk = d_inner // state_n
    return pl.pallas_call(
        functools.partial(_ssd_body, rows=rows, heads=heads),
        grid=(nseq, groups, nc),
        in_specs=[pl.BlockSpec((rows, width), lambda b, g, c: (b * nc + c, g)),
                  pl.BlockSpec((rows, state_n), lambda b, g, c: (b * nc + c, b_blk + g)),
                  pl.BlockSpec((rows, state_n), lambda b, g, c: (b * nc + c, b_blk + groups + g)),
                  pl.BlockSpec((rows, LANES), lambda b, g, c: (b * nc + c, g)),
                  pl.BlockSpec((rows, width), lambda b, g, c: (b * nc + c, g)),
                  pl.BlockSpec((1, heads, p_dim, state_n), lambda b, g, c: (b, g, 0, 0)),
                  pl.BlockSpec((1, LANES), lambda b, g, c: (0, g)),
                  pl.BlockSpec((1, width), lambda b, g, c: (0, g)),
                  pl.BlockSpec((1, width), lambda b, g, c: (0, g)),
                  _const_spec(p['tri'].shape), _const_spec(p['ssd_expand'].shape),
                  _const_spec(p['ssd_expand2'].shape)],
        out_specs=[pl.BlockSpec((rows, width), lambda b, g, c: (b * nc + c, g)),
                   pl.BlockSpec((1, heads, p_dim, state_n), lambda b, g, c: (b, g, 0, 0))],
        out_shape=[jax.ShapeDtypeStruct((t, d_inner), act_dtype),
                   jax.ShapeDtypeStruct(state.shape, F32)],
        scratch_shapes=[pltpu.VMEM((state_n, width), F32)],
        compiler_params=_params(("arbitrary", "arbitrary", "arbitrary")),
        name="ssd",
    )(xc, xc, xc, dt, z, state, p['a_log'], p['d_exp'], p['ssm_norm_g'],
      p['tri'], p['ssd_expand'], p['ssd_expand2'])


def _wkv(r, lw, k, v, kk, bb, g, state, p, nseq, seqlen, act_dtype):
    dim = r.shape[1]
    head_dim = state.shape[1]
    npair = dim // LANES
    if seqlen == SUBLANES:
        per, steps = WKV_SAMPLE_SEQS, SUBLANES
    else:
        per, steps = (WKV_PROMPT_SEQS if nseq % WKV_PROMPT_SEQS == 0 else 1), WKV_TBLOCK
    assert nseq % per == 0 and seqlen % steps == 0
    grid = (nseq // per, seqlen // steps)
    seq_spec = pl.BlockSpec((per, steps, dim), lambda i, tb: (i, tb, 0))
    st_spec = pl.BlockSpec((per * npair, head_dim, LANES), lambda i, tb: (i, 0, 0))
    as3 = lambda a: a.reshape(nseq, seqlen, dim)
    out, s_out = pl.pallas_call(
        functools.partial(_wkv_body, head_dim=head_dim),
        grid=grid,
        in_specs=[seq_spec] * 7 + [st_spec, _const_spec((1, dim)), _const_spec((1, dim)),
                                   _const_spec((1, dim)), _const_spec((LANES, LANES)),
                                   _const_spec((steps, steps))],
        out_specs=[seq_spec, st_spec],
        out_shape=[jax.ShapeDtypeStruct((nseq, seqlen, dim), act_dtype),
                   jax.ShapeDtypeStruct(state.shape, F32)],
        scratch_shapes=[pltpu.VMEM((per * npair, head_dim, LANES), F32)],
        compiler_params=_params(("arbitrary", "arbitrary")),
        name="wkv",
    )(as3(r), as3(lw), as3(k), as3(v), as3(kk), as3(bb), as3(g), state,
      p['r_k'], p['ln_w'], p['ln_b'], p['block_ones'], p['tri'][:steps, :steps])
    return out.reshape(nseq * seqlen, dim), s_out


def _post(x, ya, yb, gates, hist, p, nseq, seqlen, final_norm):
    per, seg, grid = _token_tiling(nseq, seqlen)
    t, dm = x.shape
    dff = p['wdn'].shape[0]
    lt = grid[1]
    return pl.pallas_call(
        functools.partial(_post_body, seg=seg, final_norm=final_norm),
        grid=grid,
        in_specs=[_row_spec(dm, lt), _row_spec(ya.shape[1], lt), _row_spec(yb.shape[1], lt),
                  _row_spec(gates.shape[1], lt), _hist_spec(per, dff),
                  _const_spec(p['wa'].shape), _const_spec(p['wb'].shape), _const_spec(p['wo'].shape),
                  _const_spec((1, dm)), _const_spec(p['wup'].shape), _const_spec(p['ffn_conv_w'].shape),
                  _const_spec((1, dff)), _const_spec(p['wdn'].shape), _const_spec((1, dm))],
        out_specs=[_row_spec(dm, lt), _hist_spec(per, dff)],
        out_shape=[jax.ShapeDtypeStruct((t, dm), F32), jax.ShapeDtypeStruct((nseq, SUBLANES, dff), F32)],
        scratch_shapes=[pltpu.VMEM((SUBLANES, dff), F32)],
        compiler_params=_params(("arbitrary", "arbitrary")),
        name="post",
    )(x, ya, yb, gates, hist, p['wa'], p['wb'], p['wo'], p['norm2_g'], p['wup'], p['ffn_conv_w'],
      p['ffn_conv_b'], p['wdn'], p['final_g'])


def _prep_layer(w, dims):
    d_inner, conv_dim, n_heads, shift_dim, groups, rwkv_dim, rwkv_heads = dims
    row = lambda a: a.reshape(1, -1).astype(F32)
    w_in = w['w_in']
    o1 = d_inner
    o2 = o1 + conv_dim
    o3 = o2 + n_heads
    o4 = o3 + shift_dim
    hpg = n_heads // groups

    def per_group(a):
        lead = a.shape[:-1]
        a = a.reshape(lead + (groups, hpg))
        a = jnp.pad(a, [(0, 0)] * len(lead) + [(0, 0), (0, LANES - hpg)])
        return a.reshape(lead + (groups * LANES,))

    p_dim = d_inner // n_heads
    head_dim = rwkv_dim // rwkv_heads
    lora_w = w['rwkv_w_up'].shape[0]
    lora_a = w['rwkv_a_up'].shape[0]
    lora_g = w['rwkv_g_up'].shape[0]
    assert lora_w == lora_a and lora_g == 2 * lora_w
    wlora = jnp.zeros((lora_w + lora_a + lora_g, 3 * rwkv_dim), F32)
    wlora = wlora.at[:lora_w, :rwkv_dim].set(w['rwkv_w_up'])
    wlora = wlora.at[lora_w:lora_w + lora_a, rwkv_dim:2 * rwkv_dim].set(w['rwkv_a_up'])
    wlora = wlora.at[lora_w + lora_a:, 2 * rwkv_dim:].set(w['rwkv_g_up'])

    ch = jnp.arange(rwkv_dim) // head_dim
    head_sum = (ch[:, None] == jnp.arange(LANES)[None, :]).astype(BF16)
    li = jnp.arange(LANES)
    width = hpg * p_dim
    ssd_expand = (li[:, None] == (jnp.arange(width) // p_dim)[None, :]).astype(BF16)
    ssd_expand2 = (li[:, None] == (jnp.arange(hpg * SSD_CHUNK) // SSD_CHUNK)[None, :]).astype(BF16)
    ci = jnp.arange(SSD_CHUNK)
    tri = (ci[:, None] >= ci[None, :]).astype(BF16)
    block_ones = ((li[:, None] // head_dim) == (li[None, :] // head_dim)).astype(BF16)

    return {
        'norm1_g': row(w['norm1_g']),
        'wz': w_in[:, :o1].astype(BF16),
        'wx': w_in[:, o1:o2].astype(BF16),
        'wdt': per_group(w_in[:, o2:o3]).astype(BF16),
        'wrw': w_in[:, o3:o4].astype(BF16),
        'wg': w_in[:, o4:].astype(BF16),
        'conv_w': w['ssm_conv_w'].astype(F32),
        'conv_b': row(w['ssm_conv_b']),
        'dt_bias': row(per_group(w['ssm_dt_bias'])),
        'a_log': row(per_group(w['ssm_a_log'])),
        'd_exp': row(jnp.repeat(w['ssm_d'], p_dim)),
        'ssm_norm_g': row(w['ssm_norm_g']),
        'wa': w['w_branch_a'].astype(BF16),
        'mu': row(w['rwkv_mu']),
        'wlora': wlora.astype(BF16),
        'w0': row(w['rwkv_w0']),
        'a0': row(w['rwkv_a0']),
        'k_k': row(w['rwkv_k_k']),
        'k_a': row(w['rwkv_k_a']),
        'r_k': row(w['rwkv_r_k']),
        'ln_w': row(w['rwkv_ln_w']),
        'ln_b': row(w['rwkv_ln_b']),
        'wb': w['w_branch_b'].astype(BF16),
        'wo': w['w_out'].astype(BF16),
        'norm2_g': row(w['norm2_g']),
        'wup': w['ffn_w_up'].astype(BF16),
        'ffn_conv_w': w['ffn_conv_w'].astype(F32),
        'ffn_conv_b': row(w['ffn_conv_b']),
        'wdn': w['ffn_w_down'].astype(BF16),
        'head_sum': head_sum,
        'head_expand': head_sum.T,
        'ssd_expand': ssd_expand,
        'ssd_expand2': ssd_expand2,
        'tri': tri,
        'block_ones': block_ones,
    }


def _hist8(state_rows):
    k = state_rows.shape[1]
    return jnp.pad(state_rows.astype(F32), ((0, 0), (SUBLANES - k, 0), (0, 0)))


def _pair_heads(state):
    nseq, nh, dv, dk = state.shape
    s = state.astype(F32).reshape(nseq, nh // 2, 2, dv, dk)
    return jnp.transpose(s, (0, 1, 4, 2, 3)).reshape(nseq * (nh // 2), dk, 2 * dv)


def _unpair_heads(state, nseq):
    npair, dk, dv2 = state.shape[0] // nseq, state.shape[1], state.shape[2]
    s = state.reshape(nseq, npair, dk, 2, dv2 // 2)
    return jnp.transpose(s, (0, 1, 3, 4, 2)).reshape(nseq, npair * 2, dv2 // 2, dk)


def _layer(x, conv_buf, ssm_state, shift_buf, wkv_state, ffn_buf, p, final_norm):
    nseq, seqlen, dm = x.shape
    act_dtype = BF16 if seqlen % 16 == 0 else F32
    xf = x.reshape(nseq * seqlen, dm)
    z, xc, dt, conv_tail = _ssm_in(xf, _hist8(conv_buf), p, nseq, seqlen, act_dtype)
    r, lw, k, v, kk, bb, g, gates, shift_tail = _rwkv_in(xf, _hist8(shift_buf[:, None]), p, nseq, seqlen,
                                                         act_dtype)
    ya, new_ssm = _ssd(xc, dt, z, ssm_state.astype(F32), p, nseq, seqlen, act_dtype)
    yb, wkv_pairs = _wkv(r, lw, k, v, kk, bb, g, _pair_heads(wkv_state), p, nseq, seqlen, act_dtype)
    out, ffn_tail = _post(xf, ya, yb, gates, _hist8(ffn_buf), p, nseq, seqlen, final_norm)
    new_conv = conv_tail[:, SUBLANES - conv_buf.shape[1]:]
    new_shift = shift_tail[:, SUBLANES - 1]
    new_ffn = ffn_tail[:, SUBLANES - ffn_buf.shape[1]:]
    new_wkv = _unpair_heads(wkv_pairs, nseq)
    return out.reshape(nseq, seqlen, dm), (new_conv, new_ssm, new_shift, new_wkv, new_ffn)


_LAYER_WEIGHTS = ('norm1_g', 'w_in', 'ssm_conv_w', 'ssm_conv_b', 'ssm_dt_bias', 'ssm_a_log', 'ssm_d',
                  'ssm_norm_g', 'w_branch_a', 'rwkv_mu', 'rwkv_w0', 'rwkv_w_up', 'rwkv_a0', 'rwkv_a_up',
                  'rwkv_g_up', 'rwkv_k_k', 'rwkv_k_a', 'rwkv_r_k', 'rwkv_ln_w', 'rwkv_ln_b', 'w_branch_b',
                  'w_out', 'norm2_g', 'ffn_w_up', 'ffn_conv_w', 'ffn_conv_b', 'ffn_w_down')


def kernel(x_prompt, x_sample, state_ssm_conv, state_ssm, state_rwkv_shift, state_rwkv, state_ffn_conv,
           norm1_g, w_in, ssm_conv_w, ssm_conv_b, ssm_dt_bias, ssm_a_log, ssm_d, ssm_norm_g, w_branch_a,
           rwkv_mu, rwkv_w0, rwkv_w_up, rwkv_a0, rwkv_a_up, rwkv_g_up, rwkv_k_k, rwkv_k_a, rwkv_r_k,
           rwkv_ln_w, rwkv_ln_b, w_branch_b, w_out, norm2_g, ffn_w_up, ffn_conv_w, ffn_conv_b, ffn_w_down,
           final_g):
    stacked = dict(zip(_LAYER_WEIGHTS, (
        norm1_g, w_in, ssm_conv_w, ssm_conv_b, ssm_dt_bias, ssm_a_log, ssm_d, ssm_norm_g, w_branch_a,
        rwkv_mu, rwkv_w0, rwkv_w_up, rwkv_a0, rwkv_a_up, rwkv_g_up, rwkv_k_k, rwkv_k_a, rwkv_r_k,
        rwkv_ln_w, rwkv_ln_b, w_branch_b, w_out, norm2_g, ffn_w_up, ffn_conv_w, ffn_conv_b, ffn_w_down)))
    depth = w_in.shape[0]
    _, _, n_heads, _, _ = state_ssm.shape
    conv_dim = state_ssm_conv.shape[-1]
    d_inner = w_branch_a.shape[1]
    shift_dim = state_rwkv_shift.shape[-1]
    rwkv_heads = state_rwkv.shape[2]
    rwkv_dim = w_branch_b.shape[1]
    groups = (conv_dim - d_inner) // (2 * state_ssm.shape[-1])
    dims = (d_inner, conv_dim, n_heads, shift_dim, groups, rwkv_dim, rwkv_heads)

    xp, xs = x_prompt, x_sample
    bp = xp.shape[0]
    new_p = ([], [], [], [], [])
    new_s = ([], [], [], [], [])
    for i in range(depth):
        p = _prep_layer({name: a[i] for name, a in stacked.items()}, dims)
        p['final_g'] = final_g.reshape(1, -1).astype(F32)
        last = i == depth - 1
        xp, sp = _layer(
            xp,
            jnp.zeros((bp,) + state_ssm_conv.shape[2:], F32),
            jnp.zeros((bp,) + state_ssm.shape[2:], F32),
            jnp.zeros((bp,) + state_rwkv_shift.shape[2:], F32),
            jnp.zeros((bp,) + state_rwkv.shape[2:], F32),
            jnp.zeros((bp,) + state_ffn_conv.shape[2:], F32),
            p, last)
        xs, ss = _layer(xs, state_ssm_conv[i], state_ssm[i], state_rwkv_shift[i], state_rwkv[i],
                        state_ffn_conv[i], p, last)
        for j in range(5):
            new_p[j].append(sp[j])
            new_s[j].append(ss[j])
    return (xp, xs,
            jnp.stack(new_p[0]), jnp.stack(new_p[1]), jnp.stack(new_p[2]), jnp.stack(new_p[3]),
            jnp.stack(new_p[4]),
            jnp.stack(new_s[0]), jnp.stack(new_s[1]), jnp.stack(new_s[2]), jnp.stack(new_s[3]),
            jnp.stack(new_s[4]))
```

```python
import functools

import jax
import jax.numpy as jnp
from jax import lax
from jax.experimental import pallas as pl
from jax.experimental.pallas import tpu as pltpu

F32 = jnp.float32
BF16 = jnp.bfloat16

NORM_EPS = 1e-5
GN_EPS = 64e-5

LANES = 128
SUBLANES = 8
ROW_TILE = 256
COL_STRIP = 512
SSD_CHUNK = 128
WKV_TBLOCK = 64
WKV_PROMPT_SEQS = 2
WKV_SAMPLE_SEQS = 4
VMEM_LIMIT = 56 * 1024 * 1024


def _dot(a, b):
    return jnp.dot(a, b, preferred_element_type=F32)


def _split(x, n):
    parts = []
    rem = x
    for i in range(n):
        p = rem.astype(BF16)
        parts.append(p)
        if i + 1 < n:
            rem = rem - p.astype(F32)
    return parts


def _dot_split_lhs(x, m, n):
    acc = None
    for p in _split(x, n):
        d = _dot(p, m)
        acc = d if acc is None else acc + d
    return acc


def _dot_split_rhs(m, x, n):
    acc = None
    for p in _split(x, n):
        d = _dot(m, p)
        acc = d if acc is None else acc + d
    return acc


def _sigmoid(x):
    return 1.0 / (1.0 + jnp.exp(-x))


def _silu(x):
    return x * _sigmoid(x)


def _softplus(x):
    return jnp.maximum(x, 0.0) + jnp.log1p(jnp.exp(-jnp.abs(x)))


def _rmsnorm(x, g, eps):
    ms = jnp.mean(x * x, axis=-1, keepdims=True)
    return x * lax.rsqrt(ms + eps) * g


def _shift_rows(u, hist, seg, j):
    rows = u.shape[0]
    ru = pltpu.roll(u, j, 0)
    if seg == SUBLANES:
        rh = pltpu.roll(hist, (rows - SUBLANES + j) % rows, 0)
        pos = lax.broadcasted_iota(jnp.int32, u.shape, 0) % SUBLANES
        return jnp.where(pos < j, rh, ru)
    assert seg == rows
    rh = pltpu.roll(hist, j, 0)
    pos = lax.broadcasted_iota(jnp.int32, rh.shape, 0)
    top = jnp.where(pos < j, rh, ru[:SUBLANES])
    return jnp.concatenate([top, ru[SUBLANES:]], axis=0)


def _causal_conv(u, hist, seg, w_ref, b_ref):
    taps = w_ref.shape[0]
    acc = u * w_ref[taps - 1:taps, :] + b_ref[...]
    for j in range(1, taps):
        acc = acc + _shift_rows(u, hist, seg, j) * w_ref[taps - 1 - j:taps - j, :]
    return acc


def _load_hist(hist_ref, carry_ref, seg):
    if seg == SUBLANES:
        nseq, _, c = hist_ref.shape
        return hist_ref[...].reshape(nseq * SUBLANES, c)

    @pl.when(pl.program_id(1) == 0)
    def _():
        carry_ref[...] = hist_ref[0]

    return carry_ref[...]


def _store_tail(u, tail_ref, carry_ref, seg):
    if seg == SUBLANES:
        tail_ref[...] = u.reshape(tail_ref.shape)
    else:
        last = u[u.shape[0] - SUBLANES:]
        carry_ref[...] = last
        tail_ref[0] = last


def _ssm_in_body(x_ref, hist_ref, g1_ref, wz_ref, wx_ref, wdt_ref, cw_ref, cb_ref, dtb_ref,
                 z_ref, xc_ref, dt_ref, tail_ref, carry_ref, *, seg):
    h = _rmsnorm(x_ref[...], g1_ref[...], NORM_EPS).astype(BF16)
    hist = _load_hist(hist_ref, carry_ref, seg)
    taps = cw_ref.shape[0]
    for lo in range(0, wx_ref.shape[1], COL_STRIP):
        sl = slice(lo, lo + COL_STRIP)
        u = _dot(h, wx_ref[:, sl])
        hs = hist[:, sl]
        acc = u * cw_ref[taps - 1:taps, sl] + cb_ref[:, sl]
        for j in range(1, taps):
            acc = acc + _shift_rows(u, hs, seg, j) * cw_ref[taps - 1 - j:taps - j, sl]
        xc_ref[:, sl] = _silu(acc)
        if seg == SUBLANES:
            tail_ref[:, :, sl] = u.reshape(tail_ref.shape[0], SUBLANES, COL_STRIP)
        else:
            last = u[u.shape[0] - SUBLANES:]
            carry_ref[:, sl] = last
            tail_ref[0, :, sl] = last
    for lo in range(0, wz_ref.shape[1], COL_STRIP):
        sl = slice(lo, lo + COL_STRIP)
        z_ref[:, sl] = _dot(h, wz_ref[:, sl]).astype(z_ref.dtype)
    dt_ref[...] = _softplus(_dot(h, wdt_ref[...]) + dtb_ref[...])


def _rwkv_in_body(x_ref, hist_ref, g1_ref, wrw_ref, wg_ref, mu_ref, wlora_ref, w0_ref, a0_ref,
                  kk_ref, ka_ref, seg_ref, exp_ref,
                  r_out, lw_out, k_out, v_out, kkn_out, bb_out, g_out, gates_out, tail_ref,
                  carry_ref, *, seg, dim):
    h = _rmsnorm(x_ref[...], g1_ref[...], NORM_EPS).astype(BF16)
    gates_out[...] = _sigmoid(_dot(h, wg_ref[...])).astype(gates_out.dtype)
    u = _dot(h, wrw_ref[...])
    hist = _load_hist(hist_ref, carry_ref, seg)
    prev = _shift_rows(u, hist, seg, 1)
    _store_tail(u, tail_ref, carry_ref, seg)
    mix = u + (prev - u) * mu_ref[...]
    r = mix[:, 0:dim]
    k = mix[:, dim:2 * dim]
    v = mix[:, 2 * dim:3 * dim]
    low = mix[:, 3 * dim:]
    lane = lax.broadcasted_iota(jnp.int32, low.shape, 1)
    lo_w = low.shape[1] // 4
    act = jnp.where(lane < lo_w, jnp.tanh(low), jnp.where(lane < 2 * lo_w, low, _sigmoid(low)))
    up = _dot(act.astype(BF16), wlora_ref[...])
    wlog = -_softplus(-(w0_ref[...] + up[:, 0:dim])) - 0.5
    a = _sigmoid(a0_ref[...] + up[:, dim:2 * dim])
    kkr = k * kk_ref[...]
    ss = _dot_split_lhs(kkr * kkr, seg_ref[...], 2)
    inv = 1.0 / jnp.maximum(jnp.sqrt(ss), 1e-12)
    kkn = kkr * _dot_split_lhs(inv, exp_ref[...], 3)
    r_out[...] = r
    lw_out[...] = -jnp.exp(wlog)
    k_out[...] = k * (1.0 + (a - 1.0) * ka_ref[...])
    v_out[...] = v
    kkn_out[...] = kkn
    bb_out[...] = kkn * a
    g_out[...] = up[:, 2 * dim:3 * dim].astype(g_out.dtype)


def _ssd_body(xm_ref, b_ref, c_ref, dt_ref, z_ref, st_ref, alog_ref, dexp_ref, ng_ref,
              tri_ref, e_ref, e2_ref, y_ref, so_ref, ht_ref, *, rows, heads, groups):
    chunk = pl.program_id(1)
    n_chunks = pl.num_programs(1)
    R = SSD_CHUNK
    width = xm_ref.shape[1] // groups
    state_n = b_ref.shape[1] // groups
    p_dim = width // heads
    gs = range(groups)

    def pad(v):
        if rows == R:
            return v
        return jnp.concatenate([v, jnp.zeros((R - rows, v.shape[1]), v.dtype)], axis=0)

    def cols(ref, g, n):
        return ref[:, g * n:(g + 1) * n]

    @pl.when(chunk == 0)
    def _():
        for g in gs:
            ht_ref[g] = st_ref[0, g * heads:(g + 1) * heads].reshape(width, state_n).T

    tri = tri_ref[...]
    expand = e_ref[...]
    expand2 = e2_ref[...]
    row = lax.broadcasted_iota(jnp.int32, (R, R), 0)
    col = lax.broadcasted_iota(jnp.int32, (R, R), 1)
    causal = row >= col
    lane = lax.broadcasted_iota(jnp.int32, (R, LANES), 1)

    xm = [pad(cols(xm_ref, g, width)) for g in gs]
    bm = [pad(cols(b_ref, g, state_n)) for g in gs]
    cmb = [pad(cols(c_ref, g, state_n)).astype(BF16) for g in gs]
    dt = [pad(cols(dt_ref, g, LANES)) for g in gs]
    a = [dt[g] * (-jnp.exp(cols(alog_ref, g, LANES))) for g in gs]
    cs = [_dot_split_rhs(tri, a[g], 3) for g in gs]
    dt_e = [_dot_split_lhs(dt[g], expand, 3) for g in gs]
    cs_e = [_dot_split_lhs(cs[g], expand, 3) for g in gs]
    cs_e2 = [_dot_split_lhs(cs[g], expand2, 3) for g in gs]
    bt = [bm[g].T.astype(BF16) for g in gs]
    cs_t = [cs[g].T for g in gs]
    cl_e = [cs_e[g][R - 1:R, :] for g in gs]
    xdt = [xm[g] * dt_e[g] for g in gs]
    xs = [(xdt[g] * jnp.exp(cl_e[g] - cs_e[g])).astype(BF16) for g in gs]
    ht = [ht_ref[g] for g in gs]
    st_new = [_dot(bt[g], xs[g]) for g in gs]
    y_off = [_dot(cmb[g], ht[g].astype(BF16)) for g in gs]
    cb = [_dot(cmb[g], bt[g]) for g in gs]
    for g in gs:
        ht_ref[g] = ht[g] * jnp.exp(cl_e[g]) + st_new[g]
    ys = [[] for _ in gs]
    for j in range(heads // 2):
        lhs, rhs = [], []
        for g in gs:
            ms = []
            for hh in (2 * j, 2 * j + 1):
                seg_ = cs_e2[g][:, hh * R:(hh + 1) * R] - cs_t[g][hh:hh + 1, :]
                ms.append(jnp.where(causal, cb[g] * jnp.exp(jnp.where(causal, seg_, 0.0)), 0.0).astype(BF16))
            lhs.append(jnp.concatenate(ms, axis=1))
            xp = xdt[g][:, j * LANES:(j + 1) * LANES]
            rhs.append(jnp.concatenate([jnp.where(lane < p_dim, xp, 0.0), jnp.where(lane >= p_dim, xp, 0.0)],
                                       axis=0).astype(BF16))
        for g in gs:
            ys[g].append(_dot(lhs[g], rhs[g]))
    for g in gs:
        y = jnp.concatenate(ys[g], axis=1) + y_off[g] * jnp.exp(cs_e[g])
        y = y + cols(dexp_ref, g, width) * xm[g]
        yz = y * _silu(pad(cols(z_ref, g, width).astype(F32)))
        yn = yz * lax.rsqrt(jnp.mean(yz * yz, axis=-1, keepdims=True) + NORM_EPS) * cols(ng_ref, g, width)
        y_ref[:, g * width:(g + 1) * width] = yn[:rows].astype(y_ref.dtype)

    @pl.when(chunk == n_chunks - 1)
    def _():
        for g in gs:
            so_ref[0, g * heads:(g + 1) * heads] = ht_ref[g].T.reshape((heads,) + so_ref.shape[2:])


def _wkv_body(r_ref, lw_ref, k_ref, v_ref, kk_ref, bb_ref, g_ref, s0_ref, rk_ref, lnw_ref, lnb_ref,
              bo_ref, tri_ref, o_ref, so_ref, s_ref, *, head_dim):
    nseq, steps, dim = r_ref.shape
    npair = dim // LANES
    nchunk = steps // SUBLANES
    block_ones = bo_ref[...]
    tri = tri_ref[...]

    @pl.when(pl.program_id(1) == 0)
    def _():
        for si in range(nseq):
            for p in range(npair):
                x = s0_ref[si, 2 * p:2 * p + 2].reshape(2 * head_dim, head_dim)
                xp = jnp.concatenate([x, jnp.zeros((2 * head_dim, LANES - head_dim), F32)], axis=1)
                s_ref[si * npair + p] = xp.T[:head_dim]

    lane8 = lax.broadcasted_iota(jnp.int32, (SUBLANES, LANES), 1)
    row8 = lax.broadcasted_iota(jnp.int32, (SUBLANES, LANES), 0)
    head0 = lane8 < head_dim
    rowi = lax.broadcasted_iota(jnp.int32, (head_dim, LANES), 0)
    lanei = lax.broadcasted_iota(jnp.int32, (head_dim, LANES), 1)
    diag = (rowi == lanei % head_dim).astype(F32)

    def other_head(a):
        return pltpu.roll(a, head_dim, 1)

    def rows(a, c):
        return a[c * SUBLANES:(c + 1) * SUBLANES]

    def bc(tile, i):
        return jnp.broadcast_to(tile[i:i + 1], (SUBLANES, LANES))

    chains = []
    for si in range(nseq):
        for p in range(npair):
            sl = slice(p * LANES, (p + 1) * LANES)
            lw = lw_ref[si, :, sl]
            cum = _dot_split_rhs(tri, lw, 3)
            p_in = jnp.exp(cum)
            p_inv = jnp.exp(-cum)
            at = kk_ref[si, :, sl] * jnp.exp(cum - lw)
            rt = r_ref[si, :, sl] * p_in
            bt = bb_ref[si, :, sl] * p_inv
            kt = k_ref[si, :, sl] * p_inv
            at_o, rt_o, bt_o, kt_o = other_head(at), other_head(rt), other_head(bt), other_head(kt)
            x4 = jnp.concatenate([rows(a, c) for c in range(nchunk) for a in (bt, bt_o, kt, kt_o)], axis=0)
            if x4.shape[0] < LANES:
                x4 = jnp.concatenate([x4, jnp.zeros((LANES - x4.shape[0], LANES), F32)], axis=0)
            xt = x4.T[:head_dim].astype(BF16)
            chains.append(dict(si=si, sl=sl, at=at, rt=rt, bt=bt, kt=kt, at_o=at_o, rt_o=rt_o, xt=xt,
                               v=v_ref[si, :, sl], p_end=p_in[steps - 1:steps], s=s_ref[si * npair + p],
                               ys=[]))

    for c in range(nchunk):
        for ch in chains:
            at_c, rt_c, bt_c, kt_c = rows(ch['at'], c), rows(ch['rt'], c), rows(ch['bt'], c), rows(ch['kt'], c)
            tiles = []
            for i in range(SUBLANES):
                am = jnp.where(row8 > i, at_c, 0.0)
                rm = jnp.where(row8 >= i, rt_c, 0.0)
                bi, ki = bc(bt_c, i), bc(kt_c, i)
                tiles += [am * bi, am * ki, rm * bi, rm * ki]
            coef = _dot(jnp.concatenate(tiles, axis=0).astype(BF16), block_ones)
            ch['coef'] = [rows(coef, n) for n in range(4 * SUBLANES)]
            v_c = rows(ch['v'], c)
            va = None
            yv = None
            for i in range(SUBLANES):
                vi = bc(v_c, i)
                t_ak = ch['coef'][4 * i + 1] * vi
                t_rk = ch['coef'][4 * i + 3] * vi
                va = t_ak if va is None else va + t_ak
                yv = t_rk if yv is None else yv + t_rk
            ch['va'], ch['yv'], ch['v_c'] = va, yv, v_c
            lhs = jnp.concatenate([at_c[:, :head_dim], rows(ch['at_o'], c)[:, :head_dim],
                                   rt_c[:, :head_dim], rows(ch['rt_o'], c)[:, :head_dim]], axis=0)
            ch['lhs'] = lhs.astype(BF16)
        for ch in chains:
            ch['g'] = _dot(ch['lhs'], ch['s'].astype(BF16))
        for ch in chains:
            g = ch['g']
            g_a = jnp.where(head0, rows(g, 0), rows(g, 1))
            g_r = jnp.where(head0, rows(g, 2), rows(g, 3))
            sa = g_a + ch['va']
            y = g_r + ch['yv']
            for i in range(SUBLANES):
                sai = bc(sa, i)
                if i + 1 < SUBLANES:
                    sa = sa - ch['coef'][4 * i] * sai
                y = y - ch['coef'][4 * i + 2] * sai
            ch['ys'].append(y)
            v_c = ch['v_c']
            wd = jnp.concatenate([jnp.where(head0, -sa, 0.0), jnp.where(head0, 0.0, -sa),
                                  jnp.where(head0, v_c, 0.0), jnp.where(head0, 0.0, v_c)], axis=0)
            ch['wd'] = wd.astype(BF16)
        for ch in chains:
            cols = ch['xt'][:, c * 4 * SUBLANES:(c + 1) * 4 * SUBLANES]
            ch['s'] = ch['s'] + _dot(cols, ch['wd'])

    inv_n = 1.0 / head_dim
    for n, ch in enumerate(chains):
        si, sl = ch['si'], ch['sl']
        p_col = _dot_split_lhs(diag * ch['p_end'], block_ones, 3)
        s_new = ch['s'] * p_col
        s_ref[n] = s_new
        ch['s'] = s_new
        y = jnp.concatenate(ch['ys'], axis=0)
        mu = _dot_split_lhs(y, block_ones, 2) * inv_n
        d = y - mu
        var = _dot_split_lhs(d * d, block_ones, 2) * inv_n
        yn = d * lax.rsqrt(var + GN_EPS) * lnw_ref[:, sl] + lnb_ref[:, sl]
        bonus = _dot_split_lhs(r_ref[si, :, sl] * k_ref[si, :, sl] * rk_ref[:, sl], block_ones, 2)
        out = (yn + bonus * ch['v']) * g_ref[si, :, sl].astype(F32)
        o_ref[si, :, sl] = out.astype(o_ref.dtype)

    @pl.when(pl.program_id(1) == pl.num_programs(1) - 1)
    def _():
        for n, ch in enumerate(chains):
            sp = jnp.concatenate([ch['s'], jnp.zeros((LANES - head_dim, LANES), F32)], axis=0)
            back = sp.T[:, :head_dim]
            so_ref[n // npair, 2 * (n % npair):2 * (n % npair) + 2] = back.reshape(2, head_dim, head_dim)


def _post_body(x_ref, ya_ref, yb_ref, gt_ref, hist_ref, wa_ref, wb_ref, wo_ref, g2_ref, wup_ref,
               cw_ref, cb_ref, wdn_ref, gf_ref, o_ref, tail_ref, carry_ref, *, seg, final_norm):
    dm = x_ref.shape[1]
    ua = _dot(ya_ref[...].astype(BF16), wa_ref[...])
    ub = _dot(yb_ref[...].astype(BF16), wb_ref[...])
    gates = gt_ref[...].astype(F32)
    m = (gates[:, :dm] * ua + gates[:, dm:] * ub).astype(BF16)
    x1 = x_ref[...] + _dot(m, wo_ref[...])
    h2 = _rmsnorm(x1, g2_ref[...], NORM_EPS).astype(BF16)
    up = _dot(h2, wup_ref[...])
    dff = up.shape[1] // 2
    ug = up[:, :dff]
    hist = _load_hist(hist_ref, carry_ref, seg)
    ugc = _causal_conv(ug, hist, seg, cw_ref, cb_ref)
    _store_tail(ug, tail_ref, carry_ref, seg)
    act = (_silu(ugc) * up[:, dff:]).astype(BF16)
    x2 = x1 + _dot(act, wdn_ref[...])
    if final_norm:
        x2 = _rmsnorm(x2, gf_ref[...], NORM_EPS)
    o_ref[...] = x2


def _const_spec(shape):
    nd = len(shape)
    return pl.BlockSpec(shape, lambda *_: (0,) * nd, pipeline_mode=pl.Buffered(1))


def _params(sem):
    return pltpu.CompilerParams(dimension_semantics=sem, vmem_limit_bytes=VMEM_LIMIT)


def _token_tiling(nseq, seqlen):
    if seqlen == SUBLANES:
        per = ROW_TILE // SUBLANES
        assert nseq % per == 0
        return per, SUBLANES, (nseq // per, 1)
    assert seqlen % ROW_TILE == 0
    return 1, ROW_TILE, (nseq, seqlen // ROW_TILE)


def _row_spec(cols, lt):
    return pl.BlockSpec((ROW_TILE, cols), lambda i, l: (i * lt + l, 0))


def _hist_spec(per, cols):
    return pl.BlockSpec((per, SUBLANES, cols), lambda i, l: (i, 0, 0))


def _ssm_in(x, hist, p, nseq, seqlen, act_dtype):
    per, seg, grid = _token_tiling(nseq, seqlen)
    t, dm = x.shape
    dz = p['wz'].shape[1]
    dc = p['wx'].shape[1]
    dd = p['wdt'].shape[1]
    lt = grid[1]
    return pl.pallas_call(
        functools.partial(_ssm_in_body, seg=seg),
        grid=grid,
        in_specs=[_row_spec(dm, lt), _hist_spec(per, dc), _const_spec((1, dm)),
                  _const_spec(p['wz'].shape), _const_spec(p['wx'].shape), _const_spec(p['wdt'].shape),
                  _const_spec(p['conv_w'].shape), _const_spec((1, dc)), _const_spec((1, dd))],
        out_specs=[_row_spec(dz, lt), _row_spec(dc, lt), _row_spec(dd, lt), _hist_spec(per, dc)],
        out_shape=[jax.ShapeDtypeStruct((t, dz), act_dtype), jax.ShapeDtypeStruct((t, dc), F32),
                   jax.ShapeDtypeStruct((t, dd), F32), jax.ShapeDtypeStruct((nseq, SUBLANES, dc), F32)],
        scratch_shapes=[pltpu.VMEM((SUBLANES, dc), F32)],
        compiler_params=_params(("arbitrary", "arbitrary")),
        name="ssm_in",
    )(x, hist, p['norm1_g'], p['wz'], p['wx'], p['wdt'], p['conv_w'], p['conv_b'], p['dt_bias'])


def _rwkv_in(x, hist, p, nseq, seqlen, act_dtype):
    per, seg, grid = _token_tiling(nseq, seqlen)
    t, dm = x.shape
    dim = p['w0'].shape[1]
    drw = p['wrw'].shape[1]
    dg = p['wg'].shape[1]
    lt = grid[1]
    f32_out = jax.ShapeDtypeStruct((t, dim), F32)
    return pl.pallas_call(
        functools.partial(_rwkv_in_body, seg=seg, dim=dim),
        grid=grid,
        in_specs=[_row_spec(dm, lt), _hist_spec(per, drw), _const_spec((1, dm)),
                  _const_spec(p['wrw'].shape), _const_spec(p['wg'].shape), _const_spec((1, drw)),
                  _const_spec(p['wlora'].shape), _const_spec((1, dim)), _const_spec((1, dim)),
                  _const_spec((1, dim)), _const_spec((1, dim)),
                  _const_spec(p['head_sum'].shape), _const_spec(p['head_expand'].shape)],
        out_specs=[_row_spec(dim, lt)] * 7 + [_row_spec(dg, lt), _hist_spec(per, drw)],
        out_shape=[f32_out] * 6 + [jax.ShapeDtypeStruct((t, dim), act_dtype),
                                   jax.ShapeDtypeStruct((t, dg), BF16),
                                   jax.ShapeDtypeStruct((nseq, SUBLANES, drw), F32)],
        scratch_shapes=[pltpu.VMEM((SUBLANES, drw), F32)],
        compiler_params=_params(("arbitrary", "arbitrary")),
        name="rwkv_in",
    )(x, hist, p['norm1_g'], p['wrw'], p['wg'], p['mu'], p['wlora'], p['w0'], p['a0'],
      p['k_k'], p['k_a'], p['head_sum'], p['head_expand'])


def _ssd(xc, dt, z, state, p, nseq, seqlen, act_dtype):
    t = xc.shape[0]
    _, n_heads, p_dim, state_n = state.shape
    groups = dt.shape[1] // LANES
    heads = n_heads // groups
    width = heads * p_dim
    d_inner = groups * width
    rows = SSD_CHUNK if seqlen % SSD_CHUNK == 0 else seqlen
    assert seqlen % rows == 0 and (rows == SSD_CHUNK or seqlen == rows == SUBLANES)
    nc = seqlen // rows
    assert state_n == LANES and width % LANES == 0 and p_dim * 2 == LANES
    gn = groups * state_n
    b_blk = d_inner // gn
    assert d_inner % gn == 0
    return pl.pallas_call(
        functools.partial(_ssd_body, rows=rows, heads=heads, groups=groups),
        grid=(nseq, nc),
        in_specs=[pl.BlockSpec((rows, d_inner), lambda b, c: (b * nc + c, 0)),
                  pl.BlockSpec((rows, gn), lambda b, c: (b * nc + c, b_blk)),
                  pl.BlockSpec((rows, gn), lambda b, c: (b * nc + c, b_blk + 1)),
                  pl.BlockSpec((rows, groups * LANES), lambda b, c: (b * nc + c, 0)),
                  pl.BlockSpec((rows, d_inner), lambda b, c: (b * nc + c, 0)),
                  pl.BlockSpec((1, n_heads, p_dim, state_n), lambda b, c: (b, 0, 0, 0)),
                  _const_spec((1, groups * LANES)), _const_spec((1, d_inner)), _const_spec((1, d_inner)),
                  _const_spec(p['tri'].shape), _const_spec(p['ssd_expand'].shape),
                  _const_spec(p['ssd_expand2'].shape)],
        out_specs=[pl.BlockSpec((rows, d_inner), lambda b, c: (b * nc + c, 0)),
                   pl.BlockSpec((1, n_heads, p_dim, state_n), lambda b, c: (b, 0, 0, 0))],
        out_shape=[jax.ShapeDtypeStruct((t, d_inner), act_dtype),
                   jax.ShapeDtypeStruct(state.shape, F32)],
        scratch_shapes=[pltpu.VMEM((groups, state_n, width), F32)],
        compiler_params=_params(("arbitrary", "arbitrary")),
        name="ssd",
    )(xc, xc, xc, dt, z, state, p['a_log'], p['d_exp'], p['ssm_norm_g'],
      p['tri'], p['ssd_expand'], p['ssd_expand2'])


def _wkv(r, lw, k, v, kk, bb, g, state, p, nseq, seqlen, act_dtype):
    dim = r.shape[1]
    _, n_heads, head_dim, _ = state.shape
    npair = dim // LANES
    assert n_heads == 2 * npair and 2 * head_dim == LANES
    if seqlen == SUBLANES:
        per, steps = WKV_SAMPLE_SEQS, SUBLANES
    else:
        per, steps = (WKV_PROMPT_SEQS if nseq % WKV_PROMPT_SEQS == 0 else 1), WKV_TBLOCK
    assert nseq % per == 0 and seqlen % steps == 0
    grid = (nseq // per, seqlen // steps)
    seq_spec = pl.BlockSpec((per, steps, dim), lambda i, tb: (i, tb, 0))
    st_spec = pl.BlockSpec((per, n_heads, head_dim, head_dim), lambda i, tb: (i, 0, 0, 0))
    as3 = lambda a: a.reshape(nseq, seqlen, dim)
    out, s_out = pl.pallas_call(
        functools.partial(_wkv_body, head_dim=head_dim),
        grid=grid,
        in_specs=[seq_spec] * 7 + [st_spec, _const_spec((1, dim)), _const_spec((1, dim)),
                                   _const_spec((1, dim)), _const_spec((LANES, LANES)),
                                   _const_spec((steps, steps))],
        out_specs=[seq_spec, st_spec],
        out_shape=[jax.ShapeDtypeStruct((nseq, seqlen, dim), act_dtype),
                   jax.ShapeDtypeStruct(state.shape, F32)],
        scratch_shapes=[pltpu.VMEM((per * npair, head_dim, LANES), F32)],
        compiler_params=_params(("arbitrary", "arbitrary")),
        name="wkv",
    )(as3(r), as3(lw), as3(k), as3(v), as3(kk), as3(bb), as3(g), state,
      p['r_k'], p['ln_w'], p['ln_b'], p['block_ones'], p['tri'][:steps, :steps])
    return out.reshape(nseq * seqlen, dim), s_out


def _post(x, ya, yb, gates, hist, p, nseq, seqlen, final_norm):
    per, seg, grid = _token_tiling(nseq, seqlen)
    t, dm = x.shape
    dff = p['wdn'].shape[0]
    lt = grid[1]
    return pl.pallas_call(
        functools.partial(_post_body, seg=seg, final_norm=final_norm),
        grid=grid,
        in_specs=[_row_spec(dm, lt), _row_spec(ya.shape[1], lt), _row_spec(yb.shape[1], lt),
                  _row_spec(gates.shape[1], lt), _hist_spec(per, dff),
                  _const_spec(p['wa'].shape), _const_spec(p['wb'].shape), _const_spec(p['wo'].shape),
                  _const_spec((1, dm)), _const_spec(p['wup'].shape), _const_spec(p['ffn_conv_w'].shape),
                  _const_spec((1, dff)), _const_spec(p['wdn'].shape), _const_spec((1, dm))],
        out_specs=[_row_spec(dm, lt), _hist_spec(per, dff)],
        out_shape=[jax.ShapeDtypeStruct((t, dm), F32), jax.ShapeDtypeStruct((nseq, SUBLANES, dff), F32)],
        scratch_shapes=[pltpu.VMEM((SUBLANES, dff), F32)],
        compiler_params=_params(("arbitrary", "arbitrary")),
        name="post",
    )(x, ya, yb, gates, hist, p['wa'], p['wb'], p['wo'], p['norm2_g'], p['wup'], p['ffn_conv_w'],
      p['ffn_conv_b'], p['wdn'], p['final_g'])


def _prep_layer(w, dims):
    d_inner, conv_dim, n_heads, shift_dim, groups, rwkv_dim, rwkv_heads = dims
    row = lambda a: a.reshape(1, -1).astype(F32)
    w_in = w['w_in']
    o1 = d_inner
    o2 = o1 + conv_dim
    o3 = o2 + n_heads
    o4 = o3 + shift_dim
    hpg = n_heads // groups

    def per_group(a):
        lead = a.shape[:-1]
        a = a.reshape(lead + (groups, hpg))
        a = jnp.pad(a, [(0, 0)] * len(lead) + [(0, 0), (0, LANES - hpg)])
        return a.reshape(lead + (groups * LANES,))

    p_dim = d_inner // n_heads
    head_dim = rwkv_dim // rwkv_heads
    lora_w = w['rwkv_w_up'].shape[0]
    lora_a = w['rwkv_a_up'].shape[0]
    lora_g = w['rwkv_g_up'].shape[0]
    assert lora_w == lora_a and lora_g == 2 * lora_w
    wlora = jnp.zeros((lora_w + lora_a + lora_g, 3 * rwkv_dim), F32)
    wlora = wlora.at[:lora_w, :rwkv_dim].set(w['rwkv_w_up'])
    wlora = wlora.at[lora_w:lora_w + lora_a, rwkv_dim:2 * rwkv_dim].set(w['rwkv_a_up'])
    wlora = wlora.at[lora_w + lora_a:, 2 * rwkv_dim:].set(w['rwkv_g_up'])

    ch = jnp.arange(rwkv_dim) // head_dim
    head_sum = (ch[:, None] == jnp.arange(LANES)[None, :]).astype(BF16)
    li = jnp.arange(LANES)
    width = hpg * p_dim
    ssd_expand = (li[:, None] == (jnp.arange(width) // p_dim)[None, :]).astype(BF16)
    ssd_expand2 = (li[:, None] == (jnp.arange(hpg * SSD_CHUNK) // SSD_CHUNK)[None, :]).astype(BF16)
    ci = jnp.arange(SSD_CHUNK)
    tri = (ci[:, None] >= ci[None, :]).astype(BF16)
    block_ones = ((li[:, None] // head_dim) == (li[None, :] // head_dim)).astype(BF16)

    return {
        'norm1_g': row(w['norm1_g']),
        'wz': w_in[:, :o1].astype(BF16),
        'wx': w_in[:, o1:o2].astype(BF16),
        'wdt': per_group(w_in[:, o2:o3]).astype(BF16),
        'wrw': w_in[:, o3:o4].astype(BF16),
        'wg': w_in[:, o4:].astype(BF16),
        'conv_w': w['ssm_conv_w'].astype(F32),
        'conv_b': row(w['ssm_conv_b']),
        'dt_bias': row(per_group(w['ssm_dt_bias'])),
        'a_log': row(per_group(w['ssm_a_log'])),
        'd_exp': row(jnp.repeat(w['ssm_d'], p_dim)),
        'ssm_norm_g': row(w['ssm_norm_g']),
        'wa': w['w_branch_a'].astype(BF16),
        'mu': row(w['rwkv_mu']),
        'wlora': wlora.astype(BF16),
        'w0': row(w['rwkv_w0']),
        'a0': row(w['rwkv_a0']),
        'k_k': row(w['rwkv_k_k']),
        'k_a': row(w['rwkv_k_a']),
        'r_k': row(w['rwkv_r_k']),
        'ln_w': row(w['rwkv_ln_w']),
        'ln_b': row(w['rwkv_ln_b']),
        'wb': w['w_branch_b'].astype(BF16),
        'wo': w['w_out'].astype(BF16),
        'norm2_g': row(w['norm2_g']),
        'wup': w['ffn_w_up'].astype(BF16),
        'ffn_conv_w': w['ffn_conv_w'].astype(F32),
        'ffn_conv_b': row(w['ffn_conv_b']),
        'wdn': w['ffn_w_down'].astype(BF16),
        'head_sum': head_sum,
        'head_expand': head_sum.T,
        'ssd_expand': ssd_expand,
        'ssd_expand2': ssd_expand2,
        'tri': tri,
        'block_ones': block_ones,
    }


def _hist8(state_rows):
    k = state_rows.shape[1]
    return jnp.pad(state_rows.astype(F32), ((0, 0), (SUBLANES - k, 0), (0, 0)))


def _layer(x, conv_buf, ssm_state, shift_buf, wkv_state, ffn_buf, p, final_norm):
    nseq, seqlen, dm = x.shape
    act_dtype = BF16 if seqlen % 16 == 0 else F32
    xf = x.reshape(nseq * seqlen, dm)
    z, xc, dt, conv_tail = _ssm_in(xf, _hist8(conv_buf), p, nseq, seqlen, act_dtype)
    r, lw, k, v, kk, bb, g, gates, shift_tail = _rwkv_in(xf, _hist8(shift_buf[:, None]), p, nseq, seqlen,
                                                         act_dtype)
    ya, new_ssm = _ssd(xc, dt, z, ssm_state.astype(F32), p, nseq, seqlen, act_dtype)
    yb, new_wkv = _wkv(r, lw, k, v, kk, bb, g, wkv_state.astype(F32), p, nseq, seqlen, act_dtype)
    out, ffn_tail = _post(xf, ya, yb, gates, _hist8(ffn_buf), p, nseq, seqlen, final_norm)
    new_conv = conv_tail[:, SUBLANES - conv_buf.shape[1]:]
    new_shift = shift_tail[:, SUBLANES - 1]
    new_ffn = ffn_tail[:, SUBLANES - ffn_buf.shape[1]:]
    return out.reshape(nseq, seqlen, dm), (new_conv, new_ssm, new_shift, new_wkv, new_ffn)


_LAYER_WEIGHTS = ('norm1_g', 'w_in', 'ssm_conv_w', 'ssm_conv_b', 'ssm_dt_bias', 'ssm_a_log', 'ssm_d',
                  'ssm_norm_g', 'w_branch_a', 'rwkv_mu', 'rwkv_w0', 'rwkv_w_up', 'rwkv_a0', 'rwkv_a_up',
                  'rwkv_g_up', 'rwkv_k_k', 'rwkv_k_a', 'rwkv_r_k', 'rwkv_ln_w', 'rwkv_ln_b', 'w_branch_b',
                  'w_out', 'norm2_g', 'ffn_w_up', 'ffn_conv_w', 'ffn_conv_b', 'ffn_w_down')


def kernel(x_prompt, x_sample, state_ssm_conv, state_ssm, state_rwkv_shift, state_rwkv, state_ffn_conv,
           norm1_g, w_in, ssm_conv_w, ssm_conv_b, ssm_dt_bias, ssm_a_log, ssm_d, ssm_norm_g, w_branch_a,
           rwkv_mu, rwkv_w0, rwkv_w_up, rwkv_a0, rwkv_a_up, rwkv_g_up, rwkv_k_k, rwkv_k_a, rwkv_r_k,
           rwkv_ln_w, rwkv_ln_b, w_branch_b, w_out, norm2_g, ffn_w_up, ffn_conv_w, ffn_conv_b, ffn_w_down,
           final_g):
    stacked = dict(zip(_LAYER_WEIGHTS, (
        norm1_g, w_in, ssm_conv_w, ssm_conv_b, ssm_dt_bias, ssm_a_log, ssm_d, ssm_norm_g, w_branch_a,
        rwkv_mu, rwkv_w0, rwkv_w_up, rwkv_a0, rwkv_a_up, rwkv_g_up, rwkv_k_k, rwkv_k_a, rwkv_r_k,
        rwkv_ln_w, rwkv_ln_b, w_branch_b, w_out, norm2_g, ffn_w_up, ffn_conv_w, ffn_conv_b, ffn_w_down)))
    depth = w_in.shape[0]
    _, _, n_heads, _, _ = state_ssm.shape
    conv_dim = state_ssm_conv.shape[-1]
    d_inner = w_branch_a.shape[1]
    shift_dim = state_rwkv_shift.shape[-1]
    rwkv_heads = state_rwkv.shape[2]
    rwkv_dim = w_branch_b.shape[1]
    groups = (conv_dim - d_inner) // (2 * state_ssm.shape[-1])
    dims = (d_inner, conv_dim, n_heads, shift_dim, groups, rwkv_dim, rwkv_heads)

    xp, xs = x_prompt, x_sample
    bp = xp.shape[0]
    new_p = ([], [], [], [], [])
    new_s = ([], [], [], [], [])
    for i in range(depth):
        p = _prep_layer({name: a[i] for name, a in stacked.items()}, dims)
        p['final_g'] = final_g.reshape(1, -1).astype(F32)
        last = i == depth - 1
        xp, sp = _layer(
            xp,
            jnp.zeros((bp,) + state_ssm_conv.shape[2:], F32),
            jnp.zeros((bp,) + state_ssm.shape[2:], F32),
            jnp.zeros((bp,) + state_rwkv_shift.shape[2:], F32),
            jnp.zeros((bp,) + state_rwkv.shape[2:], F32),
            jnp.zeros((bp,) + state_ffn_conv.shape[2:], F32),
            p, last)
        xs, ss = _layer(xs, state_ssm_conv[i], state_ssm[i], state_rwkv_shift[i], state_rwkv[i],
                        state_ffn_conv[i], p, last)
        for j in range(5):
            new_p[j].append(sp[j])
            new_s[j].append(ss[j])
    return (xp, xs,
            jnp.stack(new_p[0]), jnp.stack(new_p[1]), jnp.stack(new_p[2]), jnp.stack(new_p[3]),
            jnp.stack(new_p[4]),
            jnp.stack(new_s[0]), jnp.stack(new_s[1]), jnp.stack(new_s[2]), jnp.stack(new_s[3]),
            jnp.stack(new_s[4]))
```

```python
import functools

import jax
import jax.numpy as jnp
from jax import lax
from jax.experimental import pallas as pl
from jax.experimental.pallas import tpu as pltpu

F32 = jnp.float32
BF16 = jnp.bfloat16

NORM_EPS = 1e-5
GN_EPS = 64e-5

LANES = 128
SUBLANES = 8
ROW_TILE = 256
COL_STRIP = 512
SSD_CHUNK = 128
WKV_TBLOCK = 64
WKV_PROMPT_SEQS = 4
WKV_SAMPLE_SEQS = 4
VMEM_LIMIT = 56 * 1024 * 1024


def _dot(a, b):
    return jnp.dot(a, b, preferred_element_type=F32)


def _split(x, n):
    parts = []
    rem = x
    for i in range(n):
        p = rem.astype(BF16)
        parts.append(p)
        if i + 1 < n:
            rem = rem - p.astype(F32)
    return parts


def _dot_split_lhs(x, m, n):
    acc = None
    for p in _split(x, n):
        d = _dot(p, m)
        acc = d if acc is None else acc + d
    return acc


def _dot_split_rhs(m, x, n):
    acc = None
    for p in _split(x, n):
        d = _dot(m, p)
        acc = d if acc is None else acc + d
    return acc


def _sigmoid(x):
    return 1.0 / (1.0 + jnp.exp(-x))


def _silu(x):
    return x * _sigmoid(x)


def _softplus(x):
    return jnp.maximum(x, 0.0) + jnp.log1p(jnp.exp(-jnp.abs(x)))


def _rmsnorm(x, g, eps):
    ms = jnp.mean(x * x, axis=-1, keepdims=True)
    return x * lax.rsqrt(ms + eps) * g


def _shift_rows(u, hist, seg, j):
    rows = u.shape[0]
    ru = pltpu.roll(u, j, 0)
    if seg == SUBLANES:
        rh = pltpu.roll(hist, (rows - SUBLANES + j) % rows, 0)
        pos = lax.broadcasted_iota(jnp.int32, u.shape, 0) % SUBLANES
        return jnp.where(pos < j, rh, ru)
    assert seg == rows
    rh = pltpu.roll(hist, j, 0)
    pos = lax.broadcasted_iota(jnp.int32, rh.shape, 0)
    top = jnp.where(pos < j, rh, ru[:SUBLANES])
    return jnp.concatenate([top, ru[SUBLANES:]], axis=0)


def _causal_conv(u, hist, seg, w_ref, b_ref):
    taps = w_ref.shape[0]
    acc = u * w_ref[taps - 1:taps, :] + b_ref[...]
    for j in range(1, taps):
        acc = acc + _shift_rows(u, hist, seg, j) * w_ref[taps - 1 - j:taps - j, :]
    return acc


def _load_hist(hist_ref, carry_ref, seg):
    if seg == SUBLANES:
        nseq, _, c = hist_ref.shape
        return hist_ref[...].reshape(nseq * SUBLANES, c)

    @pl.when(pl.program_id(1) == 0)
    def _():
        carry_ref[...] = hist_ref[0]

    return carry_ref[...]


def _store_tail(u, tail_ref, carry_ref, seg):
    if seg == SUBLANES:
        tail_ref[...] = u.reshape(tail_ref.shape)
    else:
        last = u[u.shape[0] - SUBLANES:]
        carry_ref[...] = last
        tail_ref[0] = last


def _ssm_in_body(x_ref, hist_ref, g1_ref, wz_ref, wx_ref, wdt_ref, cw_ref, cb_ref, dtb_ref,
                 z_ref, xc_ref, dt_ref, tail_ref, carry_ref, *, seg):
    h = _rmsnorm(x_ref[...], g1_ref[...], NORM_EPS).astype(BF16)
    hist = _load_hist(hist_ref, carry_ref, seg)
    taps = cw_ref.shape[0]
    for lo in range(0, wx_ref.shape[1], COL_STRIP):
        sl = slice(lo, lo + COL_STRIP)
        u = _dot(h, wx_ref[:, sl])
        hs = hist[:, sl]
        acc = u * cw_ref[taps - 1:taps, sl] + cb_ref[:, sl]
        for j in range(1, taps):
            acc = acc + _shift_rows(u, hs, seg, j) * cw_ref[taps - 1 - j:taps - j, sl]
        xc_ref[:, sl] = _silu(acc)
        if seg == SUBLANES:
            tail_ref[:, :, sl] = u.reshape(tail_ref.shape[0], SUBLANES, COL_STRIP)
        else:
            last = u[u.shape[0] - SUBLANES:]
            carry_ref[:, sl] = last
            tail_ref[0, :, sl] = last
    for lo in range(0, wz_ref.shape[1], COL_STRIP):
        sl = slice(lo, lo + COL_STRIP)
        z_ref[:, sl] = _dot(h, wz_ref[:, sl]).astype(z_ref.dtype)
    dt_ref[...] = _softplus(_dot(h, wdt_ref[...]) + dtb_ref[...])


def _rwkv_in_body(x_ref, hist_ref, g1_ref, wrw_ref, wg_ref, mu_ref, wlora_ref, w0_ref, a0_ref,
                  kk_ref, ka_ref, seg_ref, exp_ref,
                  r_out, lw_out, k_out, v_out, kkn_out, bb_out, g_out, gates_out, tail_ref,
                  carry_ref, *, seg, dim):
    h = _rmsnorm(x_ref[...], g1_ref[...], NORM_EPS).astype(BF16)
    hist = _load_hist(hist_ref, carry_ref, seg)
    drw = wrw_ref.shape[1]

    def mixed(lo, hi):
        sl = slice(lo, hi)
        u = _dot(h, wrw_ref[:, sl])
        prev = _shift_rows(u, hist[:, sl], seg, 1)
        if seg == SUBLANES:
            tail_ref[:, :, sl] = u.reshape(tail_ref.shape[0], SUBLANES, hi - lo)
        else:
            last = u[u.shape[0] - SUBLANES:]
            carry_ref[:, sl] = last
            tail_ref[0, :, sl] = last
        return u + (prev - u) * mu_ref[:, sl]

    low = mixed(3 * dim, drw)
    lane = lax.broadcasted_iota(jnp.int32, low.shape, 1)
    lo_w = low.shape[1] // 4
    act = jnp.where(lane < lo_w, jnp.tanh(low), jnp.where(lane < 2 * lo_w, low, _sigmoid(low))).astype(BF16)
    r_out[...] = mixed(0, dim)
    wlog = -_softplus(-(w0_ref[...] + _dot(act, wlora_ref[:, 0:dim]))) - 0.5
    lw_out[...] = -jnp.exp(wlog)
    v_out[...] = mixed(2 * dim, 3 * dim)
    a = _sigmoid(a0_ref[...] + _dot(act, wlora_ref[:, dim:2 * dim]))
    k = mixed(dim, 2 * dim)
    kkr = k * kk_ref[...]
    ss = _dot_split_lhs(kkr * kkr, seg_ref[...], 2)
    inv = 1.0 / jnp.maximum(jnp.sqrt(ss), 1e-12)
    kkn = kkr * _dot_split_lhs(inv, exp_ref[...], 2)
    k_out[...] = k * (1.0 + (a - 1.0) * ka_ref[...])
    kkn_out[...] = kkn
    bb_out[...] = kkn * a
    g_out[...] = _dot(act, wlora_ref[:, 2 * dim:3 * dim]).astype(g_out.dtype)
    for lo in range(0, wg_ref.shape[1], COL_STRIP):
        sl = slice(lo, lo + COL_STRIP)
        gates_out[:, sl] = _sigmoid(_dot(h, wg_ref[:, sl])).astype(gates_out.dtype)


def _ssd_body(xm_ref, b_ref, c_ref, dt_ref, z_ref, st_ref, alog_ref, dexp_ref, ng_ref,
              tri_ref, e_ref, e2_ref, y_ref, so_ref, ht_ref, *, rows, heads, groups, seqs):
    chunk = pl.program_id(2)
    n_chunks = pl.num_programs(2)
    R = SSD_CHUNK
    width = xm_ref.shape[1] // groups
    state_n = b_ref.shape[1] // groups
    p_dim = width // heads
    gs = range(groups)
    steps = R // seqs

    def pad(v):
        if v.shape[0] == R:
            return v
        return jnp.concatenate([v, jnp.zeros((R - v.shape[0], v.shape[1]), v.dtype)], axis=0)

    def cols(ref, g, n):
        return ref[:, g * n:(g + 1) * n]

    row = lax.broadcasted_iota(jnp.int32, (R, R), 0)
    col = lax.broadcasted_iota(jnp.int32, (R, R), 1)
    lane = lax.broadcasted_iota(jnp.int32, (R, LANES), 1)
    expand = e_ref[...]
    expand2 = e2_ref[...]
    if seqs == 1:
        causal = row >= col
        tri = tri_ref[...]

        @pl.when(chunk == 0)
        def _():
            for g in gs:
                ht_ref[g] = st_ref[0, g * heads:(g + 1) * heads].reshape(width, state_n).T
    else:
        same = (row // steps) == (col // steps)
        causal = same & (row >= col)
        tri = causal.astype(BF16)

    xm = [pad(cols(xm_ref, g, width)) for g in gs]
    bm = [pad(cols(b_ref, g, state_n)) for g in gs]
    cm = [pad(cols(c_ref, g, state_n)) for g in gs]
    cmb = [cm[g].astype(BF16) for g in gs]
    dt = [pad(cols(dt_ref, g, LANES)) for g in gs]
    a = [dt[g] * (-jnp.exp(cols(alog_ref, g, LANES))) for g in gs]
    cs = [_dot_split_rhs(tri, a[g], 3) for g in gs]
    dt_e = [_dot_split_lhs(dt[g], expand, 3) for g in gs]
    cs_e = [_dot_split_lhs(cs[g], expand, 3) for g in gs]
    cs_e2 = [_dot_split_lhs(cs[g], expand2, 3) for g in gs]
    bt = [bm[g].T for g in gs]
    btb = [bt[g].astype(BF16) for g in gs]
    cs_t = [cs[g].T for g in gs]
    if seqs == 1:
        end_e = [cs_e[g][R - 1:R, :] for g in gs]
    else:
        tot = [_dot_split_rhs(same.astype(BF16), a[g], 3) for g in gs]
        end_e = [_dot_split_lhs(tot[g], expand, 3) for g in gs]
    xdt = [xm[g] * dt_e[g] for g in gs]
    xs = [(xdt[g] * jnp.exp(end_e[g] - cs_e[g])).astype(BF16) for g in gs]
    cb = [_dot(cmb[g], btb[g]) for g in gs]
    if seqs == 1:
        ht = [ht_ref[g] for g in gs]
        st_new = [_dot(btb[g], xs[g]) for g in gs]
        y_off = [_dot(cmb[g], ht[g].astype(BF16)) for g in gs]
        for g in gs:
            ht_ref[g] = ht[g] * jnp.exp(end_e[g]) + st_new[g]
    else:
        y_off = []
        for g in gs:
            tiles = []
            for s in range(seqs):
                r0 = s * steps
                ht0 = st_ref[s, g * heads:(g + 1) * heads].reshape(width, state_n).T
                tiles.append(_dot(cm[g][r0:r0 + steps].astype(BF16), ht0.astype(BF16)))
                bts = jnp.where((col // steps) == s, bt[g], 0.0).astype(BF16)
                ht1 = ht0 * jnp.exp(end_e[g][r0:r0 + 1]) + _dot(bts, xs[g])
                so_ref[s, g * heads:(g + 1) * heads] = ht1.T.reshape((heads,) + so_ref.shape[2:])
            y_off.append(jnp.concatenate(tiles, axis=0))
    ys = [[] for _ in gs]
    for j in range(heads // 2):
        lhs, rhs = [], []
        for g in gs:
            ms = []
            for hh in (2 * j, 2 * j + 1):
                seg_ = cs_e2[g][:, hh * R:(hh + 1) * R] - cs_t[g][hh:hh + 1, :]
                ms.append(jnp.where(causal, cb[g] * jnp.exp(jnp.where(causal, seg_, 0.0)), 0.0).astype(BF16))
            lhs.append(jnp.concatenate(ms, axis=1))
            xp = xdt[g][:, j * LANES:(j + 1) * LANES]
            rhs.append(jnp.concatenate([jnp.where(lane < p_dim, xp, 0.0), jnp.where(lane >= p_dim, xp, 0.0)],
                                       axis=0).astype(BF16))
        for g in gs:
            ys[g].append(_dot(lhs[g], rhs[g]))
    for g in gs:
        y = jnp.concatenate(ys[g], axis=1) + y_off[g] * jnp.exp(cs_e[g])
        y = y + cols(dexp_ref, g, width) * xm[g]
        yz = y * _silu(pad(cols(z_ref, g, width).astype(F32)))
        yn = yz * lax.rsqrt(jnp.mean(yz * yz, axis=-1, keepdims=True) + NORM_EPS) * cols(ng_ref, g, width)
        y_ref[:, g * width:(g + 1) * width] = yn[:rows].astype(y_ref.dtype)

    if seqs == 1:
        @pl.when(chunk == n_chunks - 1)
        def _():
            for g in gs:
                so_ref[0, g * heads:(g + 1) * heads] = ht_ref[g].T.reshape((heads,) + so_ref.shape[2:])


def _wkv_body(r_ref, lw_ref, k_ref, v_ref, kk_ref, bb_ref, g_ref, s0_ref, rk_ref, lnw_ref, lnb_ref,
              bo_ref, tri_ref, o_ref, so_ref, s_ref, *, head_dim):
    nseq, steps, dim = r_ref.shape
    npair = dim // LANES
    nchunk = steps // SUBLANES
    block_ones = bo_ref[...]
    tri = tri_ref[...]

    @pl.when(pl.program_id(1) == 0)
    def _():
        for si in range(nseq):
            for p in range(npair):
                x = s0_ref[si, 2 * p:2 * p + 2].reshape(2 * head_dim, head_dim)
                xp = jnp.concatenate([x, jnp.zeros((2 * head_dim, LANES - head_dim), F32)], axis=1)
                s_ref[si * npair + p] = xp.T[:head_dim]

    lane8 = lax.broadcasted_iota(jnp.int32, (SUBLANES, LANES), 1)
    row8 = lax.broadcasted_iota(jnp.int32, (SUBLANES, LANES), 0)
    head0 = lane8 < head_dim
    rowi = lax.broadcasted_iota(jnp.int32, (head_dim, LANES), 0)
    lanei = lax.broadcasted_iota(jnp.int32, (head_dim, LANES), 1)
    diag = (rowi == lanei % head_dim).astype(F32)

    def other_head(a):
        return pltpu.roll(a, head_dim, 1)

    def rows(a, c):
        return a[c * SUBLANES:(c + 1) * SUBLANES]

    def bc(tile, i):
        return jnp.broadcast_to(tile[i:i + 1], (SUBLANES, LANES))

    chains = []
    for si in range(nseq):
        for p in range(npair):
            sl = slice(p * LANES, (p + 1) * LANES)
            lw = lw_ref[si, :, sl]
            cum = _dot_split_rhs(tri, lw, 3)
            p_in = jnp.exp(cum)
            p_inv = jnp.exp(-cum)
            at = kk_ref[si, :, sl] * jnp.exp(cum - lw)
            rt = r_ref[si, :, sl] * p_in
            bt = bb_ref[si, :, sl] * p_inv
            kt = k_ref[si, :, sl] * p_inv
            at_o, rt_o, bt_o, kt_o = other_head(at), other_head(rt), other_head(bt), other_head(kt)
            x4 = jnp.concatenate([rows(a, c) for c in range(nchunk) for a in (bt, bt_o, kt, kt_o)], axis=0)
            if x4.shape[0] < LANES:
                x4 = jnp.concatenate([x4, jnp.zeros((LANES - x4.shape[0], LANES), F32)], axis=0)
            xt = x4.T[:head_dim].astype(BF16)
            chains.append(dict(si=si, sl=sl, at=at, rt=rt, bt=bt, kt=kt, at_o=at_o, rt_o=rt_o, xt=xt,
                               v=v_ref[si, :, sl], p_end=p_in[steps - 1:steps], s=s_ref[si * npair + p],
                               ys=[]))

    for c in range(nchunk):
        for ch in chains:
            at_c, rt_c, bt_c, kt_c = rows(ch['at'], c), rows(ch['rt'], c), rows(ch['bt'], c), rows(ch['kt'], c)
            tiles = []
            for i in range(SUBLANES):
                am = jnp.where(row8 > i, at_c, 0.0)
                rm = jnp.where(row8 >= i, rt_c, 0.0)
                bi, ki = bc(bt_c, i), bc(kt_c, i)
                tiles += [am * bi, am * ki, rm * bi, rm * ki]
            coef = _dot(jnp.concatenate(tiles, axis=0).astype(BF16), block_ones)
            ch['coef'] = [rows(coef, n) for n in range(4 * SUBLANES)]
            v_c = rows(ch['v'], c)
            va = None
            yv = None
            for i in range(SUBLANES):
                vi = bc(v_c, i)
                t_ak = ch['coef'][4 * i + 1] * vi
                t_rk = ch['coef'][4 * i + 3] * vi
                va = t_ak if va is None else va + t_ak
                yv = t_rk if yv is None else yv + t_rk
            ch['va'], ch['yv'], ch['v_c'] = va, yv, v_c
            lhs = jnp.concatenate([at_c[:, :head_dim], rows(ch['at_o'], c)[:, :head_dim],
                                   rt_c[:, :head_dim], rows(ch['rt_o'], c)[:, :head_dim]], axis=0)
            ch['lhs'] = lhs.astype(BF16)
        for ch in chains:
            ch['g'] = _dot(ch['lhs'], ch['s'].astype(BF16))
        for ch in chains:
            g = ch['g']
            g_a = jnp.where(head0, rows(g, 0), rows(g, 1))
            g_r = jnp.where(head0, rows(g, 2), rows(g, 3))
            sa = g_a + ch['va']
            y = g_r + ch['yv']
            for i in range(SUBLANES):
                sai = bc(sa, i)
                if i + 1 < SUBLANES:
                    sa = sa - ch['coef'][4 * i] * sai
                y = y - ch['coef'][4 * i + 2] * sai
            ch['ys'].append(y)
            v_c = ch['v_c']
            wd = jnp.concatenate([jnp.where(head0, -sa, 0.0), jnp.where(head0, 0.0, -sa),
                                  jnp.where(head0, v_c, 0.0), jnp.where(head0, 0.0, v_c)], axis=0)
            ch['wd'] = wd.astype(BF16)
        for ch in chains:
            cols = ch['xt'][:, c * 4 * SUBLANES:(c + 1) * 4 * SUBLANES]
            ch['s'] = ch['s'] + _dot(cols, ch['wd'])

    inv_n = 1.0 / head_dim
    for n, ch in enumerate(chains):
        si, sl = ch['si'], ch['sl']
        p_col = _dot_split_lhs(diag * ch['p_end'], block_ones, 3)
        s_new = ch['s'] * p_col
        s_ref[n] = s_new
        ch['s'] = s_new
        y = jnp.concatenate(ch['ys'], axis=0)
        mu = _dot_split_lhs(y, block_ones, 2) * inv_n
        d = y - mu
        var = _dot_split_lhs(d * d, block_ones, 2) * inv_n
        yn = d * lax.rsqrt(var + GN_EPS) * lnw_ref[:, sl] + lnb_ref[:, sl]
        bonus = _dot_split_lhs(r_ref[si, :, sl] * k_ref[si, :, sl] * rk_ref[:, sl], block_ones, 2)
        out = (yn + bonus * ch['v']) * g_ref[si, :, sl].astype(F32)
        o_ref[si, :, sl] = out.astype(o_ref.dtype)

    @pl.when(pl.program_id(1) == pl.num_programs(1) - 1)
    def _():
        for n, ch in enumerate(chains):
            sp = jnp.concatenate([ch['s'], jnp.zeros((LANES - head_dim, LANES), F32)], axis=0)
            back = sp.T[:, :head_dim]
            so_ref[n // npair, 2 * (n % npair):2 * (n % npair) + 2] = back.reshape(2, head_dim, head_dim)


def _post_body(x_ref, ya_ref, yb_ref, gt_ref, hist_ref, wa_ref, wb_ref, wo_ref, g2_ref, wup_ref,
               cw_ref, cb_ref, wdn_ref, gf_ref, o_ref, tail_ref, carry_ref, *, seg, final_norm):
    dm = x_ref.shape[1]
    ua = _dot(ya_ref[...].astype(BF16), wa_ref[...])
    ub = _dot(yb_ref[...].astype(BF16), wb_ref[...])
    gates = gt_ref[...].astype(F32)
    m = (gates[:, :dm] * ua + gates[:, dm:] * ub).astype(BF16)
    x1 = x_ref[...] + _dot(m, wo_ref[...])
    h2 = _rmsnorm(x1, g2_ref[...], NORM_EPS).astype(BF16)
    up = _dot(h2, wup_ref[...])
    dff = up.shape[1] // 2
    ug = up[:, :dff]
    hist = _load_hist(hist_ref, carry_ref, seg)
    ugc = _causal_conv(ug, hist, seg, cw_ref, cb_ref)
    _store_tail(ug, tail_ref, carry_ref, seg)
    act = (_silu(ugc) * up[:, dff:]).astype(BF16)
    x2 = x1 + _dot(act, wdn_ref[...])
    if final_norm:
        x2 = _rmsnorm(x2, gf_ref[...], NORM_EPS)
    o_ref[...] = x2


def _const_spec(shape):
    nd = len(shape)
    return pl.BlockSpec(shape, lambda *_: (0,) * nd, pipeline_mode=pl.Buffered(1))


def _params(sem):
    return pltpu.CompilerParams(dimension_semantics=sem, vmem_limit_bytes=VMEM_LIMIT)


def _token_tiling(nseq, seqlen):
    if seqlen == SUBLANES:
        per = ROW_TILE // SUBLANES
        assert nseq % per == 0
        return per, SUBLANES, (nseq // per, 1)
    assert seqlen % ROW_TILE == 0
    return 1, ROW_TILE, (nseq, seqlen // ROW_TILE)


def _row_spec(cols, lt):
    return pl.BlockSpec((ROW_TILE, cols), lambda i, l: (i * lt + l, 0))


def _hist_spec(per, cols):
    return pl.BlockSpec((per, SUBLANES, cols), lambda i, l: (i, 0, 0))


def _ssm_in(x, hist, p, nseq, seqlen, act_dtype):
    per, seg, grid = _token_tiling(nseq, seqlen)
    t, dm = x.shape
    dz = p['wz'].shape[1]
    dc = p['wx'].shape[1]
    dd = p['wdt'].shape[1]
    lt = grid[1]
    return pl.pallas_call(
        functools.partial(_ssm_in_body, seg=seg),
        grid=grid,
        in_specs=[_row_spec(dm, lt), _hist_spec(per, dc), _const_spec((1, dm)),
                  _const_spec(p['wz'].shape), _const_spec(p['wx'].shape), _const_spec(p['wdt'].shape),
                  _const_spec(p['conv_w'].shape), _const_spec((1, dc)), _const_spec((1, dd))],
        out_specs=[_row_spec(dz, lt), _row_spec(dc, lt), _row_spec(dd, lt), _hist_spec(per, dc)],
        out_shape=[jax.ShapeDtypeStruct((t, dz), act_dtype), jax.ShapeDtypeStruct((t, dc), F32),
                   jax.ShapeDtypeStruct((t, dd), F32), jax.ShapeDtypeStruct((nseq, SUBLANES, dc), F32)],
        scratch_shapes=[pltpu.VMEM((SUBLANES, dc), F32)],
        compiler_params=_params(("arbitrary", "arbitrary")),
        name="ssm_in",
    )(x, hist, p['norm1_g'], p['wz'], p['wx'], p['wdt'], p['conv_w'], p['conv_b'], p['dt_bias'])


def _rwkv_in(x, hist, p, nseq, seqlen, act_dtype):
    per, seg, grid = _token_tiling(nseq, seqlen)
    t, dm = x.shape
    dim = p['w0'].shape[1]
    drw = p['wrw'].shape[1]
    dg = p['wg'].shape[1]
    lt = grid[1]
    f32_out = jax.ShapeDtypeStruct((t, dim), F32)
    return pl.pallas_call(
        functools.partial(_rwkv_in_body, seg=seg, dim=dim),
        grid=grid,
        in_specs=[_row_spec(dm, lt), _hist_spec(per, drw), _const_spec((1, dm)),
                  _const_spec(p['wrw'].shape), _const_spec(p['wg'].shape), _const_spec((1, drw)),
                  _const_spec(p['wlora'].shape), _const_spec((1, dim)), _const_spec((1, dim)),
                  _const_spec((1, dim)), _const_spec((1, dim)),
                  _const_spec(p['head_sum'].shape), _const_spec(p['head_expand'].shape)],
        out_specs=[_row_spec(dim, lt)] * 7 + [_row_spec(dg, lt), _hist_spec(per, drw)],
        out_shape=[f32_out] * 6 + [jax.ShapeDtypeStruct((t, dim), act_dtype),
                                   jax.ShapeDtypeStruct((t, dg), BF16),
                                   jax.ShapeDtypeStruct((nseq, SUBLANES, drw), F32)],
        scratch_shapes=[pltpu.VMEM((SUBLANES, drw), F32)],
        compiler_params=_params(("arbitrary", "arbitrary")),
        name="rwkv_in",
    )(x, hist, p['norm1_g'], p['wrw'], p['wg'], p['mu'], p['wlora'], p['w0'], p['a0'],
      p['k_k'], p['k_a'], p['head_sum'], p['head_expand'])


def _ssd(xc, dt, z, state, p, nseq, seqlen, act_dtype):
    t = xc.shape[0]
    _, n_heads, p_dim, state_n = state.shape
    groups = dt.shape[1] // LANES
    heads = n_heads // groups
    width = heads * p_dim
    d_inner = groups * width
    assert state_n == LANES and width % LANES == 0 and p_dim * 2 == LANES
    if seqlen % SSD_CHUNK == 0:
        seqs, gps, rows, nc = 1, groups, SSD_CHUNK, seqlen // SSD_CHUNK
    elif seqlen == SUBLANES and nseq % (SSD_CHUNK // SUBLANES) == 0:
        seqs, gps, rows, nc = SSD_CHUNK // SUBLANES, 1, SSD_CHUNK, 1
    else:
        assert seqlen == SUBLANES
        seqs, gps, rows, nc = 1, groups, SUBLANES, 1
    gn = gps * state_n
    b_blk = d_inner // gn
    assert d_inner % gn == 0
    row_map = lambda b, g, c: (b * nc + c, g)
    st_spec = pl.BlockSpec((seqs, gps * heads, p_dim, state_n), lambda b, g, c: (b, g, 0, 0))
    return pl.pallas_call(
        functools.partial(_ssd_body, rows=rows, heads=heads, groups=gps, seqs=seqs),
        grid=(nseq // seqs, groups // gps, nc),
        in_specs=[pl.BlockSpec((rows, gps * width), row_map),
                  pl.BlockSpec((rows, gn), lambda b, g, c: (b * nc + c, b_blk + g)),
                  pl.BlockSpec((rows, gn), lambda b, g, c: (b * nc + c, b_blk + groups // gps + g)),
                  pl.BlockSpec((rows, gps * LANES), row_map),
                  pl.BlockSpec((rows, gps * width), row_map),
                  st_spec,
                  pl.BlockSpec((1, gps * LANES), lambda b, g, c: (0, g)),
                  pl.BlockSpec((1, gps * width), lambda b, g, c: (0, g)),
                  pl.BlockSpec((1, gps * width), lambda b, g, c: (0, g)),
                  _const_spec(p['tri'].shape), _const_spec(p['ssd_expand'].shape),
                  _const_spec(p['ssd_expand2'].shape)],
        out_specs=[pl.BlockSpec((rows, gps * width), row_map), st_spec],
        out_shape=[jax.ShapeDtypeStruct((t, d_inner), act_dtype),
                   jax.ShapeDtypeStruct(state.shape, F32)],
        scratch_shapes=[pltpu.VMEM((gps, state_n, width), F32)],
        compiler_params=_params(("arbitrary", "arbitrary", "arbitrary")),
        name="ssd",
    )(xc, xc, xc, dt, z, state, p['a_log'], p['d_exp'], p['ssm_norm_g'],
      p['tri'], p['ssd_expand'], p['ssd_expand2'])


def _wkv(r, lw, k, v, kk, bb, g, state, p, nseq, seqlen, act_dtype):
    dim = r.shape[1]
    _, n_heads, head_dim, _ = state.shape
    npair = dim // LANES
    assert n_heads == 2 * npair and 2 * head_dim == LANES
    if seqlen == SUBLANES:
        per, steps = WKV_SAMPLE_SEQS, SUBLANES
    else:
        per, steps = (WKV_PROMPT_SEQS if nseq % WKV_PROMPT_SEQS == 0 else 1), WKV_TBLOCK
    assert nseq % per == 0 and seqlen % steps == 0
    grid = (nseq // per, seqlen // steps)
    seq_spec = pl.BlockSpec((per, steps, dim), lambda i, tb: (i, tb, 0))
    st_spec = pl.BlockSpec((per, n_heads, head_dim, head_dim), lambda i, tb: (i, 0, 0, 0))
    as3 = lambda a: a.reshape(nseq, seqlen, dim)
    out, s_out = pl.pallas_call(
        functools.partial(_wkv_body, head_dim=head_dim),
        grid=grid,
        in_specs=[seq_spec] * 7 + [st_spec, _const_spec((1, dim)), _const_spec((1, dim)),
                                   _const_spec((1, dim)), _const_spec((LANES, LANES)),
                                   _const_spec((steps, steps))],
        out_specs=[seq_spec, st_spec],
        out_shape=[jax.ShapeDtypeStruct((nseq, seqlen, dim), act_dtype),
                   jax.ShapeDtypeStruct(state.shape, F32)],
        scratch_shapes=[pltpu.VMEM((per * npair, head_dim, LANES), F32)],
        compiler_params=_params(("arbitrary", "arbitrary")),
        name="wkv",
    )(as3(r), as3(lw), as3(k), as3(v), as3(kk), as3(bb), as3(g), state,
      p['r_k'], p['ln_w'], p['ln_b'], p['block_ones'], p['tri'][:steps, :steps])
    return out.reshape(nseq * seqlen, dim), s_out


def _post(x, ya, yb, gates, hist, p, nseq, seqlen, final_norm):
    per, seg, grid = _token_tiling(nseq, seqlen)
    t, dm = x.shape
    dff = p['wdn'].shape[0]
    lt = grid[1]
    return pl.pallas_call(
        functools.partial(_post_body, seg=seg, final_norm=final_norm),
        grid=grid,
        in_specs=[_row_spec(dm, lt), _row_spec(ya.shape[1], lt), _row_spec(yb.shape[1], lt),
                  _row_spec(gates.shape[1], lt), _hist_spec(per, dff),
                  _const_spec(p['wa'].shape), _const_spec(p['wb'].shape), _const_spec(p['wo'].shape),
                  _const_spec((1, dm)), _const_spec(p['wup'].shape), _const_spec(p['ffn_conv_w'].shape),
                  _const_spec((1, dff)), _const_spec(p['wdn'].shape), _const_spec((1, dm))],
        out_specs=[_row_spec(dm, lt), _hist_spec(per, dff)],
        out_shape=[jax.ShapeDtypeStruct((t, dm), F32), jax.ShapeDtypeStruct((nseq, SUBLANES, dff), F32)],
        scratch_shapes=[pltpu.VMEM((SUBLANES, dff), F32)],
        compiler_params=_params(("arbitrary", "arbitrary")),
        name="post",
    )(x, ya, yb, gates, hist, p['wa'], p['wb'], p['wo'], p['norm2_g'], p['wup'], p['ffn_conv_w'],
      p['ffn_conv_b'], p['wdn'], p['final_g'])


def _prep_layer(w, dims):
    d_inner, conv_dim, n_heads, shift_dim, groups, rwkv_dim, rwkv_heads = dims
    row = lambda a: a.reshape(1, -1).astype(F32)
    w_in = w['w_in']
    o1 = d_inner
    o2 = o1 + conv_dim
    o3 = o2 + n_heads
    o4 = o3 + shift_dim
    hpg = n_heads // groups

    def per_group(a):
        lead = a.shape[:-1]
        a = a.reshape(lead + (groups, hpg))
        a = jnp.pad(a, [(0, 0)] * len(lead) + [(0, 0), (0, LANES - hpg)])
        return a.reshape(lead + (groups * LANES,))

    p_dim = d_inner // n_heads
    head_dim = rwkv_dim // rwkv_heads
    lora_w = w['rwkv_w_up'].shape[0]
    lora_a = w['rwkv_a_up'].shape[0]
    lora_g = w['rwkv_g_up'].shape[0]
    assert lora_w == lora_a and lora_g == 2 * lora_w
    wlora = jnp.zeros((lora_w + lora_a + lora_g, 3 * rwkv_dim), F32)
    wlora = wlora.at[:lora_w, :rwkv_dim].set(w['rwkv_w_up'])
    wlora = wlora.at[lora_w:lora_w + lora_a, rwkv_dim:2 * rwkv_dim].set(w['rwkv_a_up'])
    wlora = wlora.at[lora_w + lora_a:, 2 * rwkv_dim:].set(w['rwkv_g_up'])

    ch = jnp.arange(rwkv_dim) // head_dim
    head_sum = (ch[:, None] == jnp.arange(LANES)[None, :]).astype(BF16)
    li = jnp.arange(LANES)
    width = hpg * p_dim
    ssd_expand = (li[:, None] == (jnp.arange(width) // p_dim)[None, :]).astype(BF16)
    ssd_expand2 = (li[:, None] == (jnp.arange(hpg * SSD_CHUNK) // SSD_CHUNK)[None, :]).astype(BF16)
    ci = jnp.arange(SSD_CHUNK)
    tri = (ci[:, None] >= ci[None, :]).astype(BF16)
    block_ones = ((li[:, None] // head_dim) == (li[None, :] // head_dim)).astype(BF16)

    return {
        'norm1_g': row(w['norm1_g']),
        'wz': w_in[:, :o1].astype(BF16),
        'wx': w_in[:, o1:o2].astype(BF16),
        'wdt': per_group(w_in[:, o2:o3]).astype(BF16),
        'wrw': w_in[:, o3:o4].astype(BF16),
        'wg': w_in[:, o4:].astype(BF16),
        'conv_w': w['ssm_conv_w'].astype(F32),
        'conv_b': row(w['ssm_conv_b']),
        'dt_bias': row(per_group(w['ssm_dt_bias'])),
        'a_log': row(per_group(w['ssm_a_log'])),
        'd_exp': row(jnp.repeat(w['ssm_d'], p_dim)),
        'ssm_norm_g': row(w['ssm_norm_g']),
        'wa': w['w_branch_a'].astype(BF16),
        'mu': row(w['rwkv_mu']),
        'wlora': wlora.astype(BF16),
        'w0': row(w['rwkv_w0']),
        'a0': row(w['rwkv_a0']),
        'k_k': row(w['rwkv_k_k']),
        'k_a': row(w['rwkv_k_a']),
        'r_k': row(w['rwkv_r_k']),
        'ln_w': row(w['rwkv_ln_w']),
        'ln_b': row(w['rwkv_ln_b']),
        'wb': w['w_branch_b'].astype(BF16),
        'wo': w['w_out'].astype(BF16),
        'norm2_g': row(w['norm2_g']),
        'wup': w['ffn_w_up'].astype(BF16),
        'ffn_conv_w': w['ffn_conv_w'].astype(F32),
        'ffn_conv_b': row(w['ffn_conv_b']),
        'wdn': w['ffn_w_down'].astype(BF16),
        'head_sum': head_sum,
        'head_expand': head_sum.T,
        'ssd_expand': ssd_expand,
        'ssd_expand2': ssd_expand2,
        'tri': tri,
        'block_ones': block_ones,
    }


def _hist8(state_rows):
    k = state_rows.shape[1]
    return jnp.pad(state_rows.astype(F32), ((0, 0), (SUBLANES - k, 0), (0, 0)))


def _layer(x, conv_buf, ssm_state, shift_buf, wkv_state, ffn_buf, p, final_norm):
    nseq, seqlen, dm = x.shape
    act_dtype = BF16 if seqlen % 16 == 0 else F32
    xf = x.reshape(nseq * seqlen, dm)
    z, xc, dt, conv_tail = _ssm_in(xf, _hist8(conv_buf), p, nseq, seqlen, act_dtype)
    r, lw, k, v, kk, bb, g, gates, shift_tail = _rwkv_in(xf, _hist8(shift_buf[:, None]), p, nseq, seqlen,
                                                         act_dtype)
    ya, new_ssm = _ssd(xc, dt, z, ssm_state.astype(F32), p, nseq, seqlen, act_dtype)
    yb, new_wkv = _wkv(r, lw, k, v, kk, bb, g, wkv_state.astype(F32), p, nseq, seqlen, act_dtype)
    out, ffn_tail = _post(xf, ya, yb, gates, _hist8(ffn_buf), p, nseq, seqlen, final_norm)
    new_conv = conv_tail[:, SUBLANES - conv_buf.shape[1]:]
    new_shift = shift_tail[:, SUBLANES - 1]
    new_ffn = ffn_tail[:, SUBLANES - ffn_buf.shape[1]:]
    return out.reshape(nseq, seqlen, dm), (new_conv, new_ssm, new_shift, new_wkv, new_ffn)


_LAYER_WEIGHTS = ('norm1_g', 'w_in', 'ssm_conv_w', 'ssm_conv_b', 'ssm_dt_bias', 'ssm_a_log', 'ssm_d',
                  'ssm_norm_g', 'w_branch_a', 'rwkv_mu', 'rwkv_w0', 'rwkv_w_up', 'rwkv_a0', 'rwkv_a_up',
                  'rwkv_g_up', 'rwkv_k_k', 'rwkv_k_a', 'rwkv_r_k', 'rwkv_ln_w', 'rwkv_ln_b', 'w_branch_b',
                  'w_out', 'norm2_g', 'ffn_w_up', 'ffn_conv_w', 'ffn_conv_b', 'ffn_w_down')


def kernel(x_prompt, x_sample, state_ssm_conv, state_ssm, state_rwkv_shift, state_rwkv, state_ffn_conv,
           norm1_g, w_in, ssm_conv_w, ssm_conv_b, ssm_dt_bias, ssm_a_log, ssm_d, ssm_norm_g, w_branch_a,
           rwkv_mu, rwkv_w0, rwkv_w_up, rwkv_a0, rwkv_a_up, rwkv_g_up, rwkv_k_k, rwkv_k_a, rwkv_r_k,
           rwkv_ln_w, rwkv_ln_b, w_branch_b, w_out, norm2_g, ffn_w_up, ffn_conv_w, ffn_conv_b, ffn_w_down,
           final_g):
    stacked = dict(zip(_LAYER_WEIGHTS, (
        norm1_g, w_in, ssm_conv_w, ssm_conv_b, ssm_dt_bias, ssm_a_log, ssm_d, ssm_norm_g, w_branch_a,
        rwkv_mu, rwkv_w0, rwkv_w_up, rwkv_a0, rwkv_a_up, rwkv_g_up, rwkv_k_k, rwkv_k_a, rwkv_r_k,
        rwkv_ln_w, rwkv_ln_b, w_branch_b, w_out, norm2_g, ffn_w_up, ffn_conv_w, ffn_conv_b, ffn_w_down)))
    depth = w_in.shape[0]
    _, _, n_heads, _, _ = state_ssm.shape
    conv_dim = state_ssm_conv.shape[-1]
    d_inner = w_branch_a.shape[1]
    shift_dim = state_rwkv_shift.shape[-1]
    rwkv_heads = state_rwkv.shape[2]
    rwkv_dim = w_branch_b.shape[1]
    groups = (conv_dim - d_inner) // (2 * state_ssm.shape[-1])
    dims = (d_inner, conv_dim, n_heads, shift_dim, groups, rwkv_dim, rwkv_heads)

    xp, xs = x_prompt, x_sample
    bp = xp.shape[0]
    new_p = ([], [], [], [], [])
    new_s = ([], [], [], [], [])
    for i in range(depth):
        p = _prep_layer({name: a[i] for name, a in stacked.items()}, dims)
        p['final_g'] = final_g.reshape(1, -1).astype(F32)
        last = i == depth - 1
        xp, sp = _layer(
            xp,
            jnp.zeros((bp,) + state_ssm_conv.shape[2:], F32),
            jnp.zeros((bp,) + state_ssm.shape[2:], F32),
            jnp.zeros((bp,) + state_rwkv_shift.shape[2:], F32),
            jnp.zeros((bp,) + state_rwkv.shape[2:], F32),
            jnp.zeros((bp,) + state_ffn_conv.shape[2:], F32),
            p, last)
        xs, ss = _layer(xs, state_ssm_conv[i], state_ssm[i], state_rwkv_shift[i], state_rwkv[i],
                        state_ffn_conv[i], p, last)
        for j in range(5):
            new_p[j].append(sp[j])
            new_s[j].append(ss[j])
    return (xp, xs,
            jnp.stack(new_p[0]), jnp.stack(new_p[1]), jnp.stack(new_p[2]), jnp.stack(new_p[3]),
            jnp.stack(new_p[4]),
            jnp.stack(new_s[0]), jnp.stack(new_s[1]), jnp.stack(new_s[2]), jnp.stack(new_s[3]),
            jnp.stack(new_s[4]))
```

```python
import functools

import jax
import jax.numpy as jnp
from jax import lax
from jax.experimental import pallas as pl
from jax.experimental.pallas import tpu as pltpu

F32 = jnp.float32
BF16 = jnp.bfloat16

NORM_EPS = 1e-5
GN_EPS = 64e-5

LANES = 128
SUBLANES = 8
ROW_TILE = 256
COL_STRIP = 512
SSD_CHUNK = 128
WKV_TBLOCK = 64
WKV_PROMPT_SEQS = 4
WKV_SAMPLE_SEQS = 4
VMEM_LIMIT = 56 * 1024 * 1024


def _dot(a, b):
    return jnp.dot(a, b, preferred_element_type=F32)


def _split(x, n):
    parts = []
    rem = x
    for i in range(n):
        p = rem.astype(BF16)
        parts.append(p)
        if i + 1 < n:
            rem = rem - p.astype(F32)
    return parts


def _dot_split_lhs(x, m, n):
    acc = None
    for p in _split(x, n):
        d = _dot(p, m)
        acc = d if acc is None else acc + d
    return acc


def _dot_split_rhs(m, x, n):
    acc = None
    for p in _split(x, n):
        d = _dot(m, p)
        acc = d if acc is None else acc + d
    return acc


def _sigmoid(x):
    return 1.0 / (1.0 + jnp.exp(-x))


def _silu(x):
    return x * _sigmoid(x)


def _softplus(x):
    return jnp.maximum(x, 0.0) + jnp.log1p(jnp.exp(-jnp.abs(x)))


def _rmsnorm(x, g, eps):
    ms = jnp.mean(x * x, axis=-1, keepdims=True)
    return x * lax.rsqrt(ms + eps) * g


def _shift_rows(u, hist, seg, j):
    rows = u.shape[0]
    ru = pltpu.roll(u, j, 0)
    if seg == SUBLANES:
        rh = pltpu.roll(hist, (rows - SUBLANES + j) % rows, 0)
        pos = lax.broadcasted_iota(jnp.int32, u.shape, 0) % SUBLANES
        return jnp.where(pos < j, rh, ru)
    assert seg == rows
    rh = pltpu.roll(hist, j, 0)
    pos = lax.broadcasted_iota(jnp.int32, rh.shape, 0)
    top = jnp.where(pos < j, rh, ru[:SUBLANES])
    return jnp.concatenate([top, ru[SUBLANES:]], axis=0)


def _causal_conv(u, hist, seg, w_ref, b_ref):
    taps = w_ref.shape[0]
    acc = u * w_ref[taps - 1:taps, :] + b_ref[...]
    for j in range(1, taps):
        acc = acc + _shift_rows(u, hist, seg, j) * w_ref[taps - 1 - j:taps - j, :]
    return acc


def _load_hist(hist_ref, carry_ref, seg):
    if seg == SUBLANES:
        nseq, _, c = hist_ref.shape
        return hist_ref[...].reshape(nseq * SUBLANES, c)

    @pl.when(pl.program_id(1) == 0)
    def _():
        carry_ref[...] = hist_ref[0]

    return carry_ref[...]


def _store_tail(u, tail_ref, carry_ref, seg):
    if seg == SUBLANES:
        tail_ref[...] = u.reshape(tail_ref.shape)
    else:
        last = u[u.shape[0] - SUBLANES:]
        carry_ref[...] = last
        tail_ref[0] = last


def _ssm_in_body(x_ref, hist_ref, g1_ref, wz_ref, wx_ref, wdt_ref, cw_ref, cb_ref, dtb_ref,
                 z_ref, xc_ref, dt_ref, tail_ref, carry_ref, *, seg):
    h = _rmsnorm(x_ref[...], g1_ref[...], NORM_EPS).astype(BF16)
    hist = _load_hist(hist_ref, carry_ref, seg)
    taps = cw_ref.shape[0]
    for lo in range(0, wx_ref.shape[1], COL_STRIP):
        sl = slice(lo, lo + COL_STRIP)
        u = _dot(h, wx_ref[:, sl])
        hs = hist[:, sl]
        acc = u * cw_ref[taps - 1:taps, sl] + cb_ref[:, sl]
        for j in range(1, taps):
            acc = acc + _shift_rows(u, hs, seg, j) * cw_ref[taps - 1 - j:taps - j, sl]
        xc_ref[:, sl] = _silu(acc)
        if seg == SUBLANES:
            tail_ref[:, :, sl] = u.reshape(tail_ref.shape[0], SUBLANES, COL_STRIP)
        else:
            last = u[u.shape[0] - SUBLANES:]
            carry_ref[:, sl] = last
            tail_ref[0, :, sl] = last
    for lo in range(0, wz_ref.shape[1], COL_STRIP):
        sl = slice(lo, lo + COL_STRIP)
        z_ref[:, sl] = _dot(h, wz_ref[:, sl]).astype(z_ref.dtype)
    dt_ref[...] = _softplus(_dot(h, wdt_ref[...]) + dtb_ref[...])


def _rwkv_in_body(x_ref, hist_ref, g1_ref, wrw_ref, wg_ref, mu_ref, wlora_ref, w0_ref, a0_ref,
                  kk_ref, ka_ref, seg_ref, exp_ref,
                  r_out, lw_out, k_out, v_out, kkn_out, bb_out, g_out, gates_out, tail_ref,
                  carry_ref, *, seg, dim):
    h = _rmsnorm(x_ref[...], g1_ref[...], NORM_EPS).astype(BF16)
    hist = _load_hist(hist_ref, carry_ref, seg)
    drw = wrw_ref.shape[1]

    def mixed(lo, hi):
        sl = slice(lo, hi)
        u = _dot(h, wrw_ref[:, sl])
        prev = _shift_rows(u, hist[:, sl], seg, 1)
        if seg == SUBLANES:
            tail_ref[:, :, sl] = u.reshape(tail_ref.shape[0], SUBLANES, hi - lo)
        else:
            last = u[u.shape[0] - SUBLANES:]
            carry_ref[:, sl] = last
            tail_ref[0, :, sl] = last
        return u + (prev - u) * mu_ref[:, sl]

    low = mixed(3 * dim, drw)
    lane = lax.broadcasted_iota(jnp.int32, low.shape, 1)
    lo_w = low.shape[1] // 4
    act = jnp.where(lane < lo_w, jnp.tanh(low), jnp.where(lane < 2 * lo_w, low, _sigmoid(low))).astype(BF16)
    r_out[...] = mixed(0, dim)
    wlog = -_softplus(-(w0_ref[...] + _dot(act, wlora_ref[:, 0:dim]))) - 0.5
    lw_out[...] = -jnp.exp(wlog)
    v_out[...] = mixed(2 * dim, 3 * dim)
    a = _sigmoid(a0_ref[...] + _dot(act, wlora_ref[:, dim:2 * dim]))
    k = mixed(dim, 2 * dim)
    kkr = k * kk_ref[...]
    ss = _dot_split_lhs(kkr * kkr, seg_ref[...], 2)
    inv = 1.0 / jnp.maximum(jnp.sqrt(ss), 1e-12)
    kkn = kkr * _dot_split_lhs(inv, exp_ref[...], 2)
    k_out[...] = k * (1.0 + (a - 1.0) * ka_ref[...])
    kkn_out[...] = kkn
    bb_out[...] = kkn * a
    g_out[...] = _dot(act, wlora_ref[:, 2 * dim:3 * dim]).astype(g_out.dtype)
    for lo in range(0, wg_ref.shape[1], COL_STRIP):
        sl = slice(lo, lo + COL_STRIP)
        gates_out[:, sl] = _sigmoid(_dot(h, wg_ref[:, sl])).astype(gates_out.dtype)


def _ssd_body(xm_ref, b_ref, c_ref, dt_ref, z_ref, st_ref, alog_ref, dexp_ref, ng_ref,
              tri_ref, e_ref, e2_ref, y_ref, so_ref, ht_ref, *, rows, heads, groups, seqs):
    chunk = pl.program_id(2)
    n_chunks = pl.num_programs(2)
    R = SSD_CHUNK
    width = xm_ref.shape[1] // groups
    state_n = b_ref.shape[1] // groups
    p_dim = width // heads
    gs = range(groups)
    steps = R // seqs

    def pad(v):
        if v.shape[0] == R:
            return v
        return jnp.concatenate([v, jnp.zeros((R - v.shape[0], v.shape[1]), v.dtype)], axis=0)

    def cols(ref, g, n):
        return ref[:, g * n:(g + 1) * n]

    row = lax.broadcasted_iota(jnp.int32, (R, R), 0)
    col = lax.broadcasted_iota(jnp.int32, (R, R), 1)
    lane = lax.broadcasted_iota(jnp.int32, (R, LANES), 1)
    expand = e_ref[...]
    expand2 = e2_ref[...]
    if seqs == 1:
        causal = row >= col
        tri = tri_ref[...]

        @pl.when(chunk == 0)
        def _():
            for g in gs:
                ht_ref[g] = st_ref[0, g * heads:(g + 1) * heads].reshape(width, state_n).T
    else:
        same = (row // steps) == (col // steps)
        causal = same & (row >= col)
        tri = causal.astype(BF16)

    xm = [pad(cols(xm_ref, g, width)) for g in gs]
    bm = [pad(cols(b_ref, g, state_n)) for g in gs]
    cm = [pad(cols(c_ref, g, state_n)) for g in gs]
    cmb = [cm[g].astype(BF16) for g in gs]
    dt = [pad(cols(dt_ref, g, LANES)) for g in gs]
    a = [dt[g] * (-jnp.exp(cols(alog_ref, g, LANES))) for g in gs]
    cs = [_dot_split_rhs(tri, a[g], 3) for g in gs]
    dt_e = [_dot_split_lhs(dt[g], expand, 2) for g in gs]
    cs_e = [_dot_split_lhs(cs[g], expand, 3) for g in gs]
    if seqs > 1:
        cs_e2 = [_dot_split_lhs(cs[g], expand2, 3) for g in gs]
    bt = [bm[g].T for g in gs]
    btb = [bt[g].astype(BF16) for g in gs]
    cs_t = [cs[g].T for g in gs]
    if seqs == 1:
        end_e = [cs_e[g][R - 1:R, :] for g in gs]
    else:
        tot = [_dot_split_rhs(same.astype(BF16), a[g], 3) for g in gs]
        end_e = [_dot_split_lhs(tot[g], expand, 3) for g in gs]
    xdt = [xm[g] * dt_e[g] for g in gs]
    xs = [(xdt[g] * jnp.exp(end_e[g] - cs_e[g])).astype(BF16) for g in gs]
    cb = [_dot(cmb[g], btb[g]) for g in gs]
    if seqs == 1:
        ht = [ht_ref[g] for g in gs]
        st_new = [_dot(btb[g], xs[g]) for g in gs]
        y_off = [_dot(cmb[g], ht[g].astype(BF16)) for g in gs]
        for g in gs:
            ht_ref[g] = ht[g] * jnp.exp(end_e[g]) + st_new[g]
    else:
        y_off = []
        for g in gs:
            tiles = []
            for s in range(seqs):
                r0 = s * steps
                ht0 = st_ref[s, g * heads:(g + 1) * heads].reshape(width, state_n).T
                tiles.append(_dot(cm[g][r0:r0 + steps].astype(BF16), ht0.astype(BF16)))
                bts = jnp.where((col // steps) == s, bt[g], 0.0).astype(BF16)
                ht1 = ht0 * jnp.exp(end_e[g][r0:r0 + 1]) + _dot(bts, xs[g])
                so_ref[s, g * heads:(g + 1) * heads] = ht1.T.reshape((heads,) + so_ref.shape[2:])
            y_off.append(jnp.concatenate(tiles, axis=0))
    ys = [[] for _ in gs]
    for j in range(heads // 2):
        lhs, rhs = [], []
        for g in gs:
            ms = []
            for hh in (2 * j, 2 * j + 1):
                if seqs == 1:
                    cs_col = jnp.broadcast_to(cs[g][:, hh:hh + 1], (R, R))
                else:
                    cs_col = cs_e2[g][:, hh * R:(hh + 1) * R]
                seg_ = cs_col - cs_t[g][hh:hh + 1, :]
                ms.append(jnp.where(causal, cb[g] * jnp.exp(jnp.where(causal, seg_, 0.0)), 0.0).astype(BF16))
            lhs.append(jnp.concatenate(ms, axis=1))
            xp = xdt[g][:, j * LANES:(j + 1) * LANES]
            rhs.append(jnp.concatenate([jnp.where(lane < p_dim, xp, 0.0), jnp.where(lane >= p_dim, xp, 0.0)],
                                       axis=0).astype(BF16))
        for g in gs:
            ys[g].append(_dot(lhs[g], rhs[g]))
    for g in gs:
        y = jnp.concatenate(ys[g], axis=1) + y_off[g] * jnp.exp(cs_e[g])
        y = y + cols(dexp_ref, g, width) * xm[g]
        yz = y * _silu(pad(cols(z_ref, g, width).astype(F32)))
        yn = yz * lax.rsqrt(jnp.mean(yz * yz, axis=-1, keepdims=True) + NORM_EPS) * cols(ng_ref, g, width)
        y_ref[:, g * width:(g + 1) * width] = yn[:rows].astype(y_ref.dtype)

    if seqs == 1:
        @pl.when(chunk == n_chunks - 1)
        def _():
            for g in gs:
                so_ref[0, g * heads:(g + 1) * heads] = ht_ref[g].T.reshape((heads,) + so_ref.shape[2:])


def _wkv_body(r_ref, lw_ref, k_ref, v_ref, kk_ref, bb_ref, g_ref, s0_ref, rk_ref, lnw_ref, lnb_ref,
              bo_ref, tri_ref, o_ref, so_ref, s_ref, *, head_dim, single):
    nseq, steps, dim = r_ref.shape
    npair = dim // LANES
    nchunk = steps // SUBLANES
    block_ones = bo_ref[...]
    tri = tri_ref[...]

    def load_state(si, p):
        x = s0_ref[si, 2 * p:2 * p + 2].reshape(2 * head_dim, head_dim)
        xp = jnp.concatenate([x, jnp.zeros((2 * head_dim, LANES - head_dim), F32)], axis=1)
        return xp.T[:head_dim]

    def store_state(n, s):
        sp = jnp.concatenate([s, jnp.zeros((LANES - head_dim, LANES), F32)], axis=0)
        back = sp.T[:, :head_dim]
        so_ref[n // npair, 2 * (n % npair):2 * (n % npair) + 2] = back.reshape(2, head_dim, head_dim)

    if not single:
        @pl.when(pl.program_id(1) == 0)
        def _():
            for si in range(nseq):
                for p in range(npair):
                    s_ref[si * npair + p] = load_state(si, p)

    lane8 = lax.broadcasted_iota(jnp.int32, (SUBLANES, LANES), 1)
    row8 = lax.broadcasted_iota(jnp.int32, (SUBLANES, LANES), 0)
    head0 = lane8 < head_dim
    rowi = lax.broadcasted_iota(jnp.int32, (head_dim, LANES), 0)
    lanei = lax.broadcasted_iota(jnp.int32, (head_dim, LANES), 1)
    diag = (rowi == lanei % head_dim).astype(F32)

    def other_head(a):
        return pltpu.roll(a, head_dim, 1)

    def rows(a, c):
        return a[c * SUBLANES:(c + 1) * SUBLANES]

    def bc(tile, i):
        return jnp.broadcast_to(tile[i:i + 1], (SUBLANES, LANES))

    chains = []
    for si in range(nseq):
        cum_seq = _dot_split_rhs(tri, lw_ref[si], 3)
        for p in range(npair):
            sl = slice(p * LANES, (p + 1) * LANES)
            lw = lw_ref[si, :, sl]
            cum = cum_seq[:, sl]
            p_in = jnp.exp(cum)
            p_inv = jnp.exp(-cum)
            at = kk_ref[si, :, sl] * jnp.exp(cum - lw)
            rt = r_ref[si, :, sl] * p_in
            bt = bb_ref[si, :, sl] * p_inv
            kt = k_ref[si, :, sl] * p_inv
            at_o, rt_o, bt_o, kt_o = other_head(at), other_head(rt), other_head(bt), other_head(kt)
            x4 = jnp.concatenate([rows(a, c) for c in range(nchunk) for a in (bt, bt_o, kt, kt_o)], axis=0)
            if x4.shape[0] < LANES:
                x4 = jnp.concatenate([x4, jnp.zeros((LANES - x4.shape[0], LANES), F32)], axis=0)
            xt = x4.T[:head_dim].astype(BF16)
            chains.append(dict(si=si, sl=sl, at=at, rt=rt, bt=bt, kt=kt, at_o=at_o, rt_o=rt_o, xt=xt,
                               v=v_ref[si, :, sl], p_end=p_in[steps - 1:steps],
                               s=load_state(si, p) if single else s_ref[si * npair + p],
                               ys=[]))

    for c in range(nchunk):
        for ch in chains:
            at_c, rt_c, bt_c, kt_c = rows(ch['at'], c), rows(ch['rt'], c), rows(ch['bt'], c), rows(ch['kt'], c)
            tiles = []
            for i in range(SUBLANES):
                am = jnp.where(row8 > i, at_c, 0.0)
                rm = jnp.where(row8 >= i, rt_c, 0.0)
                bi, ki = bc(bt_c, i), bc(kt_c, i)
                tiles += [am * bi, am * ki, rm * bi, rm * ki]
            coef = _dot(jnp.concatenate(tiles, axis=0).astype(BF16), block_ones)
            ch['coef'] = [rows(coef, n) for n in range(4 * SUBLANES)]
            v_c = rows(ch['v'], c)
            va = None
            yv = None
            for i in range(SUBLANES):
                vi = bc(v_c, i)
                t_ak = ch['coef'][4 * i + 1] * vi
                t_rk = ch['coef'][4 * i + 3] * vi
                va = t_ak if va is None else va + t_ak
                yv = t_rk if yv is None else yv + t_rk
            ch['va'], ch['yv'], ch['v_c'] = va, yv, v_c
            lhs = jnp.concatenate([at_c[:, :head_dim], rows(ch['at_o'], c)[:, :head_dim],
                                   rt_c[:, :head_dim], rows(ch['rt_o'], c)[:, :head_dim]], axis=0)
            ch['lhs'] = lhs.astype(BF16)
        for ch in chains:
            ch['g'] = _dot(ch['lhs'], ch['s'].astype(BF16))
        for ch in chains:
            g = ch['g']
            g_a = jnp.where(head0, rows(g, 0), rows(g, 1))
            g_r = jnp.where(head0, rows(g, 2), rows(g, 3))
            sa = g_a + ch['va']
            y = g_r + ch['yv']
            for i in range(SUBLANES):
                sai = bc(sa, i)
                if i + 1 < SUBLANES:
                    sa = sa - ch['coef'][4 * i] * sai
                y = y - ch['coef'][4 * i + 2] * sai
            ch['ys'].append(y)
            v_c = ch['v_c']
            wd = jnp.concatenate([jnp.where(head0, -sa, 0.0), jnp.where(head0, 0.0, -sa),
                                  jnp.where(head0, v_c, 0.0), jnp.where(head0, 0.0, v_c)], axis=0)
            ch['wd'] = wd.astype(BF16)
        for ch in chains:
            cols = ch['xt'][:, c * 4 * SUBLANES:(c + 1) * 4 * SUBLANES]
            ch['s'] = ch['s'] + _dot(cols, ch['wd'])

    inv_n = 1.0 / head_dim
    cat = lambda key: jnp.concatenate([ch[key] for ch in chains], axis=0)
    part = lambda a, n, m: a[n * m:(n + 1) * m]
    for ch in chains:
        ch['pd'] = diag * ch['p_end']
        ch['y'] = jnp.concatenate(ch['ys'], axis=0)
        ch['rk'] = r_ref[ch['si'], :, ch['sl']] * k_ref[ch['si'], :, ch['sl']] * rk_ref[:, ch['sl']]
    p_col = _dot_split_lhs(cat('pd'), block_ones, 3)
    y_all = cat('y')
    mu = _dot_split_lhs(y_all, block_ones, 2) * inv_n
    bonus = _dot_split_lhs(cat('rk'), block_ones, 2)
    d_all = y_all - mu
    var = _dot_split_lhs(d_all * d_all, block_ones, 2) * inv_n
    for n, ch in enumerate(chains):
        si, sl = ch['si'], ch['sl']
        s_new = ch['s'] * part(p_col, n, head_dim)
        if single:
            store_state(n, s_new)
        else:
            s_ref[n] = s_new
            ch['s'] = s_new
        yn = part(d_all, n, steps) * lax.rsqrt(part(var, n, steps) + GN_EPS) * lnw_ref[:, sl] + lnb_ref[:, sl]
        out = (yn + part(bonus, n, steps) * ch['v']) * g_ref[si, :, sl].astype(F32)
        o_ref[si, :, sl] = out.astype(o_ref.dtype)

    if not single:
        @pl.when(pl.program_id(1) == pl.num_programs(1) - 1)
        def _():
            for n, ch in enumerate(chains):
                store_state(n, ch['s'])


def _post_body(x_ref, ya_ref, yb_ref, gt_ref, hist_ref, wa_ref, wb_ref, wo_ref, g2_ref, wup_ref,
               cw_ref, cb_ref, wdn_ref, gf_ref, o_ref, tail_ref, carry_ref, *, seg, final_norm):
    dm = x_ref.shape[1]
    ua = _dot(ya_ref[...].astype(BF16), wa_ref[...])
    ub = _dot(yb_ref[...].astype(BF16), wb_ref[...])
    gates = gt_ref[...].astype(F32)
    m = (gates[:, :dm] * ua + gates[:, dm:] * ub).astype(BF16)
    x1 = x_ref[...] + _dot(m, wo_ref[...])
    h2 = _rmsnorm(x1, g2_ref[...], NORM_EPS).astype(BF16)
    up = _dot(h2, wup_ref[...])
    dff = up.shape[1] // 2
    ug = up[:, :dff]
    hist = _load_hist(hist_ref, carry_ref, seg)
    ugc = _causal_conv(ug, hist, seg, cw_ref, cb_ref)
    _store_tail(ug, tail_ref, carry_ref, seg)
    act = (_silu(ugc) * up[:, dff:]).astype(BF16)
    x2 = x1 + _dot(act, wdn_ref[...])
    if final_norm:
        x2 = _rmsnorm(x2, gf_ref[...], NORM_EPS)
    o_ref[...] = x2


def _const_spec(shape):
    nd = len(shape)
    return pl.BlockSpec(shape, lambda *_: (0,) * nd, pipeline_mode=pl.Buffered(1))


def _params(sem):
    return pltpu.CompilerParams(dimension_semantics=sem, vmem_limit_bytes=VMEM_LIMIT)


def _token_tiling(nseq, seqlen):
    if seqlen == SUBLANES:
        per = ROW_TILE // SUBLANES
        assert nseq % per == 0
        return per, SUBLANES, (nseq // per, 1)
    assert seqlen % ROW_TILE == 0
    return 1, ROW_TILE, (nseq, seqlen // ROW_TILE)


def _row_spec(cols, lt):
    return pl.BlockSpec((ROW_TILE, cols), lambda i, l: (i * lt + l, 0))


def _hist_spec(per, cols):
    return pl.BlockSpec((per, SUBLANES, cols), lambda i, l: (i, 0, 0))


def _ssm_in(x, hist, p, nseq, seqlen, act_dtype):
    per, seg, grid = _token_tiling(nseq, seqlen)
    t, dm = x.shape
    dz = p['wz'].shape[1]
    dc = p['wx'].shape[1]
    dd = p['wdt'].shape[1]
    lt = grid[1]
    return pl.pallas_call(
        functools.partial(_ssm_in_body, seg=seg),
        grid=grid,
        in_specs=[_row_spec(dm, lt), _hist_spec(per, dc), _const_spec((1, dm)),
                  _const_spec(p['wz'].shape), _const_spec(p['wx'].shape), _const_spec(p['wdt'].shape),
                  _const_spec(p['conv_w'].shape), _const_spec((1, dc)), _const_spec((1, dd))],
        out_specs=[_row_spec(dz, lt), _row_spec(dc, lt), _row_spec(dd, lt), _hist_spec(per, dc)],
        out_shape=[jax.ShapeDtypeStruct((t, dz), act_dtype), jax.ShapeDtypeStruct((t, dc), F32),
                   jax.ShapeDtypeStruct((t, dd), F32), jax.ShapeDtypeStruct((nseq, SUBLANES, dc), F32)],
        scratch_shapes=[pltpu.VMEM((SUBLANES, dc), F32)],
        compiler_params=_params(("arbitrary", "arbitrary")),
        name="ssm_in",
    )(x, hist, p['norm1_g'], p['wz'], p['wx'], p['wdt'], p['conv_w'], p['conv_b'], p['dt_bias'])


def _rwkv_in(x, hist, p, nseq, seqlen, act_dtype):
    per, seg, grid = _token_tiling(nseq, seqlen)
    t, dm = x.shape
    dim = p['w0'].shape[1]
    drw = p['wrw'].shape[1]
    dg = p['wg'].shape[1]
    lt = grid[1]
    f32_out = jax.ShapeDtypeStruct((t, dim), F32)
    return pl.pallas_call(
        functools.partial(_rwkv_in_body, seg=seg, dim=dim),
        grid=grid,
        in_specs=[_row_spec(dm, lt), _hist_spec(per, drw), _const_spec((1, dm)),
                  _const_spec(p['wrw'].shape), _const_spec(p['wg'].shape), _const_spec((1, drw)),
                  _const_spec(p['wlora'].shape), _const_spec((1, dim)), _const_spec((1, dim)),
                  _const_spec((1, dim)), _const_spec((1, dim)),
                  _const_spec(p['head_sum'].shape), _const_spec(p['head_expand'].shape)],
        out_specs=[_row_spec(dim, lt)] * 7 + [_row_spec(dg, lt), _hist_spec(per, drw)],
        out_shape=[f32_out] * 6 + [jax.ShapeDtypeStruct((t, dim), act_dtype),
                                   jax.ShapeDtypeStruct((t, dg), BF16),
                                   jax.ShapeDtypeStruct((nseq, SUBLANES, drw), F32)],
        scratch_shapes=[pltpu.VMEM((SUBLANES, drw), F32)],
        compiler_params=_params(("arbitrary", "arbitrary")),
        name="rwkv_in",
    )(x, hist, p['norm1_g'], p['wrw'], p['wg'], p['mu'], p['wlora'], p['w0'], p['a0'],
      p['k_k'], p['k_a'], p['head_sum'], p['head_expand'])


def _ssd(xc, dt, z, state, p, nseq, seqlen, act_dtype):
    t = xc.shape[0]
    _, n_heads, p_dim, state_n = state.shape
    groups = dt.shape[1] // LANES
    heads = n_heads // groups
    width = heads * p_dim
    d_inner = groups * width
    assert state_n == LANES and width % LANES == 0 and p_dim * 2 == LANES
    if seqlen % SSD_CHUNK == 0:
        seqs, gps, rows, nc = 1, groups, SSD_CHUNK, seqlen // SSD_CHUNK
    elif seqlen == SUBLANES and nseq % (SSD_CHUNK // SUBLANES) == 0:
        seqs, gps, rows, nc = SSD_CHUNK // SUBLANES, 1, SSD_CHUNK, 1
    else:
        assert seqlen == SUBLANES
        seqs, gps, rows, nc = 1, groups, SUBLANES, 1
    gn = gps * state_n
    b_blk = d_inner // gn
    assert d_inner % gn == 0
    row_map = lambda b, g, c: (b * nc + c, g)
    st_spec = pl.BlockSpec((seqs, gps * heads, p_dim, state_n), lambda b, g, c: (b, g, 0, 0))
    return pl.pallas_call(
        functools.partial(_ssd_body, rows=rows, heads=heads, groups=gps, seqs=seqs),
        grid=(nseq // seqs, groups // gps, nc),
        in_specs=[pl.BlockSpec((rows, gps * width), row_map),
                  pl.BlockSpec((rows, gn), lambda b, g, c: (b * nc + c, b_blk + g)),
                  pl.BlockSpec((rows, gn), lambda b, g, c: (b * nc + c, b_blk + groups // gps + g)),
                  pl.BlockSpec((rows, gps * LANES), row_map),
                  pl.BlockSpec((rows, gps * width), row_map),
                  st_spec,
                  pl.BlockSpec((1, gps * LANES), lambda b, g, c: (0, g)),
                  pl.BlockSpec((1, gps * width), lambda b, g, c: (0, g)),
                  pl.BlockSpec((1, gps * width), lambda b, g, c: (0, g)),
                  _const_spec(p['tri'].shape), _const_spec(p['ssd_expand'].shape),
                  _const_spec(p['ssd_expand2'].shape)],
        out_specs=[pl.BlockSpec((rows, gps * width), row_map), st_spec],
        out_shape=[jax.ShapeDtypeStruct((t, d_inner), act_dtype),
                   jax.ShapeDtypeStruct(state.shape, F32)],
        scratch_shapes=[pltpu.VMEM((gps, state_n, width), F32)],
        compiler_params=_params(("arbitrary", "arbitrary", "arbitrary")),
        name="ssd",
    )(xc, xc, xc, dt, z, state, p['a_log'], p['d_exp'], p['ssm_norm_g'],
      p['tri'], p['ssd_expand'], p['ssd_expand2'])


def _wkv(r, lw, k, v, kk, bb, g, state, p, nseq, seqlen, act_dtype):
    dim = r.shape[1]
    _, n_heads, head_dim, _ = state.shape
    npair = dim // LANES
    assert n_heads == 2 * npair and 2 * head_dim == LANES
    if seqlen == SUBLANES:
        per, steps = WKV_SAMPLE_SEQS, SUBLANES
    else:
        per, steps = (WKV_PROMPT_SEQS if nseq % WKV_PROMPT_SEQS == 0 else 1), WKV_TBLOCK
    assert nseq % per == 0 and seqlen % steps == 0
    grid = (nseq // per, seqlen // steps)
    seq_spec = pl.BlockSpec((per, steps, dim), lambda i, tb: (i, tb, 0))
    st_spec = pl.BlockSpec((per, n_heads, head_dim, head_dim), lambda i, tb: (i, 0, 0, 0))
    as3 = lambda a: a.reshape(nseq, seqlen, dim)
    out, s_out = pl.pallas_call(
        functools.partial(_wkv_body, head_dim=head_dim, single=(seqlen == steps)),
        grid=grid,
        in_specs=[seq_spec] * 7 + [st_spec, _const_spec((1, dim)), _const_spec((1, dim)),
                                   _const_spec((1, dim)), _const_spec((LANES, LANES)),
                                   _const_spec((steps, steps))],
        out_specs=[seq_spec, st_spec],
        out_shape=[jax.ShapeDtypeStruct((nseq, seqlen, dim), act_dtype),
                   jax.ShapeDtypeStruct(state.shape, F32)],
        scratch_shapes=[pltpu.VMEM((per * npair, head_dim, LANES), F32)],
        compiler_params=_params(("arbitrary", "arbitrary")),
        name="wkv",
    )(as3(r), as3(lw), as3(k), as3(v), as3(kk), as3(bb), as3(g), state,
      p['r_k'], p['ln_w'], p['ln_b'], p['block_ones'], p['tri'][:steps, :steps])
    return out.reshape(nseq * seqlen, dim), s_out


def _post(x, ya, yb, gates, hist, p, nseq, seqlen, final_norm):
    per, seg, grid = _token_tiling(nseq, seqlen)
    t, dm = x.shape
    dff = p['wdn'].shape[0]
    lt = grid[1]
    return pl.pallas_call(
        functools.partial(_post_body, seg=seg, final_norm=final_norm),
        grid=grid,
        in_specs=[_row_spec(dm, lt), _row_spec(ya.shape[1], lt), _row_spec(yb.shape[1], lt),
                  _row_spec(gates.shape[1], lt), _hist_spec(per, dff),
                  _const_spec(p['wa'].shape), _const_spec(p['wb'].shape), _const_spec(p['wo'].shape),
                  _const_spec((1, dm)), _const_spec(p['wup'].shape), _const_spec(p['ffn_conv_w'].shape),
                  _const_spec((1, dff)), _const_spec(p['wdn'].shape), _const_spec((1, dm))],
        out_specs=[_row_spec(dm, lt), _hist_spec(per, dff)],
        out_shape=[jax.ShapeDtypeStruct((t, dm), F32), jax.ShapeDtypeStruct((nseq, SUBLANES, dff), F32)],
        scratch_shapes=[pltpu.VMEM((SUBLANES, dff), F32)],
        compiler_params=_params(("arbitrary", "arbitrary")),
        name="post",
    )(x, ya, yb, gates, hist, p['wa'], p['wb'], p['wo'], p['norm2_g'], p['wup'], p['ffn_conv_w'],
      p['ffn_conv_b'], p['wdn'], p['final_g'])


def _prep_layer(w, dims):
    d_inner, conv_dim, n_heads, shift_dim, groups, rwkv_dim, rwkv_heads = dims
    row = lambda a: a.reshape(1, -1).astype(F32)
    w_in = w['w_in']
    o1 = d_inner
    o2 = o1 + conv_dim
    o3 = o2 + n_heads
    o4 = o3 + shift_dim
    hpg = n_heads // groups

    def per_group(a):
        lead = a.shape[:-1]
        a = a.reshape(lead + (groups, hpg))
        a = jnp.pad(a, [(0, 0)] * len(lead) + [(0, 0), (0, LANES - hpg)])
        return a.reshape(lead + (groups * LANES,))

    p_dim = d_inner // n_heads
    head_dim = rwkv_dim // rwkv_heads
    lora_w = w['rwkv_w_up'].shape[0]
    lora_a = w['rwkv_a_up'].shape[0]
    lora_g = w['rwkv_g_up'].shape[0]
    assert lora_w == lora_a and lora_g == 2 * lora_w
    wlora = jnp.zeros((lora_w + lora_a + lora_g, 3 * rwkv_dim), F32)
    wlora = wlora.at[:lora_w, :rwkv_dim].set(w['rwkv_w_up'])
    wlora = wlora.at[lora_w:lora_w + lora_a, rwkv_dim:2 * rwkv_dim].set(w['rwkv_a_up'])
    wlora = wlora.at[lora_w + lora_a:, 2 * rwkv_dim:].set(w['rwkv_g_up'])

    ch = jnp.arange(rwkv_dim) // head_dim
    head_sum = (ch[:, None] == jnp.arange(LANES)[None, :]).astype(BF16)
    li = jnp.arange(LANES)
    width = hpg * p_dim
    ssd_expand = (li[:, None] == (jnp.arange(width) // p_dim)[None, :]).astype(BF16)
    ssd_expand2 = (li[:, None] == (jnp.arange(hpg * SSD_CHUNK) // SSD_CHUNK)[None, :]).astype(BF16)
    ci = jnp.arange(SSD_CHUNK)
    tri = (ci[:, None] >= ci[None, :]).astype(BF16)
    block_ones = ((li[:, None] // head_dim) == (li[None, :] // head_dim)).astype(BF16)

    return {
        'norm1_g': row(w['norm1_g']),
        'wz': w_in[:, :o1].astype(BF16),
        'wx': w_in[:, o1:o2].astype(BF16),
        'wdt': per_group(w_in[:, o2:o3]).astype(BF16),
        'wrw': w_in[:, o3:o4].astype(BF16),
        'wg': w_in[:, o4:].astype(BF16),
        'conv_w': w['ssm_conv_w'].astype(F32),
        'conv_b': row(w['ssm_conv_b']),
        'dt_bias': row(per_group(w['ssm_dt_bias'])),
        'a_log': row(per_group(w['ssm_a_log'])),
        'd_exp': row(jnp.repeat(w['ssm_d'], p_dim)),
        'ssm_norm_g': row(w['ssm_norm_g']),
        'wa': w['w_branch_a'].astype(BF16),
        'mu': row(w['rwkv_mu']),
        'wlora': wlora.astype(BF16),
        'w0': row(w['rwkv_w0']),
        'a0': row(w['rwkv_a0']),
        'k_k': row(w['rwkv_k_k']),
        'k_a': row(w['rwkv_k_a']),
        'r_k': row(w['rwkv_r_k']),
        'ln_w': row(w['rwkv_ln_w']),
        'ln_b': row(w['rwkv_ln_b']),
        'wb': w['w_branch_b'].astype(BF16),
        'wo': w['w_out'].astype(BF16),
        'norm2_g': row(w['norm2_g']),
        'wup': w['ffn_w_up'].astype(BF16),
        'ffn_conv_w': w['ffn_conv_w'].astype(F32),
        'ffn_conv_b': row(w['ffn_conv_b']),
        'wdn': w['ffn_w_down'].astype(BF16),
        'head_sum': head_sum,
        'head_expand': head_sum.T,
        'ssd_expand': ssd_expand,
        'ssd_expand2': ssd_expand2,
        'tri': tri,
        'block_ones': block_ones,
    }


def _hist8(state_rows):
    k = state_rows.shape[1]
    return jnp.pad(state_rows.astype(F32), ((0, 0), (SUBLANES - k, 0), (0, 0)))


def _layer(x, conv_buf, ssm_state, shift_buf, wkv_state, ffn_buf, p, final_norm):
    nseq, seqlen, dm = x.shape
    act_dtype = BF16 if seqlen % 16 == 0 else F32
    xf = x.reshape(nseq * seqlen, dm)
    z, xc, dt, conv_tail = _ssm_in(xf, _hist8(conv_buf), p, nseq, seqlen, act_dtype)
    r, lw, k, v, kk, bb, g, gates, shift_tail = _rwkv_in(xf, _hist8(shift_buf[:, None]), p, nseq, seqlen,
                                                         act_dtype)
    ya, new_ssm = _ssd(xc, dt, z, ssm_state.astype(F32), p, nseq, seqlen, act_dtype)
    yb, new_wkv = _wkv(r, lw, k, v, kk, bb, g, wkv_state.astype(F32), p, nseq, seqlen, act_dtype)
    out, ffn_tail = _post(xf, ya, yb, gates, _hist8(ffn_buf), p, nseq, seqlen, final_norm)
    new_conv = conv_tail[:, SUBLANES - conv_buf.shape[1]:]
    new_shift = shift_tail[:, SUBLANES - 1]
    new_ffn = ffn_tail[:, SUBLANES - ffn_buf.shape[1]:]
    return out.reshape(nseq, seqlen, dm), (new_conv, new_ssm, new_shift, new_wkv, new_ffn)


_LAYER_WEIGHTS = ('norm1_g', 'w_in', 'ssm_conv_w', 'ssm_conv_b', 'ssm_dt_bias', 'ssm_a_log', 'ssm_d',
                  'ssm_norm_g', 'w_branch_a', 'rwkv_mu', 'rwkv_w0', 'rwkv_w_up', 'rwkv_a0', 'rwkv_a_up',
                  'rwkv_g_up', 'rwkv_k_k', 'rwkv_k_a', 'rwkv_r_k', 'rwkv_ln_w', 'rwkv_ln_b', 'w_branch_b',
                  'w_out', 'norm2_g', 'ffn_w_up', 'ffn_conv_w', 'ffn_conv_b', 'ffn_w_down')


def kernel(x_prompt, x_sample, state_ssm_conv, state_ssm, state_rwkv_shift, state_rwkv, state_ffn_conv,
           norm1_g, w_in, ssm_conv_w, ssm_conv_b, ssm_dt_bias, ssm_a_log, ssm_d, ssm_norm_g, w_branch_a,
           rwkv_mu, rwkv_w0, rwkv_w_up, rwkv_a0, rwkv_a_up, rwkv_g_up, rwkv_k_k, rwkv_k_a, rwkv_r_k,
           rwkv_ln_w, rwkv_ln_b, w_branch_b, w_out, norm2_g, ffn_w_up, ffn_conv_w, ffn_conv_b, ffn_w_down,
           final_g):
    stacked = dict(zip(_LAYER_WEIGHTS, (
        norm1_g, w_in, ssm_conv_w, ssm_conv_b, ssm_dt_bias, ssm_a_log, ssm_d, ssm_norm_g, w_branch_a,
        rwkv_mu, rwkv_w0, rwkv_w_up, rwkv_a0, rwkv_a_up, rwkv_g_up, rwkv_k_k, rwkv_k_a, rwkv_r_k,
        rwkv_ln_w, rwkv_ln_b, w_branch_b, w_out, norm2_g, ffn_w_up, ffn_conv_w, ffn_conv_b, ffn_w_down)))
    depth = w_in.shape[0]
    _, _, n_heads, _, _ = state_ssm.shape
    conv_dim = state_ssm_conv.shape[-1]
    d_inner = w_branch_a.shape[1]
    shift_dim = state_rwkv_shift.shape[-1]
    rwkv_heads = state_rwkv.shape[2]
    rwkv_dim = w_branch_b.shape[1]
    groups = (conv_dim - d_inner) // (2 * state_ssm.shape[-1])
    dims = (d_inner, conv_dim, n_heads, shift_dim, groups, rwkv_dim, rwkv_heads)

    xp, xs = x_prompt, x_sample
    bp = xp.shape[0]
    new_p = ([], [], [], [], [])
    new_s = ([], [], [], [], [])
    for i in range(depth):
        p = _prep_layer({name: a[i] for name, a in stacked.items()}, dims)
        p['final_g'] = final_g.reshape(1, -1).astype(F32)
        last = i == depth - 1
        xp, sp = _layer(
            xp,
            jnp.zeros((bp,) + state_ssm_conv.shape[2:], F32),
            jnp.zeros((bp,) + state_ssm.shape[2:], F32),
            jnp.zeros((bp,) + state_rwkv_shift.shape[2:], F32),
            jnp.zeros((bp,) + state_rwkv.shape[2:], F32),
            jnp.zeros((bp,) + state_ffn_conv.shape[2:], F32),
            p, last)
        xs, ss = _layer(xs, state_ssm_conv[i], state_ssm[i], state_rwkv_shift[i], state_rwkv[i],
                        state_ffn_conv[i], p, last)
        for j in range(5):
            new_p[j].append(sp[j])
            new_s[j].append(ss[j])
    return (xp, xs,
            jnp.stack(new_p[0]), jnp.stack(new_p[1]), jnp.stack(new_p[2]), jnp.stack(new_p[3]),
            jnp.stack(new_p[4]),
            jnp.stack(new_s[0]), jnp.stack(new_s[1]), jnp.stack(new_s[2]), jnp.stack(new_s[3]),
            jnp.stack(new_s[4]))
```

```python
import functools

import jax
import jax.numpy as jnp
from jax import lax
from jax.experimental import pallas as pl
from jax.experimental.pallas import tpu as pltpu

F32 = jnp.float32
BF16 = jnp.bfloat16

NORM_EPS = 1e-5
GN_EPS = 64e-5

LANES = 128
SUBLANES = 8
ROW_TILE = 256
COL_STRIP = 512
SSD_CHUNK = 128
SSD_SUBCHUNKS = 2
WKV_TBLOCK = 64
WKV_PROMPT_SEQS = 4
WKV_SAMPLE_SEQS = 4
VMEM_LIMIT = 56 * 1024 * 1024


def _dot(a, b):
    return jnp.dot(a, b, preferred_element_type=F32)


def _split(x, n):
    parts = []
    rem = x
    for i in range(n):
        p = rem.astype(BF16)
        parts.append(p)
        if i + 1 < n:
            rem = rem - p.astype(F32)
    return parts


def _dot_split_lhs(x, m, n):
    acc = None
    for p in _split(x, n):
        d = _dot(p, m)
        acc = d if acc is None else acc + d
    return acc


def _dot_split_rhs(m, x, n):
    acc = None
    for p in _split(x, n):
        d = _dot(m, p)
        acc = d if acc is None else acc + d
    return acc


def _sigmoid(x):
    return 0.5 * jnp.tanh(0.5 * x) + 0.5


def _silu(x):
    hx = 0.5 * x
    return hx * jnp.tanh(hx) + hx


def _softplus(x):
    return jnp.maximum(x, 0.0) + jnp.log1p(jnp.exp(-jnp.abs(x)))


def _rmsnorm(x, g, eps):
    ms = jnp.mean(x * x, axis=-1, keepdims=True)
    return x * lax.rsqrt(ms + eps) * g


def _shift_rows(u, hist, seg, j):
    rows = u.shape[0]
    ru = pltpu.roll(u, j, 0)
    if seg == SUBLANES:
        rh = pltpu.roll(hist, (rows - SUBLANES + j) % rows, 0)
        pos = lax.broadcasted_iota(jnp.int32, u.shape, 0) % SUBLANES
        return jnp.where(pos < j, rh, ru)
    assert seg == rows
    rh = pltpu.roll(hist, j, 0)
    pos = lax.broadcasted_iota(jnp.int32, rh.shape, 0)
    top = jnp.where(pos < j, rh, ru[:SUBLANES])
    return jnp.concatenate([top, ru[SUBLANES:]], axis=0)


def _causal_conv(u, hist, seg, w_ref, b_ref):
    taps = w_ref.shape[0]
    acc = u * w_ref[taps - 1:taps, :] + b_ref[...]
    for j in range(1, taps):
        acc = acc + _shift_rows(u, hist, seg, j) * w_ref[taps - 1 - j:taps - j, :]
    return acc


def _load_hist(hist_ref, carry_ref, seg):
    if seg == SUBLANES:
        nseq, _, c = hist_ref.shape
        return hist_ref[...].reshape(nseq * SUBLANES, c)

    @pl.when(pl.program_id(1) == 0)
    def _():
        carry_ref[...] = hist_ref[0]

    return carry_ref[...]


def _store_tail(u, tail_ref, carry_ref, seg):
    if seg == SUBLANES:
        tail_ref[...] = u.reshape(tail_ref.shape)
    else:
        last = u[u.shape[0] - SUBLANES:]
        carry_ref[...] = last
        tail_ref[0] = last


def _ssm_in_body(x_ref, hist_ref, g1_ref, wz_ref, wx_ref, wdt_ref, cw_ref, cb_ref, dtb_ref,
                 z_ref, xc_ref, dt_ref, tail_ref, carry_ref, *, seg):
    h = _rmsnorm(x_ref[...], g1_ref[...], NORM_EPS).astype(BF16)
    hist = _load_hist(hist_ref, carry_ref, seg)
    taps = cw_ref.shape[0]
    for lo in range(0, wx_ref.shape[1], COL_STRIP):
        sl = slice(lo, lo + COL_STRIP)
        u = _dot(h, wx_ref[:, sl])
        hs = hist[:, sl]
        acc = u * cw_ref[taps - 1:taps, sl] + cb_ref[:, sl]
        for j in range(1, taps):
            acc = acc + _shift_rows(u, hs, seg, j) * cw_ref[taps - 1 - j:taps - j, sl]
        xc_ref[:, sl] = _silu(acc)
        if seg == SUBLANES:
            tail_ref[:, :, sl] = u.reshape(tail_ref.shape[0], SUBLANES, COL_STRIP)
        else:
            last = u[u.shape[0] - SUBLANES:]
            carry_ref[:, sl] = last
            tail_ref[0, :, sl] = last
    for lo in range(0, wz_ref.shape[1], COL_STRIP):
        sl = slice(lo, lo + COL_STRIP)
        z_ref[:, sl] = _dot(h, wz_ref[:, sl]).astype(z_ref.dtype)
    dt_ref[...] = _softplus(_dot(h, wdt_ref[...]) + dtb_ref[...])


def _rwkv_in_body(x_ref, hist_ref, g1_ref, wrw_ref, wg_ref, mu_ref, wlora_ref, w0_ref, a0_ref,
                  kk_ref, ka_ref, seg_ref, exp_ref,
                  r_out, lw_out, k_out, v_out, kkn_out, bb_out, g_out, gates_out, tail_ref,
                  carry_ref, *, seg, dim):
    h = _rmsnorm(x_ref[...], g1_ref[...], NORM_EPS).astype(BF16)
    hist = _load_hist(hist_ref, carry_ref, seg)
    drw = wrw_ref.shape[1]

    def mixed(lo, hi):
        sl = slice(lo, hi)
        u = _dot(h, wrw_ref[:, sl])
        prev = _shift_rows(u, hist[:, sl], seg, 1)
        if seg == SUBLANES:
            tail_ref[:, :, sl] = u.reshape(tail_ref.shape[0], SUBLANES, hi - lo)
        else:
            last = u[u.shape[0] - SUBLANES:]
            carry_ref[:, sl] = last
            tail_ref[0, :, sl] = last
        return u + (prev - u) * mu_ref[:, sl]

    low = mixed(3 * dim, drw)
    lane = lax.broadcasted_iota(jnp.int32, low.shape, 1)
    lo_w = low.shape[1] // 4
    act = jnp.where(lane < lo_w, jnp.tanh(low), jnp.where(lane < 2 * lo_w, low, _sigmoid(low))).astype(BF16)
    r_out[...] = mixed(0, dim)
    wlog = -_softplus(-(w0_ref[...] + _dot(act, wlora_ref[:, 0:dim]))) - 0.5
    lw_out[...] = -jnp.exp(wlog)
    v_out[...] = mixed(2 * dim, 3 * dim)
    a = _sigmoid(a0_ref[...] + _dot(act, wlora_ref[:, dim:2 * dim]))
    k = mixed(dim, 2 * dim)
    kkr = k * kk_ref[...]
    ss = _dot_split_lhs(kkr * kkr, seg_ref[...], 2)
    inv = 1.0 / jnp.maximum(jnp.sqrt(ss), 1e-12)
    kkn = kkr * _dot_split_lhs(inv, exp_ref[...], 2)
    k_out[...] = k * (1.0 + (a - 1.0) * ka_ref[...])
    kkn_out[...] = kkn
    bb_out[...] = kkn * a
    g_out[...] = _dot(act, wlora_ref[:, 2 * dim:3 * dim]).astype(g_out.dtype)
    for lo in range(0, wg_ref.shape[1], COL_STRIP):
        sl = slice(lo, lo + COL_STRIP)
        gates_out[:, sl] = _sigmoid(_dot(h, wg_ref[:, sl])).astype(gates_out.dtype)


def _ssd_body(xm_ref, b_ref, c_ref, dt_ref, z_ref, st_ref, alog_ref, dexp_ref, ng_ref,
              tri_ref, e_ref, e2_ref, y_ref, so_ref, ht_ref, *, rows, heads, groups, seqs, subs):
    step = pl.program_id(2)
    n_steps = pl.num_programs(2)
    R = SSD_CHUNK
    width = xm_ref.shape[1] // groups
    state_n = b_ref.shape[1] // groups
    p_dim = width // heads
    gs = range(groups)
    units = [(sc, g) for sc in range(subs) for g in gs]
    us = range(len(units))
    steps = R // seqs

    def blk(ref, u, n):
        sc, g = units[u]
        v = ref[sc * R:(sc + 1) * R, g * n:(g + 1) * n] if subs > 1 else ref[:, g * n:(g + 1) * n]
        if v.shape[0] == R:
            return v
        return jnp.concatenate([v, jnp.zeros((R - v.shape[0], v.shape[1]), v.dtype)], axis=0)

    def gcols(ref, u, n):
        g = units[u][1]
        return ref[:, g * n:(g + 1) * n]

    row = lax.broadcasted_iota(jnp.int32, (R, R), 0)
    col = lax.broadcasted_iota(jnp.int32, (R, R), 1)
    lane = lax.broadcasted_iota(jnp.int32, (R, LANES), 1)
    expand = e_ref[...]
    expand2 = e2_ref[...]
    if seqs == 1:
        causal = row >= col
        tri = tri_ref[...]

        @pl.when(step == 0)
        def _():
            for g in gs:
                ht_ref[g] = st_ref[0, g * heads:(g + 1) * heads].reshape(width, state_n).T
    else:
        same = (row // steps) == (col // steps)
        causal = same & (row >= col)
        tri = causal.astype(BF16)

    xm = [blk(xm_ref, u, width) for u in us]
    bm = [blk(b_ref, u, state_n) for u in us]
    cm = [blk(c_ref, u, state_n) for u in us]
    cmb = [cm[u].astype(BF16) for u in us]
    dt = [blk(dt_ref, u, LANES) for u in us]
    a = [dt[u] * (-jnp.exp(gcols(alog_ref, u, LANES))) for u in us]
    cs = [_dot_split_rhs(tri, a[u], 3) for u in us]
    dt_e = [_dot_split_lhs(dt[u], expand, 2) for u in us]
    cs_e = [_dot_split_lhs(cs[u], expand, 3) for u in us]
    if seqs > 1:
        cs_e2 = [_dot_split_lhs(cs[u], expand2, 3) for u in us]
    bt = [bm[u].T for u in us]
    btb = [bt[u].astype(BF16) for u in us]
    cs_t = [cs[u].T for u in us]
    if seqs == 1:
        end_e = [cs_e[u][R - 1:R, :] for u in us]
    else:
        tot = [_dot_split_rhs(same.astype(BF16), a[u], 3) for u in us]
        end_e = [_dot_split_lhs(tot[u], expand, 3) for u in us]
    xdt = [xm[u] * dt_e[u] for u in us]
    xs = [(xdt[u] * jnp.exp(end_e[u] - cs_e[u])).astype(BF16) for u in us]
    cb = [_dot(cmb[u], btb[u]) for u in us]
    if seqs == 1:
        st_new = [_dot(btb[u], xs[u]) for u in us]
        ht = [ht_ref[g] for g in gs]
        y_off = []
        for u, (sc, g) in enumerate(units):
            y_off.append(_dot(cmb[u], ht[g].astype(BF16)))
            ht[g] = ht[g] * jnp.exp(end_e[u]) + st_new[u]
        for g in gs:
            ht_ref[g] = ht[g]
    else:
        y_off = []
        for u, (sc, g) in enumerate(units):
            tiles = []
            for s in range(seqs):
                r0 = s * steps
                ht0 = st_ref[s, g * heads:(g + 1) * heads].reshape(width, state_n).T
                tiles.append(_dot(cm[u][r0:r0 + steps].astype(BF16), ht0.astype(BF16)))
                bts = jnp.where((col // steps) == s, bt[u], 0.0).astype(BF16)
                ht1 = ht0 * jnp.exp(end_e[u][r0:r0 + 1]) + _dot(bts, xs[u])
                so_ref[s, g * heads:(g + 1) * heads] = ht1.T.reshape((heads,) + so_ref.shape[2:])
            y_off.append(jnp.concatenate(tiles, axis=0))
    ys = [[] for _ in us]
    for j in range(heads // 2):
        lhs, rhs = [], []
        for u in us:
            ms = []
            for hh in (2 * j, 2 * j + 1):
                if seqs == 1:
                    cs_col = jnp.broadcast_to(cs[u][:, hh:hh + 1], (R, R))
                else:
                    cs_col = cs_e2[u][:, hh * R:(hh + 1) * R]
                seg_ = cs_col - cs_t[u][hh:hh + 1, :]
                ms.append(jnp.where(causal, cb[u] * jnp.exp(jnp.where(causal, seg_, 0.0)), 0.0).astype(BF16))
            lhs.append(jnp.concatenate(ms, axis=1))
            xp = xdt[u][:, j * LANES:(j + 1) * LANES]
            rhs.append(jnp.concatenate([jnp.where(lane < p_dim, xp, 0.0), jnp.where(lane >= p_dim, xp, 0.0)],
                                       axis=0).astype(BF16))
        for u in us:
            ys[u].append(_dot(lhs[u], rhs[u]))
    for u, (sc, g) in enumerate(units):
        y = jnp.concatenate(ys[u], axis=1) + y_off[u] * jnp.exp(cs_e[u])
        y = y + gcols(dexp_ref, u, width) * xm[u]
        yz = y * _silu(blk(z_ref, u, width).astype(F32))
        yn = yz * lax.rsqrt(jnp.mean(yz * yz, axis=-1, keepdims=True) + NORM_EPS) * gcols(ng_ref, u, width)
        if subs > 1:
            y_ref[sc * R:(sc + 1) * R, g * width:(g + 1) * width] = yn.astype(y_ref.dtype)
        else:
            y_ref[:, g * width:(g + 1) * width] = yn[:rows].astype(y_ref.dtype)

    if seqs == 1:
        @pl.when(step == n_steps - 1)
        def _():
            for g in gs:
                so_ref[0, g * heads:(g + 1) * heads] = ht_ref[g].T.reshape((heads,) + so_ref.shape[2:])


def _wkv_body(r_ref, lw_ref, k_ref, v_ref, kk_ref, bb_ref, g_ref, s0_ref, rk_ref, lnw_ref, lnb_ref,
              bo_ref, tri_ref, o_ref, so_ref, s_ref, *, head_dim, single):
    nseq, steps, dim = r_ref.shape
    npair = dim // LANES
    nchunk = steps // SUBLANES
    block_ones = bo_ref[...]
    tri = tri_ref[...]

    def load_state(si, p):
        x = s0_ref[si, 2 * p:2 * p + 2].reshape(2 * head_dim, head_dim)
        xp = jnp.concatenate([x, jnp.zeros((2 * head_dim, LANES - head_dim), F32)], axis=1)
        return xp.T[:head_dim]

    def store_state(n, s):
        sp = jnp.concatenate([s, jnp.zeros((LANES - head_dim, LANES), F32)], axis=0)
        back = sp.T[:, :head_dim]
        so_ref[n // npair, 2 * (n % npair):2 * (n % npair) + 2] = back.reshape(2, head_dim, head_dim)

    if not single:
        @pl.when(pl.program_id(1) == 0)
        def _():
            for si in range(nseq):
                for p in range(npair):
                    s_ref[si * npair + p] = load_state(si, p)

    lane8 = lax.broadcasted_iota(jnp.int32, (SUBLANES, LANES), 1)
    row8 = lax.broadcasted_iota(jnp.int32, (SUBLANES, LANES), 0)
    head0 = lane8 < head_dim
    rowi = lax.broadcasted_iota(jnp.int32, (head_dim, LANES), 0)
    lanei = lax.broadcasted_iota(jnp.int32, (head_dim, LANES), 1)
    diag = (rowi == lanei % head_dim).astype(F32)

    def other_head(a):
        return pltpu.roll(a, head_dim, 1)

    def rows(a, c):
        return a[c * SUBLANES:(c + 1) * SUBLANES]

    def bc(tile, i):
        return jnp.broadcast_to(tile[i:i + 1], (SUBLANES, LANES))

    chains = []
    for si in range(nseq):
        cum_seq = _dot_split_rhs(tri, lw_ref[si], 3)
        for p in range(npair):
            sl = slice(p * LANES, (p + 1) * LANES)
            lw = lw_ref[si, :, sl]
            cum = cum_seq[:, sl]
            p_in = jnp.exp(cum)
            p_inv = jnp.exp(-cum)
            at = kk_ref[si, :, sl] * jnp.exp(cum - lw)
            rt = r_ref[si, :, sl] * p_in
            bt = bb_ref[si, :, sl] * p_inv
            kt = k_ref[si, :, sl] * p_inv
            at_o, rt_o, bt_o, kt_o = other_head(at), other_head(rt), other_head(bt), other_head(kt)
            x4 = jnp.concatenate([rows(a, c) for c in range(nchunk) for a in (bt, bt_o, kt, kt_o)], axis=0)
            if x4.shape[0] < LANES:
                x4 = jnp.concatenate([x4, jnp.zeros((LANES - x4.shape[0], LANES), F32)], axis=0)
            xt = x4.T[:head_dim].astype(BF16)
            chains.append(dict(si=si, sl=sl, at=at, rt=rt, bt=bt, kt=kt, at_o=at_o, rt_o=rt_o, xt=xt,
                               v=v_ref[si, :, sl], p_end=p_in[steps - 1:steps],
                               s=load_state(si, p) if single else s_ref[si * npair + p],
                               ys=[]))

    for c in range(nchunk):
        for ch in chains:
            at_c, rt_c, bt_c, kt_c = rows(ch['at'], c), rows(ch['rt'], c), rows(ch['bt'], c), rows(ch['kt'], c)
            tiles = []
            for i in range(SUBLANES):
                am = jnp.where(row8 > i, at_c, 0.0)
                rm = jnp.where(row8 >= i, rt_c, 0.0)
                bi, ki = bc(bt_c, i), bc(kt_c, i)
                tiles += [am * bi, am * ki, rm * bi, rm * ki]
            coef = _dot(jnp.concatenate(tiles, axis=0).astype(BF16), block_ones)
            ch['coef'] = [rows(coef, n) for n in range(4 * SUBLANES)]
            v_c = rows(ch['v'], c)
            va = None
            yv = None
            for i in range(SUBLANES):
                vi = bc(v_c, i)
                t_ak = ch['coef'][4 * i + 1] * vi
                t_rk = ch['coef'][4 * i + 3] * vi
                va = t_ak if va is None else va + t_ak
                yv = t_rk if yv is None else yv + t_rk
            ch['va'], ch['yv'], ch['v_c'] = va, yv, v_c
            lhs = jnp.concatenate([at_c[:, :head_dim], rows(ch['at_o'], c)[:, :head_dim],
                                   rt_c[:, :head_dim], rows(ch['rt_o'], c)[:, :head_dim]], axis=0)
            ch['lhs'] = lhs.astype(BF16)
        for ch in chains:
            ch['g'] = _dot(ch['lhs'], ch['s'].astype(BF16))
        for ch in chains:
            g = ch['g']
            g_a = jnp.where(head0, rows(g, 0), rows(g, 1))
            g_r = jnp.where(head0, rows(g, 2), rows(g, 3))
            sa = g_a + ch['va']
            y = g_r + ch['yv']
            for i in range(SUBLANES):
                sai = bc(sa, i)
                if i + 1 < SUBLANES:
                    sa = sa - ch['coef'][4 * i] * sai
                y = y - ch['coef'][4 * i + 2] * sai
            ch['ys'].append(y)
            v_c = ch['v_c']
            wd = jnp.concatenate([jnp.where(head0, -sa, 0.0), jnp.where(head0, 0.0, -sa),
                                  jnp.where(head0, v_c, 0.0), jnp.where(head0, 0.0, v_c)], axis=0)
            ch['wd'] = wd.astype(BF16)
        for ch in chains:
            cols = ch['xt'][:, c * 4 * SUBLANES:(c + 1) * 4 * SUBLANES]
            ch['s'] = ch['s'] + _dot(cols, ch['wd'])

    inv_n = 1.0 / head_dim
    cat = lambda key: jnp.concatenate([ch[key] for ch in chains], axis=0)
    part = lambda a, n, m: a[n * m:(n + 1) * m]
    for ch in chains:
        ch['pd'] = diag * ch['p_end']
        ch['y'] = jnp.concatenate(ch['ys'], axis=0)
        ch['rk'] = r_ref[ch['si'], :, ch['sl']] * k_ref[ch['si'], :, ch['sl']] * rk_ref[:, ch['sl']]
    p_col = _dot_split_lhs(cat('pd'), block_ones, 3)
    y_all = cat('y')
    mu = _dot_split_lhs(y_all, block_ones, 2) * inv_n
    bonus = _dot_split_lhs(cat('rk'), block_ones, 2)
    d_all = y_all - mu
    var = _dot_split_lhs(d_all * d_all, block_ones, 2) * inv_n
    for n, ch in enumerate(chains):
        si, sl = ch['si'], ch['sl']
        s_new = ch['s'] * part(p_col, n, head_dim)
        if single:
            store_state(n, s_new)
        else:
            s_ref[n] = s_new
            ch['s'] = s_new
        yn = part(d_all, n, steps) * lax.rsqrt(part(var, n, steps) + GN_EPS) * lnw_ref[:, sl] + lnb_ref[:, sl]
        out = (yn + part(bonus, n, steps) * ch['v']) * g_ref[si, :, sl].astype(F32)
        o_ref[si, :, sl] = out.astype(o_ref.dtype)

    if not single:
        @pl.when(pl.program_id(1) == pl.num_programs(1) - 1)
        def _():
            for n, ch in enumerate(chains):
                store_state(n, ch['s'])


def _post_body(x_ref, ya_ref, yb_ref, gt_ref, hist_ref, wa_ref, wb_ref, wo_ref, g2_ref, wup_ref,
               cw_ref, cb_ref, wdn_ref, gf_ref, o_ref, tail_ref, carry_ref, *, seg, final_norm):
    dm = x_ref.shape[1]
    ua = _dot(ya_ref[...].astype(BF16), wa_ref[...])
    ub = _dot(yb_ref[...].astype(BF16), wb_ref[...])
    gates = gt_ref[...].astype(F32)
    m = (gates[:, :dm] * ua + gates[:, dm:] * ub).astype(BF16)
    x1 = x_ref[...] + _dot(m, wo_ref[...])
    h2 = _rmsnorm(x1, g2_ref[...], NORM_EPS).astype(BF16)
    up = _dot(h2, wup_ref[...])
    dff = up.shape[1] // 2
    ug = up[:, :dff]
    hist = _load_hist(hist_ref, carry_ref, seg)
    ugc = _causal_conv(ug, hist, seg, cw_ref, cb_ref)
    _store_tail(ug, tail_ref, carry_ref, seg)
    act = (_silu(ugc) * up[:, dff:]).astype(BF16)
    x2 = x1 + _dot(act, wdn_ref[...])
    if final_norm:
        x2 = _rmsnorm(x2, gf_ref[...], NORM_EPS)
    o_ref[...] = x2


def _const_spec(shape):
    nd = len(shape)
    return pl.BlockSpec(shape, lambda *_: (0,) * nd, pipeline_mode=pl.Buffered(1))


def _params(sem):
    return pltpu.CompilerParams(dimension_semantics=sem, vmem_limit_bytes=VMEM_LIMIT)


def _token_tiling(nseq, seqlen):
    if seqlen == SUBLANES:
        per = ROW_TILE // SUBLANES
        assert nseq % per == 0
        return per, SUBLANES, (nseq // per, 1)
    assert seqlen % ROW_TILE == 0
    return 1, ROW_TILE, (nseq, seqlen // ROW_TILE)


def _row_spec(cols, lt):
    return pl.BlockSpec((ROW_TILE, cols), lambda i, l: (i * lt + l, 0))


def _hist_spec(per, cols):
    return pl.BlockSpec((per, SUBLANES, cols), lambda i, l: (i, 0, 0))


def _ssm_in(x, hist, p, nseq, seqlen, act_dtype):
    per, seg, grid = _token_tiling(nseq, seqlen)
    t, dm = x.shape
    dz = p['wz'].shape[1]
    dc = p['wx'].shape[1]
    dd = p['wdt'].shape[1]
    lt = grid[1]
    return pl.pallas_call(
        functools.partial(_ssm_in_body, seg=seg),
        grid=grid,
        in_specs=[_row_spec(dm, lt), _hist_spec(per, dc), _const_spec((1, dm)),
                  _const_spec(p['wz'].shape), _const_spec(p['wx'].shape), _const_spec(p['wdt'].shape),
                  _const_spec(p['conv_w'].shape), _const_spec((1, dc)), _const_spec((1, dd))],
        out_specs=[_row_spec(dz, lt), _row_spec(dc, lt), _row_spec(dd, lt), _hist_spec(per, dc)],
        out_shape=[jax.ShapeDtypeStruct((t, dz), act_dtype), jax.ShapeDtypeStruct((t, dc), F32),
                   jax.ShapeDtypeStruct((t, dd), F32), jax.ShapeDtypeStruct((nseq, SUBLANES, dc), F32)],
        scratch_shapes=[pltpu.VMEM((SUBLANES, dc), F32)],
        compiler_params=_params(("arbitrary", "arbitrary")),
        name="ssm_in",
    )(x, hist, p['norm1_g'], p['wz'], p['wx'], p['wdt'], p['conv_w'], p['conv_b'], p['dt_bias'])


def _rwkv_in(x, hist, p, nseq, seqlen, act_dtype):
    per, seg, grid = _token_tiling(nseq, seqlen)
    t, dm = x.shape
    dim = p['w0'].shape[1]
    drw = p['wrw'].shape[1]
    dg = p['wg'].shape[1]
    lt = grid[1]
    f32_out = jax.ShapeDtypeStruct((t, dim), F32)
    return pl.pallas_call(
        functools.partial(_rwkv_in_body, seg=seg, dim=dim),
        grid=grid,
        in_specs=[_row_spec(dm, lt), _hist_spec(per, drw), _const_spec((1, dm)),
                  _const_spec(p['wrw'].shape), _const_spec(p['wg'].shape), _const_spec((1, drw)),
                  _const_spec(p['wlora'].shape), _const_spec((1, dim)), _const_spec((1, dim)),
                  _const_spec((1, dim)), _const_spec((1, dim)),
                  _const_spec(p['head_sum'].shape), _const_spec(p['head_expand'].shape)],
        out_specs=[_row_spec(dim, lt)] * 7 + [_row_spec(dg, lt), _hist_spec(per, drw)],
        out_shape=[f32_out] * 6 + [jax.ShapeDtypeStruct((t, dim), act_dtype),
                                   jax.ShapeDtypeStruct((t, dg), BF16),
                                   jax.ShapeDtypeStruct((nseq, SUBLANES, drw), F32)],
        scratch_shapes=[pltpu.VMEM((SUBLANES, drw), F32)],
        compiler_params=_params(("arbitrary", "arbitrary")),
        name="rwkv_in",
    )(x, hist, p['norm1_g'], p['wrw'], p['wg'], p['mu'], p['wlora'], p['w0'], p['a0'],
      p['k_k'], p['k_a'], p['head_sum'], p['head_expand'])


def _ssd(xc, dt, z, state, p, nseq, seqlen, act_dtype):
    t = xc.shape[0]
    _, n_heads, p_dim, state_n = state.shape
    groups = dt.shape[1] // LANES
    heads = n_heads // groups
    width = heads * p_dim
    d_inner = groups * width
    assert state_n == LANES and width % LANES == 0 and p_dim * 2 == LANES
    subs = 1
    if seqlen % SSD_CHUNK == 0:
        subs = SSD_SUBCHUNKS if seqlen % (SSD_SUBCHUNKS * SSD_CHUNK) == 0 else 1
        seqs, gps, rows, nc = 1, groups, subs * SSD_CHUNK, seqlen // (subs * SSD_CHUNK)
    elif seqlen == SUBLANES and nseq % (SSD_CHUNK // SUBLANES) == 0:
        seqs, gps, rows, nc = SSD_CHUNK // SUBLANES, 1, SSD_CHUNK, 1
    else:
        assert seqlen == SUBLANES
        seqs, gps, rows, nc = 1, groups, SUBLANES, 1
    gn = gps * state_n
    b_blk = d_inner // gn
    assert d_inner % gn == 0
    row_map = lambda b, g, c: (b * nc + c, g)
    st_spec = pl.BlockSpec((seqs, gps * heads, p_dim, state_n), lambda b, g, c: (b, g, 0, 0))
    return pl.pallas_call(
        functools.partial(_ssd_body, rows=rows, heads=heads, groups=gps, seqs=seqs, subs=subs),
        grid=(nseq // seqs, groups // gps, nc),
        in_specs=[pl.BlockSpec((rows, gps * width), row_map),
                  pl.BlockSpec((rows, gn), lambda b, g, c: (b * nc + c, b_blk + g)),
                  pl.BlockSpec((rows, gn), lambda b, g, c: (b * nc + c, b_blk + groups // gps + g)),
                  pl.BlockSpec((rows, gps * LANES), row_map),
                  pl.BlockSpec((rows, gps * width), row_map),
                  st_spec,
                  pl.BlockSpec((1, gps * LANES), lambda b, g, c: (0, g)),
                  pl.BlockSpec((1, gps * width), lambda b, g, c: (0, g)),
                  pl.BlockSpec((1, gps * width), lambda b, g, c: (0, g)),
                  _const_spec(p['tri'].shape), _const_spec(p['ssd_expand'].shape),
                  _const_spec(p['ssd_expand2'].shape)],
        out_specs=[pl.BlockSpec((rows, gps * width), row_map), st_spec],
        out_shape=[jax.ShapeDtypeStruct((t, d_inner), act_dtype),
                   jax.ShapeDtypeStruct(state.shape, F32)],
        scratch_shapes=[pltpu.VMEM((gps, state_n, width), F32)],
        compiler_params=_params(("arbitrary", "arbitrary", "arbitrary")),
        name="ssd",
    )(xc, xc, xc, dt, z, state, p['a_log'], p['d_exp'], p['ssm_norm_g'],
      p['tri'], p['ssd_expand'], p['ssd_expand2'])


def _wkv(r, lw, k, v, kk, bb, g, state, p, nseq, seqlen, act_dtype):
    dim = r.shape[1]
    _, n_heads, head_dim, _ = state.shape
    npair = dim // LANES
    assert n_heads == 2 * npair and 2 * head_dim == LANES
    if seqlen == SUBLANES:
        per, steps = WKV_SAMPLE_SEQS, SUBLANES
    else:
        per, steps = (WKV_PROMPT_SEQS if nseq % WKV_PROMPT_SEQS == 0 else 1), WKV_TBLOCK
    assert nseq % per == 0 and seqlen % steps == 0
    grid = (nseq // per, seqlen // steps)
    seq_spec = pl.BlockSpec((per, steps, dim), lambda i, tb: (i, tb, 0))
    st_spec = pl.BlockSpec((per, n_heads, head_dim, head_dim), lambda i, tb: (i, 0, 0, 0))
    as3 = lambda a: a.reshape(nseq, seqlen, dim)
    out, s_out = pl.pallas_call(
        functools.partial(_wkv_body, head_dim=head_dim, single=(seqlen == steps)),
        grid=grid,
        in_specs=[seq_spec] * 7 + [st_spec, _const_spec((1, dim)), _const_spec((1, dim)),
                                   _const_spec((1, dim)), _const_spec((LANES, LANES)),
                                   _const_spec((steps, steps))],
        out_specs=[seq_spec, st_spec],
        out_shape=[jax.ShapeDtypeStruct((nseq, seqlen, dim), act_dtype),
                   jax.ShapeDtypeStruct(state.shape, F32)],
        scratch_shapes=[pltpu.VMEM((per * npair, head_dim, LANES), F32)],
        compiler_params=_params(("arbitrary", "arbitrary")),
        name="wkv",
    )(as3(r), as3(lw), as3(k), as3(v), as3(kk), as3(bb), as3(g), state,
      p['r_k'], p['ln_w'], p['ln_b'], p['block_ones'], p['tri'][:steps, :steps])
    return out.reshape(nseq * seqlen, dim), s_out


def _post(x, ya, yb, gates, hist, p, nseq, seqlen, final_norm):
    per, seg, grid = _token_tiling(nseq, seqlen)
    t, dm = x.shape
    dff = p['wdn'].shape[0]
    lt = grid[1]
    return pl.pallas_call(
        functools.partial(_post_body, seg=seg, final_norm=final_norm),
        grid=grid,
        in_specs=[_row_spec(dm, lt), _row_spec(ya.shape[1], lt), _row_spec(yb.shape[1], lt),
                  _row_spec(gates.shape[1], lt), _hist_spec(per, dff),
                  _const_spec(p['wa'].shape), _const_spec(p['wb'].shape), _const_spec(p['wo'].shape),
                  _const_spec((1, dm)), _const_spec(p['wup'].shape), _const_spec(p['ffn_conv_w'].shape),
                  _const_spec((1, dff)), _const_spec(p['wdn'].shape), _const_spec((1, dm))],
        out_specs=[_row_spec(dm, lt), _hist_spec(per, dff)],
        out_shape=[jax.ShapeDtypeStruct((t, dm), F32), jax.ShapeDtypeStruct((nseq, SUBLANES, dff), F32)],
        scratch_shapes=[pltpu.VMEM((SUBLANES, dff), F32)],
        compiler_params=_params(("arbitrary", "arbitrary")),
        name="post",
    )(x, ya, yb, gates, hist, p['wa'], p['wb'], p['wo'], p['norm2_g'], p['wup'], p['ffn_conv_w'],
      p['ffn_conv_b'], p['wdn'], p['final_g'])


def _prep_layer(w, dims):
    d_inner, conv_dim, n_heads, shift_dim, groups, rwkv_dim, rwkv_heads = dims
    row = lambda a: a.reshape(1, -1).astype(F32)
    w_in = w['w_in']
    o1 = d_inner
    o2 = o1 + conv_dim
    o3 = o2 + n_heads
    o4 = o3 + shift_dim
    hpg = n_heads // groups

    def per_group(a):
        lead = a.shape[:-1]
        a = a.reshape(lead + (groups, hpg))
        a = jnp.pad(a, [(0, 0)] * len(lead) + [(0, 0), (0, LANES - hpg)])
        return a.reshape(lead + (groups * LANES,))

    p_dim = d_inner // n_heads
    head_dim = rwkv_dim // rwkv_heads
    lora_w = w['rwkv_w_up'].shape[0]
    lora_a = w['rwkv_a_up'].shape[0]
    lora_g = w['rwkv_g_up'].shape[0]
    assert lora_w == lora_a and lora_g == 2 * lora_w
    wlora = jnp.zeros((lora_w + lora_a + lora_g, 3 * rwkv_dim), F32)
    wlora = wlora.at[:lora_w, :rwkv_dim].set(w['rwkv_w_up'])
    wlora = wlora.at[lora_w:lora_w + lora_a, rwkv_dim:2 * rwkv_dim].set(w['rwkv_a_up'])
    wlora = wlora.at[lora_w + lora_a:, 2 * rwkv_dim:].set(w['rwkv_g_up'])

    ch = jnp.arange(rwkv_dim) // head_dim
    head_sum = (ch[:, None] == jnp.arange(LANES)[None, :]).astype(BF16)
    li = jnp.arange(LANES)
    width = hpg * p_dim
    ssd_expand = (li[:, None] == (jnp.arange(width) // p_dim)[None, :]).astype(BF16)
    ssd_expand2 = (li[:, None] == (jnp.arange(hpg * SSD_CHUNK) // SSD_CHUNK)[None, :]).astype(BF16)
    ci = jnp.arange(SSD_CHUNK)
    tri = (ci[:, None] >= ci[None, :]).astype(BF16)
    block_ones = ((li[:, None] // head_dim) == (li[None, :] // head_dim)).astype(BF16)

    return {
        'norm1_g': row(w['norm1_g']),
        'wz': w_in[:, :o1].astype(BF16),
        'wx': w_in[:, o1:o2].astype(BF16),
        'wdt': per_group(w_in[:, o2:o3]).astype(BF16),
        'wrw': w_in[:, o3:o4].astype(BF16),
        'wg': w_in[:, o4:].astype(BF16),
        'conv_w': w['ssm_conv_w'].astype(F32),
        'conv_b': row(w['ssm_conv_b']),
        'dt_bias': row(per_group(w['ssm_dt_bias'])),
        'a_log': row(per_group(w['ssm_a_log'])),
        'd_exp': row(jnp.repeat(w['ssm_d'], p_dim)),
        'ssm_norm_g': row(w['ssm_norm_g']),
        'wa': w['w_branch_a'].astype(BF16),
        'mu': row(w['rwkv_mu']),
        'wlora': wlora.astype(BF16),
        'w0': row(w['rwkv_w0']),
        'a0': row(w['rwkv_a0']),
        'k_k': row(w['rwkv_k_k']),
        'k_a': row(w['rwkv_k_a']),
        'r_k': row(w['rwkv_r_k']),
        'ln_w': row(w['rwkv_ln_w']),
        'ln_b': row(w['rwkv_ln_b']),
        'wb': w['w_branch_b'].astype(BF16),
        'wo': w['w_out'].astype(BF16),
        'norm2_g': row(w['norm2_g']),
        'wup': w['ffn_w_up'].astype(BF16),
        'ffn_conv_w': w['ffn_conv_w'].astype(F32),
        'ffn_conv_b': row(w['ffn_conv_b']),
        'wdn': w['ffn_w_down'].astype(BF16),
        'head_sum': head_sum,
        'head_expand': head_sum.T,
        'ssd_expand': ssd_expand,
        'ssd_expand2': ssd_expand2,
        'tri': tri,
        'block_ones': block_ones,
    }


def _hist8(state_rows):
    k = state_rows.shape[1]
    return jnp.pad(state_rows.astype(F32), ((0, 0), (SUBLANES - k, 0), (0, 0)))


def _layer(x, conv_buf, ssm_state, shift_buf, wkv_state, ffn_buf, p, final_norm):
    nseq, seqlen, dm = x.shape
    act_dtype = BF16 if seqlen % 16 == 0 else F32
    xf = x.reshape(nseq * seqlen, dm)
    z, xc, dt, conv_tail = _ssm_in(xf, _hist8(conv_buf), p, nseq, seqlen, act_dtype)
    r, lw, k, v, kk, bb, g, gates, shift_tail = _rwkv_in(xf, _hist8(shift_buf[:, None]), p, nseq, seqlen,
                                                         act_dtype)
    ya, new_ssm = _ssd(xc, dt, z, ssm_state.astype(F32), p, nseq, seqlen, act_dtype)
    yb, new_wkv = _wkv(r, lw, k, v, kk, bb, g, wkv_state.astype(F32), p, nseq, seqlen, act_dtype)
    out, ffn_tail = _post(xf, ya, yb, gates, _hist8(ffn_buf), p, nseq, seqlen, final_norm)
    new_conv = conv_tail[:, SUBLANES - conv_buf.shape[1]:]
    new_shift = shift_tail[:, SUBLANES - 1]
    new_ffn = ffn_tail[:, SUBLANES - ffn_buf.shape[1]:]
    return out.reshape(nseq, seqlen, dm), (new_conv, new_ssm, new_shift, new_wkv, new_ffn)


_LAYER_WEIGHTS = ('norm1_g', 'w_in', 'ssm_conv_w', 'ssm_conv_b', 'ssm_dt_bias', 'ssm_a_log', 'ssm_d',
                  'ssm_norm_g', 'w_branch_a', 'rwkv_mu', 'rwkv_w0', 'rwkv_w_up', 'rwkv_a0', 'rwkv_a_up',
                  'rwkv_g_up', 'rwkv_k_k', 'rwkv_k_a', 'rwkv_r_k', 'rwkv_ln_w', 'rwkv_ln_b', 'w_branch_b',
                  'w_out', 'norm2_g', 'ffn_w_up', 'ffn_conv_w', 'ffn_conv_b', 'ffn_w_down')


def kernel(x_prompt, x_sample, state_ssm_conv, state_ssm, state_rwkv_shift, state_rwkv, state_ffn_conv,
           norm1_g, w_in, ssm_conv_w, ssm_conv_b, ssm_dt_bias, ssm_a_log, ssm_d, ssm_norm_g, w_branch_a,
           rwkv_mu, rwkv_w0, rwkv_w_up, rwkv_a0, rwkv_a_up, rwkv_g_up, rwkv_k_k, rwkv_k_a, rwkv_r_k,
           rwkv_ln_w, rwkv_ln_b, w_branch_b, w_out, norm2_g, ffn_w_up, ffn_conv_w, ffn_conv_b, ffn_w_down,
           final_g):
    stacked = dict(zip(_LAYER_WEIGHTS, (
        norm1_g, w_in, ssm_conv_w, ssm_conv_b, ssm_dt_bias, ssm_a_log, ssm_d, ssm_norm_g, w_branch_a,
        rwkv_mu, rwkv_w0, rwkv_w_up, rwkv_a0, rwkv_a_up, rwkv_g_up, rwkv_k_k, rwkv_k_a, rwkv_r_k,
        rwkv_ln_w, rwkv_ln_b, w_branch_b, w_out, norm2_g, ffn_w_up, ffn_conv_w, ffn_conv_b, ffn_w_down)))
    depth = w_in.shape[0]
    _, _, n_heads, _, _ = state_ssm.shape
    conv_dim = state_ssm_conv.shape[-1]
    d_inner = w_branch_a.shape[1]
    shift_dim = state_rwkv_shift.shape[-1]
    rwkv_heads = state_rwkv.shape[2]
    rwkv_dim = w_branch_b.shape[1]
    groups = (conv_dim - d_inner) // (2 * state_ssm.shape[-1])
    dims = (d_inner, conv_dim, n_heads, shift_dim, groups, rwkv_dim, rwkv_heads)

    xp, xs = x_prompt, x_sample
    bp = xp.shape[0]
    new_p = ([], [], [], [], [])
    new_s = ([], [], [], [], [])
    for i in range(depth):
        p = _prep_layer({name: a[i] for name, a in stacked.items()}, dims)
        p['final_g'] = final_g.reshape(1, -1).astype(F32)
        last = i == depth - 1
        xp, sp = _layer(
            xp,
            jnp.zeros((bp,) + state_ssm_conv.shape[2:], F32),
            jnp.zeros((bp,) + state_ssm.shape[2:], F32),
            jnp.zeros((bp,) + state_rwkv_shift.shape[2:], F32),
            jnp.zeros((bp,) + state_rwkv.shape[2:], F32),
            jnp.zeros((bp,) + state_ffn_conv.shape[2:], F32),
            p, last)
        xs, ss = _layer(xs, state_ssm_conv[i], state_ssm[i], state_rwkv_shift[i], state_rwkv[i],
                        state_ffn_conv[i], p, last)
        for j in range(5):
            new_p[j].append(sp[j])
            new_s[j].append(ss[j])
    return (xp, xs,
            jnp.stack(new_p[0]), jnp.stack(new_p[1]), jnp.stack(new_p[2]), jnp.stack(new_p[3]),
            jnp.stack(new_p[4]),
            jnp.stack(new_s[0]), jnp.stack(new_s[1]), jnp.stack(new_s[2]), jnp.stack(new_s[3]),
            jnp.stack(new_s[4]))
```

```python
import functools

import jax
import jax.numpy as jnp
from jax import lax
from jax.experimental import pallas as pl
from jax.experimental.pallas import tpu as pltpu

F32 = jnp.float32
BF16 = jnp.bfloat16

NORM_EPS = 1e-5
GN_EPS = 64e-5

LANES = 128
SUBLANES = 8
ROW_TILE = 256
COL_STRIP = 512
SSD_CHUNK = 128
SSD_SUBCHUNKS = 4
WKV_TBLOCK = 64
WKV_PROMPT_SEQS = 4
WKV_SAMPLE_SEQS = 8
VMEM_LIMIT = 56 * 1024 * 1024


def _dot(a, b):
    return jnp.dot(a, b, preferred_element_type=F32)


def _split(x, n):
    parts = []
    rem = x
    for i in range(n):
        p = rem.astype(BF16)
        parts.append(p)
        if i + 1 < n:
            rem = rem - p.astype(F32)
    return parts


def _dot_split_lhs(x, m, n):
    acc = None
    for p in _split(x, n):
        d = _dot(p, m)
        acc = d if acc is None else acc + d
    return acc


def _dot_split_rhs(m, x, n):
    acc = None
    for p in _split(x, n):
        d = _dot(m, p)
        acc = d if acc is None else acc + d
    return acc


def _sigmoid(x):
    return 0.5 * jnp.tanh(0.5 * x) + 0.5


def _silu(x):
    hx = 0.5 * x
    return hx * jnp.tanh(hx) + hx


def _softplus(x):
    return jnp.maximum(x, 0.0) + jnp.log1p(jnp.exp(-jnp.abs(x)))


def _rmsnorm(x, g, eps):
    ms = jnp.mean(x * x, axis=-1, keepdims=True)
    return x * lax.rsqrt(ms + eps) * g


def _shift_rows(u, hist, seg, j):
    rows = u.shape[0]
    ru = pltpu.roll(u, j, 0)
    if seg == SUBLANES:
        rh = pltpu.roll(hist, (rows - SUBLANES + j) % rows, 0)
        pos = lax.broadcasted_iota(jnp.int32, u.shape, 0) % SUBLANES
        return jnp.where(pos < j, rh, ru)
    assert seg == rows
    rh = pltpu.roll(hist, j, 0)
    pos = lax.broadcasted_iota(jnp.int32, rh.shape, 0)
    top = jnp.where(pos < j, rh, ru[:SUBLANES])
    return jnp.concatenate([top, ru[SUBLANES:]], axis=0)


def _causal_conv(u, hist, seg, w_ref, b_ref):
    taps = w_ref.shape[0]
    acc = u * w_ref[taps - 1:taps, :] + b_ref[...]
    for j in range(1, taps):
        acc = acc + _shift_rows(u, hist, seg, j) * w_ref[taps - 1 - j:taps - j, :]
    return acc


def _load_hist(hist_ref, carry_ref, seg):
    if seg == SUBLANES:
        nseq, _, c = hist_ref.shape
        return hist_ref[...].reshape(nseq * SUBLANES, c)

    @pl.when(pl.program_id(1) == 0)
    def _():
        carry_ref[...] = hist_ref[0]

    return carry_ref[...]


def _store_tail(u, tail_ref, carry_ref, seg):
    if seg == SUBLANES:
        tail_ref[...] = u.reshape(tail_ref.shape)
    else:
        last = u[u.shape[0] - SUBLANES:]
        carry_ref[...] = last
        tail_ref[0] = last


def _ssm_in_body(x_ref, hist_ref, g1_ref, wz_ref, wx_ref, wdt_ref, cw_ref, cb_ref, dtb_ref,
                 z_ref, xc_ref, dt_ref, tail_ref, carry_ref, *, seg):
    h = _rmsnorm(x_ref[...], g1_ref[...], NORM_EPS).astype(BF16)
    hist = _load_hist(hist_ref, carry_ref, seg)
    taps = cw_ref.shape[0]
    for lo in range(0, wx_ref.shape[1], COL_STRIP):
        sl = slice(lo, lo + COL_STRIP)
        u = _dot(h, wx_ref[:, sl])
        hs = hist[:, sl]
        acc = u * cw_ref[taps - 1:taps, sl] + cb_ref[:, sl]
        for j in range(1, taps):
            acc = acc + _shift_rows(u, hs, seg, j) * cw_ref[taps - 1 - j:taps - j, sl]
        xc_ref[:, sl] = _silu(acc)
        if seg == SUBLANES:
            tail_ref[:, :, sl] = u.reshape(tail_ref.shape[0], SUBLANES, COL_STRIP)
        else:
            last = u[u.shape[0] - SUBLANES:]
            carry_ref[:, sl] = last
            tail_ref[0, :, sl] = last
    for lo in range(0, wz_ref.shape[1], COL_STRIP):
        sl = slice(lo, lo + COL_STRIP)
        z_ref[:, sl] = _dot(h, wz_ref[:, sl]).astype(z_ref.dtype)
    dt_ref[...] = _softplus(_dot(h, wdt_ref[...]) + dtb_ref[...])


def _rwkv_in_body(x_ref, hist_ref, g1_ref, wrw_ref, wg_ref, mu_ref, wlora_ref, w0_ref, a0_ref,
                  kk_ref, ka_ref, seg_ref, exp_ref,
                  r_out, lw_out, k_out, v_out, kkn_out, bb_out, g_out, gates_out, tail_ref,
                  carry_ref, *, seg, dim):
    h = _rmsnorm(x_ref[...], g1_ref[...], NORM_EPS).astype(BF16)
    hist = _load_hist(hist_ref, carry_ref, seg)
    drw = wrw_ref.shape[1]

    def mixed(lo, hi):
        sl = slice(lo, hi)
        u = _dot(h, wrw_ref[:, sl])
        prev = _shift_rows(u, hist[:, sl], seg, 1)
        if seg == SUBLANES:
            tail_ref[:, :, sl] = u.reshape(tail_ref.shape[0], SUBLANES, hi - lo)
        else:
            last = u[u.shape[0] - SUBLANES:]
            carry_ref[:, sl] = last
            tail_ref[0, :, sl] = last
        return u + (prev - u) * mu_ref[:, sl]

    low = mixed(3 * dim, drw)
    lane = lax.broadcasted_iota(jnp.int32, low.shape, 1)
    lo_w = low.shape[1] // 4
    act = jnp.where(lane < lo_w, jnp.tanh(low), jnp.where(lane < 2 * lo_w, low, _sigmoid(low))).astype(BF16)
    r_out[...] = mixed(0, dim)
    wlog = -_softplus(-(w0_ref[...] + _dot(act, wlora_ref[:, 0:dim]))) - 0.5
    lw_out[...] = -jnp.exp(wlog)
    v_out[...] = mixed(2 * dim, 3 * dim)
    a = _sigmoid(a0_ref[...] + _dot(act, wlora_ref[:, dim:2 * dim]))
    k = mixed(dim, 2 * dim)
    kkr = k * kk_ref[...]
    ss = _dot_split_lhs(kkr * kkr, seg_ref[...], 2)
    inv = 1.0 / jnp.maximum(jnp.sqrt(ss), 1e-12)
    kkn = kkr * _dot_split_lhs(inv, exp_ref[...], 2)
    k_out[...] = k * (1.0 + (a - 1.0) * ka_ref[...])
    kkn_out[...] = kkn
    bb_out[...] = kkn * a
    g_out[...] = _dot(act, wlora_ref[:, 2 * dim:3 * dim]).astype(g_out.dtype)
    for lo in range(0, wg_ref.shape[1], COL_STRIP):
        sl = slice(lo, lo + COL_STRIP)
        gates_out[:, sl] = _sigmoid(_dot(h, wg_ref[:, sl])).astype(gates_out.dtype)


def _ssd_body(xm_ref, b_ref, c_ref, dt_ref, z_ref, st_ref, alog_ref, dexp_ref, ng_ref,
              tri_ref, e_ref, e2_ref, y_ref, so_ref, ht_ref, *, rows, heads, groups, seqs, subs):
    step = pl.program_id(2)
    n_steps = pl.num_programs(2)
    R = SSD_CHUNK
    width = xm_ref.shape[1] // groups
    state_n = b_ref.shape[1] // groups
    p_dim = width // heads
    gs = range(groups)
    units = [(sc, g) for sc in range(subs) for g in gs]
    us = range(len(units))
    steps = R // seqs

    def blk(ref, u, n):
        sc, g = units[u]
        v = ref[sc * R:(sc + 1) * R, g * n:(g + 1) * n] if subs > 1 else ref[:, g * n:(g + 1) * n]
        if v.shape[0] == R:
            return v
        return jnp.concatenate([v, jnp.zeros((R - v.shape[0], v.shape[1]), v.dtype)], axis=0)

    def gcols(ref, u, n):
        g = units[u][1]
        return ref[:, g * n:(g + 1) * n]

    row = lax.broadcasted_iota(jnp.int32, (R, R), 0)
    col = lax.broadcasted_iota(jnp.int32, (R, R), 1)
    lane = lax.broadcasted_iota(jnp.int32, (R, LANES), 1)
    expand = e_ref[...]
    expand2 = e2_ref[...]
    if seqs == 1:
        causal = row >= col
        tri = tri_ref[...]

        @pl.when(step == 0)
        def _():
            for g in gs:
                ht_ref[g] = st_ref[0, g * heads:(g + 1) * heads].reshape(width, state_n).T
    else:
        same = (row // steps) == (col // steps)
        causal = same & (row >= col)
        tri = causal.astype(BF16)

    xm = [blk(xm_ref, u, width) for u in us]
    bm = [blk(b_ref, u, state_n) for u in us]
    cm = [blk(c_ref, u, state_n) for u in us]
    cmb = [cm[u].astype(BF16) for u in us]
    dt = [blk(dt_ref, u, LANES) for u in us]
    a = [dt[u] * (-jnp.exp(gcols(alog_ref, u, LANES))) for u in us]
    cs = [_dot_split_rhs(tri, a[u], 3) for u in us]
    dt_e = [_dot_split_lhs(dt[u], expand, 2) for u in us]
    cs_e = [_dot_split_lhs(cs[u], expand, 3) for u in us]
    if seqs > 1:
        cs_e2 = [_dot_split_lhs(cs[u], expand2, 3) for u in us]
    bt = [bm[u].T for u in us]
    btb = [bt[u].astype(BF16) for u in us]
    cs_t = [cs[u].T for u in us]
    if seqs == 1:
        end_e = [cs_e[u][R - 1:R, :] for u in us]
    else:
        tot = [_dot_split_rhs(same.astype(BF16), a[u], 3) for u in us]
        end_e = [_dot_split_lhs(tot[u], expand, 3) for u in us]
    xdt = [xm[u] * dt_e[u] for u in us]
    xs = [(xdt[u] * jnp.exp(end_e[u] - cs_e[u])).astype(BF16) for u in us]
    cb = [_dot(cmb[u], btb[u]) for u in us]
    if seqs == 1:
        st_new = [_dot(btb[u], xs[u]) for u in us]
        ht = [ht_ref[g] for g in gs]
        y_off = []
        for u, (sc, g) in enumerate(units):
            y_off.append(_dot(cmb[u], ht[g].astype(BF16)))
            ht[g] = ht[g] * jnp.exp(end_e[u]) + st_new[u]
        for g in gs:
            ht_ref[g] = ht[g]
    else:
        y_off = []
        for u, (sc, g) in enumerate(units):
            tiles = []
            for s in range(seqs):
                r0 = s * steps
                ht0 = st_ref[s, g * heads:(g + 1) * heads].reshape(width, state_n).T
                tiles.append(_dot(cm[u][r0:r0 + steps].astype(BF16), ht0.astype(BF16)))
                bts = jnp.where((col // steps) == s, bt[u], 0.0).astype(BF16)
                ht1 = ht0 * jnp.exp(end_e[u][r0:r0 + 1]) + _dot(bts, xs[u])
                so_ref[s, g * heads:(g + 1) * heads] = ht1.T.reshape((heads,) + so_ref.shape[2:])
            y_off.append(jnp.concatenate(tiles, axis=0))
    ys = [[] for _ in us]
    for j in range(heads // 2):
        lhs, rhs = [], []
        for u in us:
            ms = []
            for hh in (2 * j, 2 * j + 1):
                if seqs == 1:
                    cs_col = jnp.broadcast_to(cs[u][:, hh:hh + 1], (R, R))
                else:
                    cs_col = cs_e2[u][:, hh * R:(hh + 1) * R]
                seg_ = cs_col - cs_t[u][hh:hh + 1, :]
                ms.append(jnp.where(causal, cb[u] * jnp.exp(jnp.where(causal, seg_, 0.0)), 0.0).astype(BF16))
            lhs.append(jnp.concatenate(ms, axis=1))
            xp = xdt[u][:, j * LANES:(j + 1) * LANES]
            rhs.append(jnp.concatenate([jnp.where(lane < p_dim, xp, 0.0), jnp.where(lane >= p_dim, xp, 0.0)],
                                       axis=0).astype(BF16))
        for u in us:
            ys[u].append(_dot(lhs[u], rhs[u]))
    for u, (sc, g) in enumerate(units):
        y = jnp.concatenate(ys[u], axis=1) + y_off[u] * jnp.exp(cs_e[u])
        y = y + gcols(dexp_ref, u, width) * xm[u]
        yz = y * _silu(blk(z_ref, u, width).astype(F32))
        yn = yz * lax.rsqrt(jnp.mean(yz * yz, axis=-1, keepdims=True) + NORM_EPS) * gcols(ng_ref, u, width)
        if subs > 1:
            y_ref[sc * R:(sc + 1) * R, g * width:(g + 1) * width] = yn.astype(y_ref.dtype)
        else:
            y_ref[:, g * width:(g + 1) * width] = yn[:rows].astype(y_ref.dtype)

    if seqs == 1:
        @pl.when(step == n_steps - 1)
        def _():
            for g in gs:
                so_ref[0, g * heads:(g + 1) * heads] = ht_ref[g].T.reshape((heads,) + so_ref.shape[2:])


def _wkv_body(r_ref, lw_ref, k_ref, v_ref, kk_ref, bb_ref, g_ref, s0_ref, rk_ref, lnw_ref, lnb_ref,
              bo_ref, tri_ref, o_ref, so_ref, s_ref, *, head_dim, single):
    nseq, steps, dim = r_ref.shape
    npair = dim // LANES
    nchunk = steps // SUBLANES
    block_ones = bo_ref[...]
    tri = tri_ref[...]

    def load_state(si, p):
        x = s0_ref[si, 2 * p:2 * p + 2].reshape(2 * head_dim, head_dim)
        xp = jnp.concatenate([x, jnp.zeros((2 * head_dim, LANES - head_dim), F32)], axis=1)
        return xp.T[:head_dim]

    def store_state(n, s):
        sp = jnp.concatenate([s, jnp.zeros((LANES - head_dim, LANES), F32)], axis=0)
        back = sp.T[:, :head_dim]
        so_ref[n // npair, 2 * (n % npair):2 * (n % npair) + 2] = back.reshape(2, head_dim, head_dim)

    if not single:
        @pl.when(pl.program_id(1) == 0)
        def _():
            for si in range(nseq):
                for p in range(npair):
                    s_ref[si * npair + p] = load_state(si, p)

    lane8 = lax.broadcasted_iota(jnp.int32, (SUBLANES, LANES), 1)
    row8 = lax.broadcasted_iota(jnp.int32, (SUBLANES, LANES), 0)
    head0 = lane8 < head_dim
    rowi = lax.broadcasted_iota(jnp.int32, (head_dim, LANES), 0)
    lanei = lax.broadcasted_iota(jnp.int32, (head_dim, LANES), 1)
    diag = (rowi == lanei % head_dim).astype(F32)

    def other_head(a):
        return pltpu.roll(a, head_dim, 1)

    def rows(a, c):
        return a[c * SUBLANES:(c + 1) * SUBLANES]

    def bc(tile, i):
        return jnp.broadcast_to(tile[i:i + 1], (SUBLANES, LANES))

    chains = []
    for si in range(nseq):
        cum_seq = _dot_split_rhs(tri, lw_ref[si], 3)
        for p in range(npair):
            sl = slice(p * LANES, (p + 1) * LANES)
            lw = lw_ref[si, :, sl]
            cum = cum_seq[:, sl]
            p_in = jnp.exp(cum)
            p_inv = jnp.exp(-cum)
            at = kk_ref[si, :, sl] * jnp.exp(cum - lw)
            rt = r_ref[si, :, sl] * p_in
            bt = bb_ref[si, :, sl] * p_inv
            kt = k_ref[si, :, sl] * p_inv
            at_o, rt_o, bt_o, kt_o = other_head(at), other_head(rt), other_head(bt), other_head(kt)
            x4 = jnp.concatenate([rows(a, c) for c in range(nchunk) for a in (bt, bt_o, kt, kt_o)], axis=0)
            if x4.shape[0] < LANES:
                x4 = jnp.concatenate([x4, jnp.zeros((LANES - x4.shape[0], LANES), F32)], axis=0)
            xt = x4.T[:head_dim].astype(BF16)
            chains.append(dict(si=si, sl=sl, at=at, rt=rt, bt=bt, kt=kt, at_o=at_o, rt_o=rt_o, xt=xt,
                               v=v_ref[si, :, sl], p_end=p_in[steps - 1:steps],
                               s=load_state(si, p) if single else s_ref[si * npair + p],
                               ys=[]))

    for c in range(nchunk):
        for ch in chains:
            at_c, rt_c, bt_c, kt_c = rows(ch['at'], c), rows(ch['rt'], c), rows(ch['bt'], c), rows(ch['kt'], c)
            tiles = []
            for i in range(SUBLANES):
                am = jnp.where(row8 > i, at_c, 0.0)
                rm = jnp.where(row8 >= i, rt_c, 0.0)
                bi, ki = bc(bt_c, i), bc(kt_c, i)
                tiles += [am * bi, am * ki, rm * bi, rm * ki]
            coef = _dot(jnp.concatenate(tiles, axis=0).astype(BF16), block_ones)
            ch['coef'] = [rows(coef, n) for n in range(4 * SUBLANES)]
            v_c = rows(ch['v'], c)
            va = None
            yv = None
            for i in range(SUBLANES):
                vi = bc(v_c, i)
                t_ak = ch['coef'][4 * i + 1] * vi
                t_rk = ch['coef'][4 * i + 3] * vi
                va = t_ak if va is None else va + t_ak
                yv = t_rk if yv is None else yv + t_rk
            ch['va'], ch['yv'], ch['v_c'] = va, yv, v_c
            lhs = jnp.concatenate([at_c[:, :head_dim], rows(ch['at_o'], c)[:, :head_dim],
                                   rt_c[:, :head_dim], rows(ch['rt_o'], c)[:, :head_dim]], axis=0)
            ch['lhs'] = lhs.astype(BF16)
        for ch in chains:
            ch['g'] = _dot(ch['lhs'], ch['s'].astype(BF16))
        for ch in chains:
            g = ch['g']
            g_a = jnp.where(head0, rows(g, 0), rows(g, 1))
            g_r = jnp.where(head0, rows(g, 2), rows(g, 3))
            sa = g_a + ch['va']
            y = g_r + ch['yv']
            for i in range(SUBLANES):
                sai = bc(sa, i)
                if i + 1 < SUBLANES:
                    sa = sa - ch['coef'][4 * i] * sai
                y = y - ch['coef'][4 * i + 2] * sai
            ch['ys'].append(y)
            v_c = ch['v_c']
            wd = jnp.concatenate([jnp.where(head0, -sa, 0.0), jnp.where(head0, 0.0, -sa),
                                  jnp.where(head0, v_c, 0.0), jnp.where(head0, 0.0, v_c)], axis=0)
            ch['wd'] = wd.astype(BF16)
        for ch in chains:
            cols = ch['xt'][:, c * 4 * SUBLANES:(c + 1) * 4 * SUBLANES]
            ch['s'] = ch['s'] + _dot(cols, ch['wd'])

    inv_n = 1.0 / head_dim
    cat = lambda key: jnp.concatenate([ch[key] for ch in chains], axis=0)
    part = lambda a, n, m: a[n * m:(n + 1) * m]
    for ch in chains:
        ch['pd'] = diag * ch['p_end']
        ch['y'] = jnp.concatenate(ch['ys'], axis=0)
        ch['rk'] = r_ref[ch['si'], :, ch['sl']] * k_ref[ch['si'], :, ch['sl']] * rk_ref[:, ch['sl']]
    p_col = _dot_split_lhs(cat('pd'), block_ones, 3)
    y_all = cat('y')
    mu = _dot_split_lhs(y_all, block_ones, 2) * inv_n
    bonus = _dot_split_lhs(cat('rk'), block_ones, 2)
    d_all = y_all - mu
    var = _dot_split_lhs(d_all * d_all, block_ones, 2) * inv_n
    for n, ch in enumerate(chains):
        si, sl = ch['si'], ch['sl']
        s_new = ch['s'] * part(p_col, n, head_dim)
        if single:
            store_state(n, s_new)
        else:
            s_ref[n] = s_new
            ch['s'] = s_new
        yn = part(d_all, n, steps) * lax.rsqrt(part(var, n, steps) + GN_EPS) * lnw_ref[:, sl] + lnb_ref[:, sl]
        out = (yn + part(bonus, n, steps) * ch['v']) * g_ref[si, :, sl].astype(F32)
        o_ref[si, :, sl] = out.astype(o_ref.dtype)

    if not single:
        @pl.when(pl.program_id(1) == pl.num_programs(1) - 1)
        def _():
            for n, ch in enumerate(chains):
                store_state(n, ch['s'])


def _post_body(x_ref, ya_ref, yb_ref, gt_ref, hist_ref, wa_ref, wb_ref, wo_ref, g2_ref, wup_ref,
               cw_ref, cb_ref, wdn_ref, gf_ref, o_ref, tail_ref, carry_ref, *, seg, final_norm):
    dm = x_ref.shape[1]
    ua = _dot(ya_ref[...].astype(BF16), wa_ref[...])
    ub = _dot(yb_ref[...].astype(BF16), wb_ref[...])
    gates = gt_ref[...].astype(F32)
    m = (gates[:, :dm] * ua + gates[:, dm:] * ub).astype(BF16)
    x1 = x_ref[...] + _dot(m, wo_ref[...])
    h2 = _rmsnorm(x1, g2_ref[...], NORM_EPS).astype(BF16)
    up = _dot(h2, wup_ref[...])
    dff = up.shape[1] // 2
    ug = up[:, :dff]
    hist = _load_hist(hist_ref, carry_ref, seg)
    ugc = _causal_conv(ug, hist, seg, cw_ref, cb_ref)
    _store_tail(ug, tail_ref, carry_ref, seg)
    act = (_silu(ugc) * up[:, dff:]).astype(BF16)
    x2 = x1 + _dot(act, wdn_ref[...])
    if final_norm:
        x2 = _rmsnorm(x2, gf_ref[...], NORM_EPS)
    o_ref[...] = x2


def _const_spec(shape):
    nd = len(shape)
    return pl.BlockSpec(shape, lambda *_: (0,) * nd, pipeline_mode=pl.Buffered(1))


def _params(sem):
    return pltpu.CompilerParams(dimension_semantics=sem, vmem_limit_bytes=VMEM_LIMIT)


def _token_tiling(nseq, seqlen):
    if seqlen == SUBLANES:
        per = ROW_TILE // SUBLANES
        assert nseq % per == 0
        return per, SUBLANES, (nseq // per, 1)
    assert seqlen % ROW_TILE == 0
    return 1, ROW_TILE, (nseq, seqlen // ROW_TILE)


def _row_spec(cols, lt):
    return pl.BlockSpec((ROW_TILE, cols), lambda i, l: (i * lt + l, 0))


def _hist_spec(per, cols):
    return pl.BlockSpec((per, SUBLANES, cols), lambda i, l: (i, 0, 0))


def _ssm_in(x, hist, p, nseq, seqlen, act_dtype):
    per, seg, grid = _token_tiling(nseq, seqlen)
    t, dm = x.shape
    dz = p['wz'].shape[1]
    dc = p['wx'].shape[1]
    dd = p['wdt'].shape[1]
    lt = grid[1]
    return pl.pallas_call(
        functools.partial(_ssm_in_body, seg=seg),
        grid=grid,
        in_specs=[_row_spec(dm, lt), _hist_spec(per, dc), _const_spec((1, dm)),
                  _const_spec(p['wz'].shape), _const_spec(p['wx'].shape), _const_spec(p['wdt'].shape),
                  _const_spec(p['conv_w'].shape), _const_spec((1, dc)), _const_spec((1, dd))],
        out_specs=[_row_spec(dz, lt), _row_spec(dc, lt), _row_spec(dd, lt), _hist_spec(per, dc)],
        out_shape=[jax.ShapeDtypeStruct((t, dz), act_dtype), jax.ShapeDtypeStruct((t, dc), F32),
                   jax.ShapeDtypeStruct((t, dd), F32), jax.ShapeDtypeStruct((nseq, SUBLANES, dc), F32)],
        scratch_shapes=[pltpu.VMEM((SUBLANES, dc), F32)],
        compiler_params=_params(("arbitrary", "arbitrary")),
        name="ssm_in",
    )(x, hist, p['norm1_g'], p['wz'], p['wx'], p['wdt'], p['conv_w'], p['conv_b'], p['dt_bias'])


def _rwkv_in(x, hist, p, nseq, seqlen, act_dtype):
    per, seg, grid = _token_tiling(nseq, seqlen)
    t, dm = x.shape
    dim = p['w0'].shape[1]
    drw = p['wrw'].shape[1]
    dg = p['wg'].shape[1]
    lt = grid[1]
    f32_out = jax.ShapeDtypeStruct((t, dim), F32)
    return pl.pallas_call(
        functools.partial(_rwkv_in_body, seg=seg, dim=dim),
        grid=grid,
        in_specs=[_row_spec(dm, lt), _hist_spec(per, drw), _const_spec((1, dm)),
                  _const_spec(p['wrw'].shape), _const_spec(p['wg'].shape), _const_spec((1, drw)),
                  _const_spec(p['wlora'].shape), _const_spec((1, dim)), _const_spec((1, dim)),
                  _const_spec((1, dim)), _const_spec((1, dim)),
                  _const_spec(p['head_sum'].shape), _const_spec(p['head_expand'].shape)],
        out_specs=[_row_spec(dim, lt)] * 7 + [_row_spec(dg, lt), _hist_spec(per, drw)],
        out_shape=[f32_out] * 6 + [jax.ShapeDtypeStruct((t, dim), act_dtype),
                                   jax.ShapeDtypeStruct((t, dg), BF16),
                                   jax.ShapeDtypeStruct((nseq, SUBLANES, drw), F32)],
        scratch_shapes=[pltpu.VMEM((SUBLANES, drw), F32)],
        compiler_params=_params(("arbitrary", "arbitrary")),
        name="rwkv_in",
    )(x, hist, p['norm1_g'], p['wrw'], p['wg'], p['mu'], p['wlora'], p['w0'], p['a0'],
      p['k_k'], p['k_a'], p['head_sum'], p['head_expand'])


def _ssd(xc, dt, z, state, p, nseq, seqlen, act_dtype):
    t = xc.shape[0]
    _, n_heads, p_dim, state_n = state.shape
    groups = dt.shape[1] // LANES
    heads = n_heads // groups
    width = heads * p_dim
    d_inner = groups * width
    assert state_n == LANES and width % LANES == 0 and p_dim * 2 == LANES
    subs = 1
    if seqlen % SSD_CHUNK == 0:
        subs = SSD_SUBCHUNKS if seqlen % (SSD_SUBCHUNKS * SSD_CHUNK) == 0 else 1
        seqs, gps, rows, nc = 1, groups, subs * SSD_CHUNK, seqlen // (subs * SSD_CHUNK)
    elif seqlen == SUBLANES and nseq % (SSD_CHUNK // SUBLANES) == 0:
        seqs, gps, rows, nc = SSD_CHUNK // SUBLANES, 1, SSD_CHUNK, 1
    else:
        assert seqlen == SUBLANES
        seqs, gps, rows, nc = 1, groups, SUBLANES, 1
    gn = gps * state_n
    b_blk = d_inner // gn
    assert d_inner % gn == 0
    row_map = lambda b, g, c: (b * nc + c, g)
    st_spec = pl.BlockSpec((seqs, gps * heads, p_dim, state_n), lambda b, g, c: (b, g, 0, 0))
    return pl.pallas_call(
        functools.partial(_ssd_body, rows=rows, heads=heads, groups=gps, seqs=seqs, subs=subs),
        grid=(nseq // seqs, groups // gps, nc),
        in_specs=[pl.BlockSpec((rows, gps * width), row_map),
                  pl.BlockSpec((rows, gn), lambda b, g, c: (b * nc + c, b_blk + g)),
                  pl.BlockSpec((rows, gn), lambda b, g, c: (b * nc + c, b_blk + groups // gps + g)),
                  pl.BlockSpec((rows, gps * LANES), row_map),
                  pl.BlockSpec((rows, gps * width), row_map),
                  st_spec,
                  pl.BlockSpec((1, gps * LANES), lambda b, g, c: (0, g)),
                  pl.BlockSpec((1, gps * width), lambda b, g, c: (0, g)),
                  pl.BlockSpec((1, gps * width), lambda b, g, c: (0, g)),
                  _const_spec(p['tri'].shape), _const_spec(p['ssd_expand'].shape),
                  _const_spec(p['ssd_expand2'].shape)],
        out_specs=[pl.BlockSpec((rows, gps * width), row_map), st_spec],
        out_shape=[jax.ShapeDtypeStruct((t, d_inner), act_dtype),
                   jax.ShapeDtypeStruct(state.shape, F32)],
        scratch_shapes=[pltpu.VMEM((gps, state_n, width), F32)],
        compiler_params=_params(("arbitrary", "arbitrary", "arbitrary")),
        name="ssd",
    )(xc, xc, xc, dt, z, state, p['a_log'], p['d_exp'], p['ssm_norm_g'],
      p['tri'], p['ssd_expand'], p['ssd_expand2'])


def _wkv(r, lw, k, v, kk, bb, g, state, p, nseq, seqlen, act_dtype):
    dim = r.shape[1]
    _, n_heads, head_dim, _ = state.shape
    npair = dim // LANES
    assert n_heads == 2 * npair and 2 * head_dim == LANES
    if seqlen == SUBLANES:
        per, steps = WKV_SAMPLE_SEQS, SUBLANES
    else:
        per, steps = (WKV_PROMPT_SEQS if nseq % WKV_PROMPT_SEQS == 0 else 1), WKV_TBLOCK
    assert nseq % per == 0 and seqlen % steps == 0
    grid = (nseq // per, seqlen // steps)
    seq_spec = pl.BlockSpec((per, steps, dim), lambda i, tb: (i, tb, 0))
    st_spec = pl.BlockSpec((per, n_heads, head_dim, head_dim), lambda i, tb: (i, 0, 0, 0))
    as3 = lambda a: a.reshape(nseq, seqlen, dim)
    out, s_out = pl.pallas_call(
        functools.partial(_wkv_body, head_dim=head_dim, single=(seqlen == steps)),
        grid=grid,
        in_specs=[seq_spec] * 7 + [st_spec, _const_spec((1, dim)), _const_spec((1, dim)),
                                   _const_spec((1, dim)), _const_spec((LANES, LANES)),
                                   _const_spec((steps, steps))],
        out_specs=[seq_spec, st_spec],
        out_shape=[jax.ShapeDtypeStruct((nseq, seqlen, dim), act_dtype),
                   jax.ShapeDtypeStruct(state.shape, F32)],
        scratch_shapes=[pltpu.VMEM((per * npair, head_dim, LANES), F32)],
        compiler_params=_params(("arbitrary", "arbitrary")),
        name="wkv",
    )(as3(r), as3(lw), as3(k), as3(v), as3(kk), as3(bb), as3(g), state,
      p['r_k'], p['ln_w'], p['ln_b'], p['block_ones'], p['tri'][:steps, :steps])
    return out.reshape(nseq * seqlen, dim), s_out


def _post(x, ya, yb, gates, hist, p, nseq, seqlen, final_norm):
    per, seg, grid = _token_tiling(nseq, seqlen)
    t, dm = x.shape
    dff = p['wdn'].shape[0]
    lt = grid[1]
    return pl.pallas_call(
        functools.partial(_post_body, seg=seg, final_norm=final_norm),
        grid=grid,
        in_specs=[_row_spec(dm, lt), _row_spec(ya.shape[1], lt), _row_spec(yb.shape[1], lt),
                  _row_spec(gates.shape[1], lt), _hist_spec(per, dff),
                  _const_spec(p['wa'].shape), _const_spec(p['wb'].shape), _const_spec(p['wo'].shape),
                  _const_spec((1, dm)), _const_spec(p['wup'].shape), _const_spec(p['ffn_conv_w'].shape),
                  _const_spec((1, dff)), _const_spec(p['wdn'].shape), _const_spec((1, dm))],
        out_specs=[_row_spec(dm, lt), _hist_spec(per, dff)],
        out_shape=[jax.ShapeDtypeStruct((t, dm), F32), jax.ShapeDtypeStruct((nseq, SUBLANES, dff), F32)],
        scratch_shapes=[pltpu.VMEM((SUBLANES, dff), F32)],
        compiler_params=_params(("arbitrary", "arbitrary")),
        name="post",
    )(x, ya, yb, gates, hist, p['wa'], p['wb'], p['wo'], p['norm2_g'], p['wup'], p['ffn_conv_w'],
      p['ffn_conv_b'], p['wdn'], p['final_g'])


def _prep_layer(w, dims):
    d_inner, conv_dim, n_heads, shift_dim, groups, rwkv_dim, rwkv_heads = dims
    row = lambda a: a.reshape(1, -1).astype(F32)
    w_in = w['w_in']
    o1 = d_inner
    o2 = o1 + conv_dim
    o3 = o2 + n_heads
    o4 = o3 + shift_dim
    hpg = n_heads // groups

    def per_group(a):
        lead = a.shape[:-1]
        a = a.reshape(lead + (groups, hpg))
        a = jnp.pad(a, [(0, 0)] * len(lead) + [(0, 0), (0, LANES - hpg)])
        return a.reshape(lead + (groups * LANES,))

    p_dim = d_inner // n_heads
    head_dim = rwkv_dim // rwkv_heads
    lora_w = w['rwkv_w_up'].shape[0]
    lora_a = w['rwkv_a_up'].shape[0]
    lora_g = w['rwkv_g_up'].shape[0]
    assert lora_w == lora_a and lora_g == 2 * lora_w
    wlora = jnp.zeros((lora_w + lora_a + lora_g, 3 * rwkv_dim), F32)
    wlora = wlora.at[:lora_w, :rwkv_dim].set(w['rwkv_w_up'])
    wlora = wlora.at[lora_w:lora_w + lora_a, rwkv_dim:2 * rwkv_dim].set(w['rwkv_a_up'])
    wlora = wlora.at[lora_w + lora_a:, 2 * rwkv_dim:].set(w['rwkv_g_up'])

    ch = jnp.arange(rwkv_dim) // head_dim
    head_sum = (ch[:, None] == jnp.arange(LANES)[None, :]).astype(BF16)
    li = jnp.arange(LANES)
    width = hpg * p_dim
    ssd_expand = (li[:, None] == (jnp.arange(width) // p_dim)[None, :]).astype(BF16)
    ssd_expand2 = (li[:, None] == (jnp.arange(hpg * SSD_CHUNK) // SSD_CHUNK)[None, :]).astype(BF16)
    ci = jnp.arange(SSD_CHUNK)
    tri = (ci[:, None] >= ci[None, :]).astype(BF16)
    block_ones = ((li[:, None] // head_dim) == (li[None, :] // head_dim)).astype(BF16)

    return {
        'norm1_g': row(w['norm1_g']),
        'wz': w_in[:, :o1].astype(BF16),
        'wx': w_in[:, o1:o2].astype(BF16),
        'wdt': per_group(w_in[:, o2:o3]).astype(BF16),
        'wrw': w_in[:, o3:o4].astype(BF16),
        'wg': w_in[:, o4:].astype(BF16),
        'conv_w': w['ssm_conv_w'].astype(F32),
        'conv_b': row(w['ssm_conv_b']),
        'dt_bias': row(per_group(w['ssm_dt_bias'])),
        'a_log': row(per_group(w['ssm_a_log'])),
        'd_exp': row(jnp.repeat(w['ssm_d'], p_dim)),
        'ssm_norm_g': row(w['ssm_norm_g']),
        'wa': w['w_branch_a'].astype(BF16),
        'mu': row(w['rwkv_mu']),
        'wlora': wlora.astype(BF16),
        'w0': row(w['rwkv_w0']),
        'a0': row(w['rwkv_a0']),
        'k_k': row(w['rwkv_k_k']),
        'k_a': row(w['rwkv_k_a']),
        'r_k': row(w['rwkv_r_k']),
        'ln_w': row(w['rwkv_ln_w']),
        'ln_b': row(w['rwkv_ln_b']),
        'wb': w['w_branch_b'].astype(BF16),
        'wo': w['w_out'].astype(BF16),
        'norm2_g': row(w['norm2_g']),
        'wup': w['ffn_w_up'].astype(BF16),
        'ffn_conv_w': w['ffn_conv_w'].astype(F32),
        'ffn_conv_b': row(w['ffn_conv_b']),
        'wdn': w['ffn_w_down'].astype(BF16),
        'head_sum': head_sum,
        'head_expand': head_sum.T,
        'ssd_expand': ssd_expand,
        'ssd_expand2': ssd_expand2,
        'tri': tri,
        'block_ones': block_ones,
    }


def _hist8(state_rows):
    k = state_rows.shape[1]
    return jnp.pad(state_rows.astype(F32), ((0, 0), (SUBLANES - k, 0), (0, 0)))


def _layer(x, conv_buf, ssm_state, shift_buf, wkv_state, ffn_buf, p, final_norm):
    nseq, seqlen, dm = x.shape
    act_dtype = BF16 if seqlen % 16 == 0 else F32
    xf = x.reshape(nseq * seqlen, dm)
    z, xc, dt, conv_tail = _ssm_in(xf, _hist8(conv_buf), p, nseq, seqlen, act_dtype)
    r, lw, k, v, kk, bb, g, gates, shift_tail = _rwkv_in(xf, _hist8(shift_buf[:, None]), p, nseq, seqlen,
                                                         act_dtype)
    ya, new_ssm = _ssd(xc, dt, z, ssm_state.astype(F32), p, nseq, seqlen, act_dtype)
    yb, new_wkv = _wkv(r, lw, k, v, kk, bb, g, wkv_state.astype(F32), p, nseq, seqlen, act_dtype)
    out, ffn_tail = _post(xf, ya, yb, gates, _hist8(ffn_buf), p, nseq, seqlen, final_norm)
    new_conv = conv_tail[:, SUBLANES - conv_buf.shape[1]:]
    new_shift = shift_tail[:, SUBLANES - 1]
    new_ffn = ffn_tail[:, SUBLANES - ffn_buf.shape[1]:]
    return out.reshape(nseq, seqlen, dm), (new_conv, new_ssm, new_shift, new_wkv, new_ffn)


_LAYER_WEIGHTS = ('norm1_g', 'w_in', 'ssm_conv_w', 'ssm_conv_b', 'ssm_dt_bias', 'ssm_a_log', 'ssm_d',
                  'ssm_norm_g', 'w_branch_a', 'rwkv_mu', 'rwkv_w0', 'rwkv_w_up', 'rwkv_a0', 'rwkv_a_up',
                  'rwkv_g_up', 'rwkv_k_k', 'rwkv_k_a', 'rwkv_r_k', 'rwkv_ln_w', 'rwkv_ln_b', 'w_branch_b',
                  'w_out', 'norm2_g', 'ffn_w_up', 'ffn_conv_w', 'ffn_conv_b', 'ffn_w_down')


def kernel(x_prompt, x_sample, state_ssm_conv, state_ssm, state_rwkv_shift, state_rwkv, state_ffn_conv,
           norm1_g, w_in, ssm_conv_w, ssm_conv_b, ssm_dt_bias, ssm_a_log, ssm_d, ssm_norm_g, w_branch_a,
           rwkv_mu, rwkv_w0, rwkv_w_up, rwkv_a0, rwkv_a_up, rwkv_g_up, rwkv_k_k, rwkv_k_a, rwkv_r_k,
           rwkv_ln_w, rwkv_ln_b, w_branch_b, w_out, norm2_g, ffn_w_up, ffn_conv_w, ffn_conv_b, ffn_w_down,
           final_g):
    stacked = dict(zip(_LAYER_WEIGHTS, (
        norm1_g, w_in, ssm_conv_w, ssm_conv_b, ssm_dt_bias, ssm_a_log, ssm_d, ssm_norm_g, w_branch_a,
        rwkv_mu, rwkv_w0, rwkv_w_up, rwkv_a0, rwkv_a_up, rwkv_g_up, rwkv_k_k, rwkv_k_a, rwkv_r_k,
        rwkv_ln_w, rwkv_ln_b, w_branch_b, w_out, norm2_g, ffn_w_up, ffn_conv_w, ffn_conv_b, ffn_w_down)))
    depth = w_in.shape[0]
    _, _, n_heads, _, _ = state_ssm.shape
    conv_dim = state_ssm_conv.shape[-1]
    d_inner = w_branch_a.shape[1]
    shift_dim = state_rwkv_shift.shape[-1]
    rwkv_heads = state_rwkv.shape[2]
    rwkv_dim = w_branch_b.shape[1]
    groups = (conv_dim - d_inner) // (2 * state_ssm.shape[-1])
    dims = (d_inner, conv_dim, n_heads, shift_dim, groups, rwkv_dim, rwkv_heads)

    xp, xs = x_prompt, x_sample
    bp = xp.shape[0]
    new_p = ([], [], [], [], [])
    new_s = ([], [], [], [], [])
    for i in range(depth):
        p = _prep_layer({name: a[i] for name, a in stacked.items()}, dims)
        p['final_g'] = final_g.reshape(1, -1).astype(F32)
        last = i == depth - 1
        xp, sp = _layer(
            xp,
            jnp.zeros((bp,) + state_ssm_conv.shape[2:], F32),
            jnp.zeros((bp,) + state_ssm.shape[2:], F32),
            jnp.zeros((bp,) + state_rwkv_shift.shape[2:], F32),
            jnp.zeros((bp,) + state_rwkv.shape[2:], F32),
            jnp.zeros((bp,) + state_ffn_conv.shape[2:], F32),
            p, last)
        xs, ss = _layer(xs, state_ssm_conv[i], state_ssm[i], state_rwkv_shift[i], state_rwkv[i],
                        state_ffn_conv[i], p, last)
        for j in range(5):
            new_p[j].append(sp[j])
            new_s[j].append(ss[j])
    return (xp, xs,
            jnp.stack(new_p[0]), jnp.stack(new_p[1]), jnp.stack(new_p[2]), jnp.stack(new_p[3]),
            jnp.stack(new_p[4]),
            jnp.stack(new_s[0]), jnp.stack(new_s[1]), jnp.stack(new_s[2]), jnp.stack(new_s[3]),
            jnp.stack(new_s[4]))
```

```python
import functools

import jax
import jax.numpy as jnp
from jax import lax
from jax.experimental import pallas as pl
from jax.experimental.pallas import tpu as pltpu

F32 = jnp.float32
BF16 = jnp.bfloat16

NORM_EPS = 1e-5
GN_EPS = 64e-5

LANES = 128
SUBLANES = 8
ROW_TILE = 256
SSM_ROW_TILE = 512
COL_STRIP = 512
SSD_CHUNK = 128
SSD_SUBCHUNKS = 4
WKV_TBLOCK = 64
WKV_PROMPT_SEQS = 4
WKV_SAMPLE_SEQS = 8
VMEM_LIMIT = 56 * 1024 * 1024


def _dot(a, b):
    return jnp.dot(a, b, preferred_element_type=F32)


def _split(x, n):
    parts = []
    rem = x
    for i in range(n):
        p = rem.astype(BF16)
        parts.append(p)
        if i + 1 < n:
            rem = rem - p.astype(F32)
    return parts


def _dot_split_lhs(x, m, n):
    acc = None
    for p in _split(x, n):
        d = _dot(p, m)
        acc = d if acc is None else acc + d
    return acc


def _dot_split_rhs(m, x, n):
    acc = None
    for p in _split(x, n):
        d = _dot(m, p)
        acc = d if acc is None else acc + d
    return acc


def _sigmoid(x):
    return 0.5 * jnp.tanh(0.5 * x) + 0.5


def _silu(x):
    hx = 0.5 * x
    return hx * jnp.tanh(hx) + hx


def _softplus(x):
    return jnp.maximum(x, 0.0) + jnp.log1p(jnp.exp(-jnp.abs(x)))


def _rmsnorm(x, g, eps):
    ms = jnp.mean(x * x, axis=-1, keepdims=True)
    return x * lax.rsqrt(ms + eps) * g


def _shift_rows(u, hist, seg, j):
    rows = u.shape[0]
    ru = pltpu.roll(u, j, 0)
    if seg == SUBLANES:
        rh = pltpu.roll(hist, (rows - SUBLANES + j) % rows, 0)
        pos = lax.broadcasted_iota(jnp.int32, u.shape, 0) % SUBLANES
        return jnp.where(pos < j, rh, ru)
    assert seg == rows
    rh = pltpu.roll(hist, j, 0)
    pos = lax.broadcasted_iota(jnp.int32, rh.shape, 0)
    top = jnp.where(pos < j, rh, ru[:SUBLANES])
    return jnp.concatenate([top, ru[SUBLANES:]], axis=0)


def _causal_conv(u, hist, seg, w_ref, b_ref):
    taps = w_ref.shape[0]
    acc = u * w_ref[taps - 1:taps, :] + b_ref[...]
    for j in range(1, taps):
        acc = acc + _shift_rows(u, hist, seg, j) * w_ref[taps - 1 - j:taps - j, :]
    return acc


def _load_hist(hist_ref, carry_ref, seg):
    if seg == SUBLANES:
        nseq, _, c = hist_ref.shape
        return hist_ref[...].reshape(nseq * SUBLANES, c)

    @pl.when(pl.program_id(1) == 0)
    def _():
        carry_ref[...] = hist_ref[0]

    return carry_ref[...]


def _store_tail(u, tail_ref, carry_ref, seg):
    if seg == SUBLANES:
        tail_ref[...] = u.reshape(tail_ref.shape)
    else:
        last = u[u.shape[0] - SUBLANES:]
        carry_ref[...] = last
        tail_ref[0] = last


def _ssm_in_body(x_ref, hist_ref, g1_ref, wz_ref, wx_ref, wdt_ref, cw_ref, cb_ref, dtb_ref,
                 z_ref, xc_ref, dt_ref, tail_ref, carry_ref, *, seg):
    h = _rmsnorm(x_ref[...], g1_ref[...], NORM_EPS).astype(BF16)
    hist = _load_hist(hist_ref, carry_ref, seg)
    taps = cw_ref.shape[0]
    for lo in range(0, wx_ref.shape[1], COL_STRIP):
        sl = slice(lo, lo + COL_STRIP)
        u = _dot(h, wx_ref[:, sl])
        hs = hist[:, sl]
        acc = u * cw_ref[taps - 1:taps, sl] + cb_ref[:, sl]
        for j in range(1, taps):
            acc = acc + _shift_rows(u, hs, seg, j) * cw_ref[taps - 1 - j:taps - j, sl]
        xc_ref[:, sl] = _silu(acc)
        if seg == SUBLANES:
            tail_ref[:, :, sl] = u.reshape(tail_ref.shape[0], SUBLANES, COL_STRIP)
        else:
            last = u[u.shape[0] - SUBLANES:]
            carry_ref[:, sl] = last
            tail_ref[0, :, sl] = last
    for lo in range(0, wz_ref.shape[1], COL_STRIP):
        sl = slice(lo, lo + COL_STRIP)
        z_ref[:, sl] = _dot(h, wz_ref[:, sl]).astype(z_ref.dtype)
    dt_ref[...] = _softplus(_dot(h, wdt_ref[...]) + dtb_ref[...])


def _rwkv_in_body(x_ref, hist_ref, g1_ref, wrw_ref, wg_ref, mu_ref, wlora_ref, w0_ref, a0_ref,
                  kk_ref, ka_ref, seg_ref, exp_ref,
                  r_out, lw_out, k_out, v_out, kkn_out, bb_out, g_out, gates_out, tail_ref,
                  carry_ref, *, seg, dim):
    h = _rmsnorm(x_ref[...], g1_ref[...], NORM_EPS).astype(BF16)
    hist = _load_hist(hist_ref, carry_ref, seg)
    drw = wrw_ref.shape[1]

    def mixed(lo, hi):
        sl = slice(lo, hi)
        u = _dot(h, wrw_ref[:, sl])
        prev = _shift_rows(u, hist[:, sl], seg, 1)
        if seg == SUBLANES:
            tail_ref[:, :, sl] = u.reshape(tail_ref.shape[0], SUBLANES, hi - lo)
        else:
            last = u[u.shape[0] - SUBLANES:]
            carry_ref[:, sl] = last
            tail_ref[0, :, sl] = last
        return u + (prev - u) * mu_ref[:, sl]

    low = mixed(3 * dim, drw)
    lane = lax.broadcasted_iota(jnp.int32, low.shape, 1)
    lo_w = low.shape[1] // 4
    act = jnp.where(lane < lo_w, jnp.tanh(low), jnp.where(lane < 2 * lo_w, low, _sigmoid(low))).astype(BF16)
    r_out[...] = mixed(0, dim)
    wlog = -_softplus(-(w0_ref[...] + _dot(act, wlora_ref[:, 0:dim]))) - 0.5
    lw_out[...] = -jnp.exp(wlog)
    v_out[...] = mixed(2 * dim, 3 * dim)
    a = _sigmoid(a0_ref[...] + _dot(act, wlora_ref[:, dim:2 * dim]))
    k = mixed(dim, 2 * dim)
    kkr = k * kk_ref[...]
    ss = _dot_split_lhs(kkr * kkr, seg_ref[...], 2)
    inv = 1.0 / jnp.maximum(jnp.sqrt(ss), 1e-12)
    kkn = kkr * _dot_split_lhs(inv, exp_ref[...], 2)
    k_out[...] = k * (1.0 + (a - 1.0) * ka_ref[...])
    kkn_out[...] = kkn
    bb_out[...] = kkn * a
    g_out[...] = _dot(act, wlora_ref[:, 2 * dim:3 * dim]).astype(g_out.dtype)
    for lo in range(0, wg_ref.shape[1], COL_STRIP):
        sl = slice(lo, lo + COL_STRIP)
        gates_out[:, sl] = _sigmoid(_dot(h, wg_ref[:, sl])).astype(gates_out.dtype)


def _ssd_body(xm_ref, b_ref, c_ref, dt_ref, z_ref, st_ref, alog_ref, dexp_ref, ng_ref,
              tri_ref, e_ref, e2_ref, y_ref, so_ref, ht_ref, *, rows, heads, groups, seqs, subs):
    step = pl.program_id(2)
    n_steps = pl.num_programs(2)
    R = SSD_CHUNK
    width = xm_ref.shape[1] // groups
    state_n = b_ref.shape[1] // groups
    p_dim = width // heads
    gs = range(groups)
    units = [(sc, g) for sc in range(subs) for g in gs]
    us = range(len(units))
    steps = R // seqs

    def blk(ref, u, n):
        sc, g = units[u]
        v = ref[sc * R:(sc + 1) * R, g * n:(g + 1) * n] if subs > 1 else ref[:, g * n:(g + 1) * n]
        if v.shape[0] == R:
            return v
        return jnp.concatenate([v, jnp.zeros((R - v.shape[0], v.shape[1]), v.dtype)], axis=0)

    def gcols(ref, u, n):
        g = units[u][1]
        return ref[:, g * n:(g + 1) * n]

    row = lax.broadcasted_iota(jnp.int32, (R, R), 0)
    col = lax.broadcasted_iota(jnp.int32, (R, R), 1)
    lane = lax.broadcasted_iota(jnp.int32, (R, LANES), 1)
    expand = e_ref[...]
    expand2 = e2_ref[...]
    if seqs == 1:
        causal = row >= col
        tri = tri_ref[...]

        @pl.when(step == 0)
        def _():
            for g in gs:
                ht_ref[g] = st_ref[0, g * heads:(g + 1) * heads].reshape(width, state_n).T
    else:
        same = (row // steps) == (col // steps)
        causal = same & (row >= col)
        tri = causal.astype(BF16)

    xm = [blk(xm_ref, u, width) for u in us]
    bm = [blk(b_ref, u, state_n) for u in us]
    cm = [blk(c_ref, u, state_n) for u in us]
    cmb = [cm[u].astype(BF16) for u in us]
    dt = [blk(dt_ref, u, LANES) for u in us]
    a = [dt[u] * (-jnp.exp(gcols(alog_ref, u, LANES))) for u in us]
    cs = [_dot_split_rhs(tri, a[u], 3) for u in us]
    dt_e = [_dot_split_lhs(dt[u], expand, 2) for u in us]
    cs_e = [_dot_split_lhs(cs[u], expand, 3) for u in us]
    if seqs > 1:
        cs_e2 = [_dot_split_lhs(cs[u], expand2, 3) for u in us]
    bt = [bm[u].T for u in us]
    btb = [bt[u].astype(BF16) for u in us]
    cs_t = [cs[u].T for u in us]
    if seqs == 1:
        end_e = [cs_e[u][R - 1:R, :] for u in us]
    else:
        tot = [_dot_split_rhs(same.astype(BF16), a[u], 3) for u in us]
        end_e = [_dot_split_lhs(tot[u], expand, 3) for u in us]
    xdt = [xm[u] * dt_e[u] for u in us]
    xs = [(xdt[u] * jnp.exp(end_e[u] - cs_e[u])).astype(BF16) for u in us]
    cb = [_dot(cmb[u], btb[u]) for u in us]
    if seqs == 1:
        st_new = [_dot(btb[u], xs[u]) for u in us]
        ht = [ht_ref[g] for g in gs]
        y_off = []
        for u, (sc, g) in enumerate(units):
            y_off.append(_dot(cmb[u], ht[g].astype(BF16)))
            ht[g] = ht[g] * jnp.exp(end_e[u]) + st_new[u]
        for g in gs:
            ht_ref[g] = ht[g]
    else:
        y_off = []
        for u, (sc, g) in enumerate(units):
            tiles = []
            for s in range(seqs):
                r0 = s * steps
                ht0 = st_ref[s, g * heads:(g + 1) * heads].reshape(width, state_n).T
                tiles.append(_dot(cm[u][r0:r0 + steps].astype(BF16), ht0.astype(BF16)))
                bts = jnp.where((col // steps) == s, bt[u], 0.0).astype(BF16)
                ht1 = ht0 * jnp.exp(end_e[u][r0:r0 + 1]) + _dot(bts, xs[u])
                so_ref[s, g * heads:(g + 1) * heads] = ht1.T.reshape((heads,) + so_ref.shape[2:])
            y_off.append(jnp.concatenate(tiles, axis=0))
    ys = [[] for _ in us]
    for j in range(heads // 2):
        lhs, rhs = [], []
        for u in us:
            ms = []
            for hh in (2 * j, 2 * j + 1):
                if seqs == 1:
                    cs_col = jnp.broadcast_to(cs[u][:, hh:hh + 1], (R, R))
                else:
                    cs_col = cs_e2[u][:, hh * R:(hh + 1) * R]
                seg_ = cs_col - cs_t[u][hh:hh + 1, :]
                ms.append(jnp.where(causal, cb[u] * jnp.exp(jnp.where(causal, seg_, 0.0)), 0.0).astype(BF16))
            lhs.append(jnp.concatenate(ms, axis=1))
            xp = xdt[u][:, j * LANES:(j + 1) * LANES]
            rhs.append(jnp.concatenate([jnp.where(lane < p_dim, xp, 0.0), jnp.where(lane >= p_dim, xp, 0.0)],
                                       axis=0).astype(BF16))
        for u in us:
            ys[u].append(_dot(lhs[u], rhs[u]))
    for u, (sc, g) in enumerate(units):
        y = jnp.concatenate(ys[u], axis=1) + y_off[u] * jnp.exp(cs_e[u])
        y = y + gcols(dexp_ref, u, width) * xm[u]
        yz = y * _silu(blk(z_ref, u, width).astype(F32))
        yn = yz * lax.rsqrt(jnp.mean(yz * yz, axis=-1, keepdims=True) + NORM_EPS) * gcols(ng_ref, u, width)
        if subs > 1:
            y_ref[sc * R:(sc + 1) * R, g * width:(g + 1) * width] = yn.astype(y_ref.dtype)
        else:
            y_ref[:, g * width:(g + 1) * width] = yn[:rows].astype(y_ref.dtype)

    if seqs == 1:
        @pl.when(step == n_steps - 1)
        def _():
            for g in gs:
                so_ref[0, g * heads:(g + 1) * heads] = ht_ref[g].T.reshape((heads,) + so_ref.shape[2:])


def _wkv_body(r_ref, lw_ref, k_ref, v_ref, kk_ref, bb_ref, g_ref, s0_ref, rk_ref, lnw_ref, lnb_ref,
              bo_ref, tri_ref, o_ref, so_ref, s_ref, *, head_dim, single):
    nseq, steps, dim = r_ref.shape
    npair = dim // LANES
    nchunk = steps // SUBLANES
    block_ones = bo_ref[...]
    tri = tri_ref[...]

    def load_state(si, p):
        x = s0_ref[si, 2 * p:2 * p + 2].reshape(2 * head_dim, head_dim)
        xp = jnp.concatenate([x, jnp.zeros((2 * head_dim, LANES - head_dim), F32)], axis=1)
        return xp.T[:head_dim]

    def store_state(n, s):
        sp = jnp.concatenate([s, jnp.zeros((LANES - head_dim, LANES), F32)], axis=0)
        back = sp.T[:, :head_dim]
        so_ref[n // npair, 2 * (n % npair):2 * (n % npair) + 2] = back.reshape(2, head_dim, head_dim)

    if not single:
        @pl.when(pl.program_id(1) == 0)
        def _():
            for si in range(nseq):
                for p in range(npair):
                    s_ref[si * npair + p] = load_state(si, p)

    lane8 = lax.broadcasted_iota(jnp.int32, (SUBLANES, LANES), 1)
    row8 = lax.broadcasted_iota(jnp.int32, (SUBLANES, LANES), 0)
    head0 = lane8 < head_dim
    rowi = lax.broadcasted_iota(jnp.int32, (head_dim, LANES), 0)
    lanei = lax.broadcasted_iota(jnp.int32, (head_dim, LANES), 1)
    diag = (rowi == lanei % head_dim).astype(F32)

    def other_head(a):
        return pltpu.roll(a, head_dim, 1)

    def rows(a, c):
        return a[c * SUBLANES:(c + 1) * SUBLANES]

    def bc(tile, i):
        return jnp.broadcast_to(tile[i:i + 1], (SUBLANES, LANES))

    chains = []
    for si in range(nseq):
        cum_seq = _dot_split_rhs(tri, lw_ref[si], 3)
        for p in range(npair):
            sl = slice(p * LANES, (p + 1) * LANES)
            lw = lw_ref[si, :, sl]
            cum = cum_seq[:, sl]
            p_in = jnp.exp(cum)
            p_inv = jnp.exp(-cum)
            at = kk_ref[si, :, sl] * jnp.exp(cum - lw)
            rt = r_ref[si, :, sl] * p_in
            bt = bb_ref[si, :, sl] * p_inv
            kt = k_ref[si, :, sl] * p_inv
            at_o, rt_o, bt_o, kt_o = other_head(at), other_head(rt), other_head(bt), other_head(kt)
            x4 = jnp.concatenate([rows(a, c) for c in range(nchunk) for a in (bt, bt_o, kt, kt_o)], axis=0)
            if x4.shape[0] < LANES:
                x4 = jnp.concatenate([x4, jnp.zeros((LANES - x4.shape[0], LANES), F32)], axis=0)
            xt = x4.T[:head_dim].astype(BF16)
            chains.append(dict(si=si, sl=sl, at=at, rt=rt, bt=bt, kt=kt, at_o=at_o, rt_o=rt_o, xt=xt,
                               v=v_ref[si, :, sl], p_end=p_in[steps - 1:steps],
                               s=load_state(si, p) if single else s_ref[si * npair + p],
                               ys=[]))

    for c in range(nchunk):
        for ch in chains:
            at_c, rt_c, bt_c, kt_c = rows(ch['at'], c), rows(ch['rt'], c), rows(ch['bt'], c), rows(ch['kt'], c)
            tiles = []
            for i in range(SUBLANES):
                am = jnp.where(row8 > i, at_c, 0.0)
                rm = jnp.where(row8 >= i, rt_c, 0.0)
                bi, ki = bc(bt_c, i), bc(kt_c, i)
                tiles += [am * bi, am * ki, rm * bi, rm * ki]
            coef = _dot(jnp.concatenate(tiles, axis=0).astype(BF16), block_ones)
            ch['coef'] = [rows(coef, n) for n in range(4 * SUBLANES)]
            v_c = rows(ch['v'], c)
            va = None
            yv = None
            for i in range(SUBLANES):
                vi = bc(v_c, i)
                t_ak = ch['coef'][4 * i + 1] * vi
                t_rk = ch['coef'][4 * i + 3] * vi
                va = t_ak if va is None else va + t_ak
                yv = t_rk if yv is None else yv + t_rk
            ch['va'], ch['yv'], ch['v_c'] = va, yv, v_c
            lhs = jnp.concatenate([at_c[:, :head_dim], rows(ch['at_o'], c)[:, :head_dim],
                                   rt_c[:, :head_dim], rows(ch['rt_o'], c)[:, :head_dim]], axis=0)
            ch['lhs'] = lhs.astype(BF16)
        for ch in chains:
            ch['g'] = _dot(ch['lhs'], ch['s'].astype(BF16))
        for ch in chains:
            g = ch['g']
            g_a = jnp.where(head0, rows(g, 0), rows(g, 1))
            g_r = jnp.where(head0, rows(g, 2), rows(g, 3))
            sa = g_a + ch['va']
            y = g_r + ch['yv']
            for i in range(SUBLANES):
                sai = bc(sa, i)
                if i + 1 < SUBLANES:
                    sa = sa - ch['coef'][4 * i] * sai
                y = y - ch['coef'][4 * i + 2] * sai
            ch['ys'].append(y)
            v_c = ch['v_c']
            wd = jnp.concatenate([jnp.where(head0, -sa, 0.0), jnp.where(head0, 0.0, -sa),
                                  jnp.where(head0, v_c, 0.0), jnp.where(head0, 0.0, v_c)], axis=0)
            ch['wd'] = wd.astype(BF16)
        for ch in chains:
            cols = ch['xt'][:, c * 4 * SUBLANES:(c + 1) * 4 * SUBLANES]
            ch['s'] = ch['s'] + _dot(cols, ch['wd'])

    inv_n = 1.0 / head_dim
    cat = lambda key: jnp.concatenate([ch[key] for ch in chains], axis=0)
    part = lambda a, n, m: a[n * m:(n + 1) * m]
    for ch in chains:
        ch['pd'] = diag * ch['p_end']
        ch['y'] = jnp.concatenate(ch['ys'], axis=0)
        ch['rk'] = r_ref[ch['si'], :, ch['sl']] * k_ref[ch['si'], :, ch['sl']] * rk_ref[:, ch['sl']]
    p_col = _dot_split_lhs(cat('pd'), block_ones, 3)
    y_all = cat('y')
    mu = _dot_split_lhs(y_all, block_ones, 2) * inv_n
    bonus = _dot_split_lhs(cat('rk'), block_ones, 2)
    d_all = y_all - mu
    var = _dot_split_lhs(d_all * d_all, block_ones, 2) * inv_n
    for n, ch in enumerate(chains):
        si, sl = ch['si'], ch['sl']
        s_new = ch['s'] * part(p_col, n, head_dim)
        if single:
            store_state(n, s_new)
        else:
            s_ref[n] = s_new
            ch['s'] = s_new
        yn = part(d_all, n, steps) * lax.rsqrt(part(var, n, steps) + GN_EPS) * lnw_ref[:, sl] + lnb_ref[:, sl]
        out = (yn + part(bonus, n, steps) * ch['v']) * g_ref[si, :, sl].astype(F32)
        o_ref[si, :, sl] = out.astype(o_ref.dtype)

    if not single:
        @pl.when(pl.program_id(1) == pl.num_programs(1) - 1)
        def _():
            for n, ch in enumerate(chains):
                store_state(n, ch['s'])


def _post_body(x_ref, ya_ref, yb_ref, gt_ref, hist_ref, wa_ref, wb_ref, wo_ref, g2_ref, wup_ref,
               cw_ref, cb_ref, wdn_ref, gf_ref, o_ref, tail_ref, carry_ref, *, seg, final_norm):
    dm = x_ref.shape[1]
    ua = _dot(ya_ref[...].astype(BF16), wa_ref[...])
    ub = _dot(yb_ref[...].astype(BF16), wb_ref[...])
    gates = gt_ref[...].astype(F32)
    m = (gates[:, :dm] * ua + gates[:, dm:] * ub).astype(BF16)
    x1 = x_ref[...] + _dot(m, wo_ref[...])
    h2 = _rmsnorm(x1, g2_ref[...], NORM_EPS).astype(BF16)
    up = _dot(h2, wup_ref[...])
    dff = up.shape[1] // 2
    ug = up[:, :dff]
    hist = _load_hist(hist_ref, carry_ref, seg)
    ugc = _causal_conv(ug, hist, seg, cw_ref, cb_ref)
    _store_tail(ug, tail_ref, carry_ref, seg)
    act = (_silu(ugc) * up[:, dff:]).astype(BF16)
    x2 = x1 + _dot(act, wdn_ref[...])
    if final_norm:
        x2 = _rmsnorm(x2, gf_ref[...], NORM_EPS)
    o_ref[...] = x2


def _const_spec(shape):
    nd = len(shape)
    return pl.BlockSpec(shape, lambda *_: (0,) * nd, pipeline_mode=pl.Buffered(1))


def _params(sem):
    return pltpu.CompilerParams(dimension_semantics=sem, vmem_limit_bytes=VMEM_LIMIT)


def _token_tiling(nseq, seqlen, tile=ROW_TILE):
    if seqlen == SUBLANES:
        per = tile // SUBLANES
        assert nseq % per == 0
        return per, SUBLANES, (nseq // per, 1)
    assert seqlen % tile == 0
    return 1, tile, (nseq, seqlen // tile)


def _row_spec(cols, lt, tile=ROW_TILE):
    return pl.BlockSpec((tile, cols), lambda i, l: (i * lt + l, 0))


def _hist_spec(per, cols):
    return pl.BlockSpec((per, SUBLANES, cols), lambda i, l: (i, 0, 0))


def _ssm_in(x, hist, p, nseq, seqlen, act_dtype):
    tile = SSM_ROW_TILE if seqlen % SSM_ROW_TILE == 0 else ROW_TILE
    per, seg, grid = _token_tiling(nseq, seqlen, tile)
    row_spec = functools.partial(_row_spec, tile=tile)
    t, dm = x.shape
    dz = p['wz'].shape[1]
    dc = p['wx'].shape[1]
    dd = p['wdt'].shape[1]
    lt = grid[1]
    return pl.pallas_call(
        functools.partial(_ssm_in_body, seg=seg),
        grid=grid,
        in_specs=[row_spec(dm, lt), _hist_spec(per, dc), _const_spec((1, dm)),
                  _const_spec(p['wz'].shape), _const_spec(p['wx'].shape), _const_spec(p['wdt'].shape),
                  _const_spec(p['conv_w'].shape), _const_spec((1, dc)), _const_spec((1, dd))],
        out_specs=[row_spec(dz, lt), row_spec(dc, lt), row_spec(dd, lt), _hist_spec(per, dc)],
        out_shape=[jax.ShapeDtypeStruct((t, dz), act_dtype), jax.ShapeDtypeStruct((t, dc), F32),
                   jax.ShapeDtypeStruct((t, dd), F32), jax.ShapeDtypeStruct((nseq, SUBLANES, dc), F32)],
        scratch_shapes=[pltpu.VMEM((SUBLANES, dc), F32)],
        compiler_params=_params(("arbitrary", "arbitrary")),
        name="ssm_in",
    )(x, hist, p['norm1_g'], p['wz'], p['wx'], p['wdt'], p['conv_w'], p['conv_b'], p['dt_bias'])


def _rwkv_in(x, hist, p, nseq, seqlen, act_dtype):
    per, seg, grid = _token_tiling(nseq, seqlen)
    t, dm = x.shape
    dim = p['w0'].shape[1]
    drw = p['wrw'].shape[1]
    dg = p['wg'].shape[1]
    lt = grid[1]
    f32_out = jax.ShapeDtypeStruct((t, dim), F32)
    return pl.pallas_call(
        functools.partial(_rwkv_in_body, seg=seg, dim=dim),
        grid=grid,
        in_specs=[_row_spec(dm, lt), _hist_spec(per, drw), _const_spec((1, dm)),
                  _const_spec(p['wrw'].shape), _const_spec(p['wg'].shape), _const_spec((1, drw)),
                  _const_spec(p['wlora'].shape), _const_spec((1, dim)), _const_spec((1, dim)),
                  _const_spec((1, dim)), _const_spec((1, dim)),
                  _const_spec(p['head_sum'].shape), _const_spec(p['head_expand'].shape)],
        out_specs=[_row_spec(dim, lt)] * 7 + [_row_spec(dg, lt), _hist_spec(per, drw)],
        out_shape=[f32_out] * 6 + [jax.ShapeDtypeStruct((t, dim), act_dtype),
                                   jax.ShapeDtypeStruct((t, dg), BF16),
                                   jax.ShapeDtypeStruct((nseq, SUBLANES, drw), F32)],
        scratch_shapes=[pltpu.VMEM((SUBLANES, drw), F32)],
        compiler_params=_params(("arbitrary", "arbitrary")),
        name="rwkv_in",
    )(x, hist, p['norm1_g'], p['wrw'], p['wg'], p['mu'], p['wlora'], p['w0'], p['a0'],
      p['k_k'], p['k_a'], p['head_sum'], p['head_expand'])


def _ssd(xc, dt, z, state, p, nseq, seqlen, act_dtype):
    t = xc.shape[0]
    _, n_heads, p_dim, state_n = state.shape
    groups = dt.shape[1] // LANES
    heads = n_heads // groups
    width = heads * p_dim
    d_inner = groups * width
    assert state_n == LANES and width % LANES == 0 and p_dim * 2 == LANES
    subs = 1
    if seqlen % SSD_CHUNK == 0:
        subs = SSD_SUBCHUNKS if seqlen % (SSD_SUBCHUNKS * SSD_CHUNK) == 0 else 1
        seqs, gps, rows, nc = 1, groups, subs * SSD_CHUNK, seqlen // (subs * SSD_CHUNK)
    elif seqlen == SUBLANES and nseq % (SSD_CHUNK // SUBLANES) == 0:
        seqs, gps, rows, nc = SSD_CHUNK // SUBLANES, 1, SSD_CHUNK, 1
    else:
        assert seqlen == SUBLANES
        seqs, gps, rows, nc = 1, groups, SUBLANES, 1
    gn = gps * state_n
    b_blk = d_inner // gn
    assert d_inner % gn == 0
    row_map = lambda b, g, c: (b * nc + c, g)
    st_spec = pl.BlockSpec((seqs, gps * heads, p_dim, state_n), lambda b, g, c: (b, g, 0, 0))
    return pl.pallas_call(
        functools.partial(_ssd_body, rows=rows, heads=heads, groups=gps, seqs=seqs, subs=subs),
        grid=(nseq // seqs, groups // gps, nc),
        in_specs=[pl.BlockSpec((rows, gps * width), row_map),
                  pl.BlockSpec((rows, gn), lambda b, g, c: (b * nc + c, b_blk + g)),
                  pl.BlockSpec((rows, gn), lambda b, g, c: (b * nc + c, b_blk + groups // gps + g)),
                  pl.BlockSpec((rows, gps * LANES), row_map),
                  pl.BlockSpec((rows, gps * width), row_map),
                  st_spec,
                  pl.BlockSpec((1, gps * LANES), lambda b, g, c: (0, g)),
                  pl.BlockSpec((1, gps * width), lambda b, g, c: (0, g)),
                  pl.BlockSpec((1, gps * width), lambda b, g, c: (0, g)),
                  _const_spec(p['tri'].shape), _const_spec(p['ssd_expand'].shape),
                  _const_spec(p['ssd_expand2'].shape)],
        out_specs=[pl.BlockSpec((rows, gps * width), row_map), st_spec],
        out_shape=[jax.ShapeDtypeStruct((t, d_inner), act_dtype),
                   jax.ShapeDtypeStruct(state.shape, F32)],
        scratch_shapes=[pltpu.VMEM((gps, state_n, width), F32)],
        compiler_params=_params(("arbitrary", "arbitrary", "arbitrary")),
        name="ssd",
    )(xc, xc, xc, dt, z, state, p['a_log'], p['d_exp'], p['ssm_norm_g'],
      p['tri'], p['ssd_expand'], p['ssd_expand2'])


def _wkv(r, lw, k, v, kk, bb, g, state, p, nseq, seqlen, act_dtype):
    dim = r.shape[1]
    _, n_heads, head_dim, _ = state.shape
    npair = dim // LANES
    assert n_heads == 2 * npair and 2 * head_dim == LANES
    if seqlen == SUBLANES:
        per, steps = WKV_SAMPLE_SEQS, SUBLANES
    else:
        per, steps = (WKV_PROMPT_SEQS if nseq % WKV_PROMPT_SEQS == 0 else 1), WKV_TBLOCK
    assert nseq % per == 0 and seqlen % steps == 0
    grid = (nseq // per, seqlen // steps)
    seq_spec = pl.BlockSpec((per, steps, dim), lambda i, tb: (i, tb, 0))
    st_spec = pl.BlockSpec((per, n_heads, head_dim, head_dim), lambda i, tb: (i, 0, 0, 0))
    as3 = lambda a: a.reshape(nseq, seqlen, dim)
    out, s_out = pl.pallas_call(
        functools.partial(_wkv_body, head_dim=head_dim, single=(seqlen == steps)),
        grid=grid,
        in_specs=[seq_spec] * 7 + [st_spec, _const_spec((1, dim)), _const_spec((1, dim)),
                                   _const_spec((1, dim)), _const_spec((LANES, LANES)),
                                   _const_spec((steps, steps))],
        out_specs=[seq_spec, st_spec],
        out_shape=[jax.ShapeDtypeStruct((nseq, seqlen, dim), act_dtype),
                   jax.ShapeDtypeStruct(state.shape, F32)],
        scratch_shapes=[pltpu.VMEM((per * npair, head_dim, LANES), F32)],
        compiler_params=_params(("arbitrary", "arbitrary")),
        name="wkv",
    )(as3(r), as3(lw), as3(k), as3(v), as3(kk), as3(bb), as3(g), state,
      p['r_k'], p['ln_w'], p['ln_b'], p['block_ones'], p['tri'][:steps, :steps])
    return out.reshape(nseq * seqlen, dim), s_out


def _post(x, ya, yb, gates, hist, p, nseq, seqlen, final_norm):
    per, seg, grid = _token_tiling(nseq, seqlen)
    t, dm = x.shape
    dff = p['wdn'].shape[0]
    lt = grid[1]
    return pl.pallas_call(
        functools.partial(_post_body, seg=seg, final_norm=final_norm),
        grid=grid,
        in_specs=[_row_spec(dm, lt), _row_spec(ya.shape[1], lt), _row_spec(yb.shape[1], lt),
                  _row_spec(gates.shape[1], lt), _hist_spec(per, dff),
                  _const_spec(p['wa'].shape), _const_spec(p['wb'].shape), _const_spec(p['wo'].shape),
                  _const_spec((1, dm)), _const_spec(p['wup'].shape), _const_spec(p['ffn_conv_w'].shape),
                  _const_spec((1, dff)), _const_spec(p['wdn'].shape), _const_spec((1, dm))],
        out_specs=[_row_spec(dm, lt), _hist_spec(per, dff)],
        out_shape=[jax.ShapeDtypeStruct((t, dm), F32), jax.ShapeDtypeStruct((nseq, SUBLANES, dff), F32)],
        scratch_shapes=[pltpu.VMEM((SUBLANES, dff), F32)],
        compiler_params=_params(("arbitrary", "arbitrary")),
        name="post",
    )(x, ya, yb, gates, hist, p['wa'], p['wb'], p['wo'], p['norm2_g'], p['wup'], p['ffn_conv_w'],
      p['ffn_conv_b'], p['wdn'], p['final_g'])


def _prep_layer(w, dims):
    d_inner, conv_dim, n_heads, shift_dim, groups, rwkv_dim, rwkv_heads = dims
    row = lambda a: a.reshape(1, -1).astype(F32)
    w_in = w['w_in']
    o1 = d_inner
    o2 = o1 + conv_dim
    o3 = o2 + n_heads
    o4 = o3 + shift_dim
    hpg = n_heads // groups

    def per_group(a):
        lead = a.shape[:-1]
        a = a.reshape(lead + (groups, hpg))
        a = jnp.pad(a, [(0, 0)] * len(lead) + [(0, 0), (0, LANES - hpg)])
        return a.reshape(lead + (groups * LANES,))

    p_dim = d_inner // n_heads
    head_dim = rwkv_dim // rwkv_heads
    lora_w = w['rwkv_w_up'].shape[0]
    lora_a = w['rwkv_a_up'].shape[0]
    lora_g = w['rwkv_g_up'].shape[0]
    assert lora_w == lora_a and lora_g == 2 * lora_w
    wlora = jnp.zeros((lora_w + lora_a + lora_g, 3 * rwkv_dim), F32)
    wlora = wlora.at[:lora_w, :rwkv_dim].set(w['rwkv_w_up'])
    wlora = wlora.at[lora_w:lora_w + lora_a, rwkv_dim:2 * rwkv_dim].set(w['rwkv_a_up'])
    wlora = wlora.at[lora_w + lora_a:, 2 * rwkv_dim:].set(w['rwkv_g_up'])

    ch = jnp.arange(rwkv_dim) // head_dim
    head_sum = (ch[:, None] == jnp.arange(LANES)[None, :]).astype(BF16)
    li = jnp.arange(LANES)
    width = hpg * p_dim
    ssd_expand = (li[:, None] == (jnp.arange(width) // p_dim)[None, :]).astype(BF16)
    ssd_expand2 = (li[:, None] == (jnp.arange(hpg * SSD_CHUNK) // SSD_CHUNK)[None, :]).astype(BF16)
    ci = jnp.arange(SSD_CHUNK)
    tri = (ci[:, None] >= ci[None, :]).astype(BF16)
    block_ones = ((li[:, None] // head_dim) == (li[None, :] // head_dim)).astype(BF16)

    return {
        'norm1_g': row(w['norm1_g']),
        'wz': w_in[:, :o1].astype(BF16),
        'wx': w_in[:, o1:o2].astype(BF16),
        'wdt': per_group(w_in[:, o2:o3]).astype(BF16),
        'wrw': w_in[:, o3:o4].astype(BF16),
        'wg': w_in[:, o4:].astype(BF16),
        'conv_w': w['ssm_conv_w'].astype(F32),
        'conv_b': row(w['ssm_conv_b']),
        'dt_bias': row(per_group(w['ssm_dt_bias'])),
        'a_log': row(per_group(w['ssm_a_log'])),
        'd_exp': row(jnp.repeat(w['ssm_d'], p_dim)),
        'ssm_norm_g': row(w['ssm_norm_g']),
        'wa': w['w_branch_a'].astype(BF16),
        'mu': row(w['rwkv_mu']),
        'wlora': wlora.astype(BF16),
        'w0': row(w['rwkv_w0']),
        'a0': row(w['rwkv_a0']),
        'k_k': row(w['rwkv_k_k']),
        'k_a': row(w['rwkv_k_a']),
        'r_k': row(w['rwkv_r_k']),
        'ln_w': row(w['rwkv_ln_w']),
        'ln_b': row(w['rwkv_ln_b']),
        'wb': w['w_branch_b'].astype(BF16),
        'wo': w['w_out'].astype(BF16),
        'norm2_g': row(w['norm2_g']),
        'wup': w['ffn_w_up'].astype(BF16),
        'ffn_conv_w': w['ffn_conv_w'].astype(F32),
        'ffn_conv_b': row(w['ffn_conv_b']),
        'wdn': w['ffn_w_down'].astype(BF16),
        'head_sum': head_sum,
        'head_expand': head_sum.T,
        'ssd_expand': ssd_expand,
        'ssd_expand2': ssd_expand2,
        'tri': tri,
        'block_ones': block_ones,
    }


def _hist8(state_rows):
    k = state_rows.shape[1]
    return jnp.pad(state_rows.astype(F32), ((0, 0), (SUBLANES - k, 0), (0, 0)))


def _layer(x, conv_buf, ssm_state, shift_buf, wkv_state, ffn_buf, p, final_norm):
    nseq, seqlen, dm = x.shape
    act_dtype = BF16 if seqlen % 16 == 0 else F32
    xf = x.reshape(nseq * seqlen, dm)
    z, xc, dt, conv_tail = _ssm_in(xf, _hist8(conv_buf), p, nseq, seqlen, act_dtype)
    r, lw, k, v, kk, bb, g, gates, shift_tail = _rwkv_in(xf, _hist8(shift_buf[:, None]), p, nseq, seqlen,
                                                         act_dtype)
    ya, new_ssm = _ssd(xc, dt, z, ssm_state.astype(F32), p, nseq, seqlen, act_dtype)
    yb, new_wkv = _wkv(r, lw, k, v, kk, bb, g, wkv_state.astype(F32), p, nseq, seqlen, act_dtype)
    out, ffn_tail = _post(xf, ya, yb, gates, _hist8(ffn_buf), p, nseq, seqlen, final_norm)
    new_conv = conv_tail[:, SUBLANES - conv_buf.shape[1]:]
    new_shift = shift_tail[:, SUBLANES - 1]
    new_ffn = ffn_tail[:, SUBLANES - ffn_buf.shape[1]:]
    return out.reshape(nseq, seqlen, dm), (new_conv, new_ssm, new_shift, new_wkv, new_ffn)


_LAYER_WEIGHTS = ('norm1_g', 'w_in', 'ssm_conv_w', 'ssm_conv_b', 'ssm_dt_bias', 'ssm_a_log', 'ssm_d',
                  'ssm_norm_g', 'w_branch_a', 'rwkv_mu', 'rwkv_w0', 'rwkv_w_up', 'rwkv_a0', 'rwkv_a_up',
                  'rwkv_g_up', 'rwkv_k_k', 'rwkv_k_a', 'rwkv_r_k', 'rwkv_ln_w', 'rwkv_ln_b', 'w_branch_b',
                  'w_out', 'norm2_g', 'ffn_w_up', 'ffn_conv_w', 'ffn_conv_b', 'ffn_w_down')


def kernel(x_prompt, x_sample, state_ssm_conv, state_ssm, state_rwkv_shift, state_rwkv, state_ffn_conv,
           norm1_g, w_in, ssm_conv_w, ssm_conv_b, ssm_dt_bias, ssm_a_log, ssm_d, ssm_norm_g, w_branch_a,
           rwkv_mu, rwkv_w0, rwkv_w_up, rwkv_a0, rwkv_a_up, rwkv_g_up, rwkv_k_k, rwkv_k_a, rwkv_r_k,
           rwkv_ln_w, rwkv_ln_b, w_branch_b, w_out, norm2_g, ffn_w_up, ffn_conv_w, ffn_conv_b, ffn_w_down,
           final_g):
    stacked = dict(zip(_LAYER_WEIGHTS, (
        norm1_g, w_in, ssm_conv_w, ssm_conv_b, ssm_dt_bias, ssm_a_log, ssm_d, ssm_norm_g, w_branch_a,
        rwkv_mu, rwkv_w0, rwkv_w_up, rwkv_a0, rwkv_a_up, rwkv_g_up, rwkv_k_k, rwkv_k_a, rwkv_r_k,
        rwkv_ln_w, rwkv_ln_b, w_branch_b, w_out, norm2_g, ffn_w_up, ffn_conv_w, ffn_conv_b, ffn_w_down)))
    depth = w_in.shape[0]
    _, _, n_heads, _, _ = state_ssm.shape
    conv_dim = state_ssm_conv.shape[-1]
    d_inner = w_branch_a.shape[1]
    shift_dim = state_rwkv_shift.shape[-1]
    rwkv_heads = state_rwkv.shape[2]
    rwkv_dim = w_branch_b.shape[1]
    groups = (conv_dim - d_inner) // (2 * state_ssm.shape[-1])
    dims = (d_inner, conv_dim, n_heads, shift_dim, groups, rwkv_dim, rwkv_heads)

    xp, xs = x_prompt, x_sample
    bp = xp.shape[0]
    new_p = ([], [], [], [], [])
    new_s = ([], [], [], [], [])
    for i in range(depth):
        p = _prep_layer({name: a[i] for name, a in stacked.items()}, dims)
        p['final_g'] = final_g.reshape(1, -1).astype(F32)
        last = i == depth - 1
        xp, sp = _layer(
            xp,
            jnp.zeros((bp,) + state_ssm_conv.shape[2:], F32),
            jnp.zeros((bp,) + state_ssm.shape[2:], F32),
            jnp.zeros((bp,) + state_rwkv_shift.shape[2:], F32),
            jnp.zeros((bp,) + state_rwkv.shape[2:], F32),
            jnp.zeros((bp,) + state_ffn_conv.shape[2:], F32),
            p, last)
        xs, ss = _layer(xs, state_ssm_conv[i], state_ssm[i], state_rwkv_shift[i], state_rwkv[i],
                        state_ffn_conv[i], p, last)
        for j in range(5):
            new_p[j].append(sp[j])
            new_s[j].append(ss[j])
    return (xp, xs,
            jnp.stack(new_p[0]), jnp.stack(new_p[1]), jnp.stack(new_p[2]), jnp.stack(new_p[3]),
            jnp.stack(new_p[4]),
            jnp.stack(new_s[0]), jnp.stack(new_s[1]), jnp.stack(new_s[2]), jnp.stack(new_s[3]),
            jnp.stack(new_s[4]))
```

```python
import functools

import jax
import jax.numpy as jnp
from jax import lax
from jax.experimental import pallas as pl
from jax.experimental.pallas import tpu as pltpu

F32 = jnp.float32
BF16 = jnp.bfloat16

NORM_EPS = 1e-5
GN_EPS = 64e-5

LANES = 128
SUBLANES = 8
ROW_TILE = 256
IN_ROW_TILE = 512
COL_STRIP = 512
SSD_CHUNK = 128
SSD_SUBCHUNKS = 4
WKV_TBLOCK = 64
WKV_PROMPT_SEQS = 4
WKV_SAMPLE_SEQS = 8
VMEM_LIMIT = 56 * 1024 * 1024


def _dot(a, b):
    return jnp.dot(a, b, preferred_element_type=F32)


def _split(x, n):
    parts = []
    rem = x
    for i in range(n):
        p = rem.astype(BF16)
        parts.append(p)
        if i + 1 < n:
            rem = rem - p.astype(F32)
    return parts


def _dot_split_lhs(x, m, n):
    acc = None
    for p in _split(x, n):
        d = _dot(p, m)
        acc = d if acc is None else acc + d
    return acc


def _dot_split_rhs(m, x, n):
    acc = None
    for p in _split(x, n):
        d = _dot(m, p)
        acc = d if acc is None else acc + d
    return acc


def _sigmoid(x):
    return 0.5 * jnp.tanh(0.5 * x) + 0.5


def _silu(x):
    hx = 0.5 * x
    return hx * jnp.tanh(hx) + hx


def _softplus(x):
    return jnp.maximum(x, 0.0) + jnp.log1p(jnp.exp(-jnp.abs(x)))


def _rmsnorm(x, g, eps):
    ms = jnp.mean(x * x, axis=-1, keepdims=True)
    return x * lax.rsqrt(ms + eps) * g


def _shift_rows(u, hist, seg, j):
    rows = u.shape[0]
    ru = pltpu.roll(u, j, 0)
    if seg == SUBLANES:
        rh = pltpu.roll(hist, (rows - SUBLANES + j) % rows, 0)
        pos = lax.broadcasted_iota(jnp.int32, u.shape, 0) % SUBLANES
        return jnp.where(pos < j, rh, ru)
    assert seg == rows
    rh = pltpu.roll(hist, j, 0)
    pos = lax.broadcasted_iota(jnp.int32, rh.shape, 0)
    top = jnp.where(pos < j, rh, ru[:SUBLANES])
    return jnp.concatenate([top, ru[SUBLANES:]], axis=0)


def _causal_conv(u, hist, seg, w_ref, b_ref):
    taps = w_ref.shape[0]
    acc = u * w_ref[taps - 1:taps, :] + b_ref[...]
    for j in range(1, taps):
        acc = acc + _shift_rows(u, hist, seg, j) * w_ref[taps - 1 - j:taps - j, :]
    return acc


def _load_hist(hist_ref, carry_ref, seg):
    if seg == SUBLANES:
        nseq, _, c = hist_ref.shape
        return hist_ref[...].reshape(nseq * SUBLANES, c)

    @pl.when(pl.program_id(1) == 0)
    def _():
        carry_ref[...] = hist_ref[0]

    return carry_ref[...]


def _store_tail(u, tail_ref, carry_ref, seg):
    if seg == SUBLANES:
        tail_ref[...] = u.reshape(tail_ref.shape)
    else:
        last = u[u.shape[0] - SUBLANES:]
        carry_ref[...] = last
        tail_ref[0] = last


def _ssm_in_body(x_ref, hist_ref, g1_ref, wz_ref, wx_ref, wdt_ref, cw_ref, cb_ref, dtb_ref,
                 z_ref, xc_ref, dt_ref, tail_ref, carry_ref, *, seg):
    h = _rmsnorm(x_ref[...], g1_ref[...], NORM_EPS).astype(BF16)
    hist = _load_hist(hist_ref, carry_ref, seg)
    taps = cw_ref.shape[0]
    for lo in range(0, wx_ref.shape[1], COL_STRIP):
        sl = slice(lo, lo + COL_STRIP)
        u = _dot(h, wx_ref[:, sl])
        hs = hist[:, sl]
        acc = u * cw_ref[taps - 1:taps, sl] + cb_ref[:, sl]
        for j in range(1, taps):
            acc = acc + _shift_rows(u, hs, seg, j) * cw_ref[taps - 1 - j:taps - j, sl]
        xc_ref[:, sl] = _silu(acc)
        if seg == SUBLANES:
            tail_ref[:, :, sl] = u.reshape(tail_ref.shape[0], SUBLANES, COL_STRIP)
        else:
            last = u[u.shape[0] - SUBLANES:]
            carry_ref[:, sl] = last
            tail_ref[0, :, sl] = last
    for lo in range(0, wz_ref.shape[1], COL_STRIP):
        sl = slice(lo, lo + COL_STRIP)
        z_ref[:, sl] = _dot(h, wz_ref[:, sl]).astype(z_ref.dtype)
    dt_ref[...] = _softplus(_dot(h, wdt_ref[...]) + dtb_ref[...])


def _rwkv_in_body(x_ref, hist_ref, g1_ref, wrw_ref, wg_ref, mu_ref, wlora_ref, w0_ref, a0_ref,
                  kk_ref, ka_ref, seg_ref, exp_ref,
                  r_out, lw_out, k_out, v_out, kkn_out, bb_out, g_out, gates_out, tail_ref,
                  carry_ref, *, seg, dim):
    h = _rmsnorm(x_ref[...], g1_ref[...], NORM_EPS).astype(BF16)
    hist = _load_hist(hist_ref, carry_ref, seg)
    drw = wrw_ref.shape[1]

    def mixed(lo, hi):
        sl = slice(lo, hi)
        u = _dot(h, wrw_ref[:, sl])
        prev = _shift_rows(u, hist[:, sl], seg, 1)
        if seg == SUBLANES:
            tail_ref[:, :, sl] = u.reshape(tail_ref.shape[0], SUBLANES, hi - lo)
        else:
            last = u[u.shape[0] - SUBLANES:]
            carry_ref[:, sl] = last
            tail_ref[0, :, sl] = last
        return u + (prev - u) * mu_ref[:, sl]

    low = mixed(3 * dim, drw)
    lane = lax.broadcasted_iota(jnp.int32, low.shape, 1)
    lo_w = low.shape[1] // 4
    act = jnp.where(lane < lo_w, jnp.tanh(low), jnp.where(lane < 2 * lo_w, low, _sigmoid(low))).astype(BF16)
    r_out[...] = mixed(0, dim)
    wlog = -_softplus(-(w0_ref[...] + _dot(act, wlora_ref[:, 0:dim]))) - 0.5
    lw_out[...] = -jnp.exp(wlog)
    v_out[...] = mixed(2 * dim, 3 * dim)
    a = _sigmoid(a0_ref[...] + _dot(act, wlora_ref[:, dim:2 * dim]))
    k = mixed(dim, 2 * dim)
    kkr = k * kk_ref[...]
    ss = _dot_split_lhs(kkr * kkr, seg_ref[...], 2)
    inv = 1.0 / jnp.maximum(jnp.sqrt(ss), 1e-12)
    kkn = kkr * _dot_split_lhs(inv, exp_ref[...], 2)
    k_out[...] = k * (1.0 + (a - 1.0) * ka_ref[...])
    kkn_out[...] = kkn
    bb_out[...] = kkn * a
    g_out[...] = _dot(act, wlora_ref[:, 2 * dim:3 * dim]).astype(g_out.dtype)
    for lo in range(0, wg_ref.shape[1], COL_STRIP):
        sl = slice(lo, lo + COL_STRIP)
        gates_out[:, sl] = _sigmoid(_dot(h, wg_ref[:, sl])).astype(gates_out.dtype)


def _ssd_body(xm_ref, b_ref, c_ref, dt_ref, z_ref, st_ref, alog_ref, dexp_ref, ng_ref,
              tri_ref, e_ref, e2_ref, y_ref, so_ref, ht_ref, *, rows, heads, groups, seqs, subs):
    step = pl.program_id(2)
    n_steps = pl.num_programs(2)
    R = SSD_CHUNK
    width = xm_ref.shape[1] // groups
    state_n = b_ref.shape[1] // groups
    p_dim = width // heads
    gs = range(groups)
    units = [(sc, g) for sc in range(subs) for g in gs]
    us = range(len(units))
    steps = R // seqs

    def blk(ref, u, n):
        sc, g = units[u]
        v = ref[sc * R:(sc + 1) * R, g * n:(g + 1) * n] if subs > 1 else ref[:, g * n:(g + 1) * n]
        if v.shape[0] == R:
            return v
        return jnp.concatenate([v, jnp.zeros((R - v.shape[0], v.shape[1]), v.dtype)], axis=0)

    def gcols(ref, u, n):
        g = units[u][1]
        return ref[:, g * n:(g + 1) * n]

    row = lax.broadcasted_iota(jnp.int32, (R, R), 0)
    col = lax.broadcasted_iota(jnp.int32, (R, R), 1)
    lane = lax.broadcasted_iota(jnp.int32, (R, LANES), 1)
    expand = e_ref[...]
    expand2 = e2_ref[...]
    if seqs == 1:
        causal = row >= col
        tri = tri_ref[...]

        @pl.when(step == 0)
        def _():
            for g in gs:
                ht_ref[g] = st_ref[0, g * heads:(g + 1) * heads].reshape(width, state_n).T
    else:
        same = (row // steps) == (col // steps)
        causal = same & (row >= col)
        tri = causal.astype(BF16)

    xm = [blk(xm_ref, u, width) for u in us]
    bm = [blk(b_ref, u, state_n) for u in us]
    cm = [blk(c_ref, u, state_n) for u in us]
    cmb = [cm[u].astype(BF16) for u in us]
    dt = [blk(dt_ref, u, LANES) for u in us]
    a = [dt[u] * (-jnp.exp(gcols(alog_ref, u, LANES))) for u in us]
    cs = [_dot_split_rhs(tri, a[u], 3) for u in us]
    dt_e = [_dot_split_lhs(dt[u], expand, 2) for u in us]
    cs_e = [_dot_split_lhs(cs[u], expand, 3) for u in us]
    if seqs > 1:
        cs_e2 = [_dot_split_lhs(cs[u], expand2, 3) for u in us]
    bt = [bm[u].T for u in us]
    btb = [bt[u].astype(BF16) for u in us]
    cs_t = [cs[u].T for u in us]
    if seqs == 1:
        end_e = [cs_e[u][R - 1:R, :] for u in us]
    else:
        tot = [_dot_split_rhs(same.astype(BF16), a[u], 3) for u in us]
        end_e = [_dot_split_lhs(tot[u], expand, 3) for u in us]
    xdt = [xm[u] * dt_e[u] for u in us]
    xs = [(xdt[u] * jnp.exp(end_e[u] - cs_e[u])).astype(BF16) for u in us]
    cb = [_dot(cmb[u], btb[u]) for u in us]
    if seqs == 1:
        st_new = [_dot(btb[u], xs[u]) for u in us]
        ht = [ht_ref[g] for g in gs]
        y_off = []
        for u, (sc, g) in enumerate(units):
            y_off.append(_dot(cmb[u], ht[g].astype(BF16)))
            ht[g] = ht[g] * jnp.exp(end_e[u]) + st_new[u]
        for g in gs:
            ht_ref[g] = ht[g]
    else:
        y_off = []
        for u, (sc, g) in enumerate(units):
            tiles = []
            for s in range(seqs):
                r0 = s * steps
                ht0 = st_ref[s, g * heads:(g + 1) * heads].reshape(width, state_n).T
                tiles.append(_dot(cm[u][r0:r0 + steps].astype(BF16), ht0.astype(BF16)))
                bts = jnp.where((col // steps) == s, bt[u], 0.0).astype(BF16)
                ht1 = ht0 * jnp.exp(end_e[u][r0:r0 + 1]) + _dot(bts, xs[u])
                so_ref[s, g * heads:(g + 1) * heads] = ht1.T.reshape((heads,) + so_ref.shape[2:])
            y_off.append(jnp.concatenate(tiles, axis=0))
    ys = [[] for _ in us]
    for j in range(heads // 2):
        lhs, rhs = [], []
        for u in us:
            ms = []
            for hh in (2 * j, 2 * j + 1):
                if seqs == 1:
                    cs_col = jnp.broadcast_to(cs[u][:, hh:hh + 1], (R, R))
                else:
                    cs_col = cs_e2[u][:, hh * R:(hh + 1) * R]
                seg_ = cs_col - cs_t[u][hh:hh + 1, :]
                ms.append(jnp.where(causal, cb[u] * jnp.exp(jnp.where(causal, seg_, 0.0)), 0.0).astype(BF16))
            lhs.append(jnp.concatenate(ms, axis=1))
            xp = xdt[u][:, j * LANES:(j + 1) * LANES]
            rhs.append(jnp.concatenate([jnp.where(lane < p_dim, xp, 0.0), jnp.where(lane >= p_dim, xp, 0.0)],
                                       axis=0).astype(BF16))
        for u in us:
            ys[u].append(_dot(lhs[u], rhs[u]))
    for u, (sc, g) in enumerate(units):
        y = jnp.concatenate(ys[u], axis=1) + y_off[u] * jnp.exp(cs_e[u])
        y = y + gcols(dexp_ref, u, width) * xm[u]
        yz = y * _silu(blk(z_ref, u, width).astype(F32))
        yn = yz * lax.rsqrt(jnp.mean(yz * yz, axis=-1, keepdims=True) + NORM_EPS) * gcols(ng_ref, u, width)
        if subs > 1:
            y_ref[sc * R:(sc + 1) * R, g * width:(g + 1) * width] = yn.astype(y_ref.dtype)
        else:
            y_ref[:, g * width:(g + 1) * width] = yn[:rows].astype(y_ref.dtype)

    if seqs == 1:
        @pl.when(step == n_steps - 1)
        def _():
            for g in gs:
                so_ref[0, g * heads:(g + 1) * heads] = ht_ref[g].T.reshape((heads,) + so_ref.shape[2:])


def _wkv_body(r_ref, lw_ref, k_ref, v_ref, kk_ref, bb_ref, g_ref, s0_ref, rk_ref, lnw_ref, lnb_ref,
              bo_ref, tri_ref, o_ref, so_ref, s_ref, *, head_dim, single):
    nseq, steps, dim = r_ref.shape
    npair = dim // LANES
    nchunk = steps // SUBLANES
    block_ones = bo_ref[...]
    tri = tri_ref[...]

    def load_state(si, p):
        x = s0_ref[si, 2 * p:2 * p + 2].reshape(2 * head_dim, head_dim)
        xp = jnp.concatenate([x, jnp.zeros((2 * head_dim, LANES - head_dim), F32)], axis=1)
        return xp.T[:head_dim]

    def store_state(n, s):
        sp = jnp.concatenate([s, jnp.zeros((LANES - head_dim, LANES), F32)], axis=0)
        back = sp.T[:, :head_dim]
        so_ref[n // npair, 2 * (n % npair):2 * (n % npair) + 2] = back.reshape(2, head_dim, head_dim)

    if not single:
        @pl.when(pl.program_id(1) == 0)
        def _():
            for si in range(nseq):
                for p in range(npair):
                    s_ref[si * npair + p] = load_state(si, p)

    lane8 = lax.broadcasted_iota(jnp.int32, (SUBLANES, LANES), 1)
    row8 = lax.broadcasted_iota(jnp.int32, (SUBLANES, LANES), 0)
    head0 = lane8 < head_dim
    rowi = lax.broadcasted_iota(jnp.int32, (head_dim, LANES), 0)
    lanei = lax.broadcasted_iota(jnp.int32, (head_dim, LANES), 1)
    diag = (rowi == lanei % head_dim).astype(F32)

    def other_head(a):
        return pltpu.roll(a, head_dim, 1)

    def rows(a, c):
        return a[c * SUBLANES:(c + 1) * SUBLANES]

    def bc(tile, i):
        return jnp.broadcast_to(tile[i:i + 1], (SUBLANES, LANES))

    chains = []
    for si in range(nseq):
        cum_seq = _dot_split_rhs(tri, lw_ref[si], 3)
        for p in range(npair):
            sl = slice(p * LANES, (p + 1) * LANES)
            lw = lw_ref[si, :, sl]
            cum = cum_seq[:, sl]
            p_in = jnp.exp(cum)
            p_inv = jnp.exp(-cum)
            at = kk_ref[si, :, sl] * jnp.exp(cum - lw)
            rt = r_ref[si, :, sl] * p_in
            bt = bb_ref[si, :, sl] * p_inv
            kt = k_ref[si, :, sl] * p_inv
            at_o, rt_o, bt_o, kt_o = other_head(at), other_head(rt), other_head(bt), other_head(kt)
            x4 = jnp.concatenate([rows(a, c) for c in range(nchunk) for a in (bt, bt_o, kt, kt_o)], axis=0)
            if x4.shape[0] < LANES:
                x4 = jnp.concatenate([x4, jnp.zeros((LANES - x4.shape[0], LANES), F32)], axis=0)
            xt = x4.T[:head_dim].astype(BF16)
            chains.append(dict(si=si, sl=sl, at=at, rt=rt, bt=bt, kt=kt, at_o=at_o, rt_o=rt_o, xt=xt,
                               v=v_ref[si, :, sl], p_end=p_in[steps - 1:steps],
                               s=load_state(si, p) if single else s_ref[si * npair + p],
                               ys=[]))

    for c in range(nchunk):
        for ch in chains:
            at_c, rt_c, bt_c, kt_c = rows(ch['at'], c), rows(ch['rt'], c), rows(ch['bt'], c), rows(ch['kt'], c)
            tiles = []
            for i in range(SUBLANES):
                am = jnp.where(row8 > i, at_c, 0.0)
                rm = jnp.where(row8 >= i, rt_c, 0.0)
                bi, ki = bc(bt_c, i), bc(kt_c, i)
                tiles += [am * bi, am * ki, rm * bi, rm * ki]
            coef = _dot(jnp.concatenate(tiles, axis=0).astype(BF16), block_ones)
            ch['coef'] = [rows(coef, n) for n in range(4 * SUBLANES)]
            v_c = rows(ch['v'], c)
            va = None
            yv = None
            for i in range(SUBLANES):
                vi = bc(v_c, i)
                t_ak = ch['coef'][4 * i + 1] * vi
                t_rk = ch['coef'][4 * i + 3] * vi
                va = t_ak if va is None else va + t_ak
                yv = t_rk if yv is None else yv + t_rk
            ch['va'], ch['yv'], ch['v_c'] = va, yv, v_c
            lhs = jnp.concatenate([at_c[:, :head_dim], rows(ch['at_o'], c)[:, :head_dim],
                                   rt_c[:, :head_dim], rows(ch['rt_o'], c)[:, :head_dim]], axis=0)
            ch['lhs'] = lhs.astype(BF16)
        for ch in chains:
            ch['g'] = _dot(ch['lhs'], ch['s'].astype(BF16))
        for ch in chains:
            g = ch['g']
            g_a = jnp.where(head0, rows(g, 0), rows(g, 1))
            g_r = jnp.where(head0, rows(g, 2), rows(g, 3))
            sa = g_a + ch['va']
            y = g_r + ch['yv']
            for i in range(SUBLANES):
                sai = bc(sa, i)
                if i + 1 < SUBLANES:
                    sa = sa - ch['coef'][4 * i] * sai
                y = y - ch['coef'][4 * i + 2] * sai
            ch['ys'].append(y)
            v_c = ch['v_c']
            wd = jnp.concatenate([jnp.where(head0, -sa, 0.0), jnp.where(head0, 0.0, -sa),
                                  jnp.where(head0, v_c, 0.0), jnp.where(head0, 0.0, v_c)], axis=0)
            ch['wd'] = wd.astype(BF16)
        for ch in chains:
            cols = ch['xt'][:, c * 4 * SUBLANES:(c + 1) * 4 * SUBLANES]
            ch['s'] = ch['s'] + _dot(cols, ch['wd'])

    inv_n = 1.0 / head_dim
    cat = lambda key: jnp.concatenate([ch[key] for ch in chains], axis=0)
    part = lambda a, n, m: a[n * m:(n + 1) * m]
    for ch in chains:
        ch['pd'] = diag * ch['p_end']
        ch['y'] = jnp.concatenate(ch['ys'], axis=0)
        ch['rk'] = r_ref[ch['si'], :, ch['sl']] * k_ref[ch['si'], :, ch['sl']] * rk_ref[:, ch['sl']]
    p_col = _dot_split_lhs(cat('pd'), block_ones, 3)
    y_all = cat('y')
    mu = _dot_split_lhs(y_all, block_ones, 2) * inv_n
    bonus = _dot_split_lhs(cat('rk'), block_ones, 2)
    d_all = y_all - mu
    var = _dot_split_lhs(d_all * d_all, block_ones, 2) * inv_n
    for n, ch in enumerate(chains):
        si, sl = ch['si'], ch['sl']
        s_new = ch['s'] * part(p_col, n, head_dim)
        if single:
            store_state(n, s_new)
        else:
            s_ref[n] = s_new
            ch['s'] = s_new
        yn = part(d_all, n, steps) * lax.rsqrt(part(var, n, steps) + GN_EPS) * lnw_ref[:, sl] + lnb_ref[:, sl]
        out = (yn + part(bonus, n, steps) * ch['v']) * g_ref[si, :, sl].astype(F32)
        o_ref[si, :, sl] = out.astype(o_ref.dtype)

    if not single:
        @pl.when(pl.program_id(1) == pl.num_programs(1) - 1)
        def _():
            for n, ch in enumerate(chains):
                store_state(n, ch['s'])


def _post_body(x_ref, ya_ref, yb_ref, gt_ref, hist_ref, wa_ref, wb_ref, wo_ref, g2_ref, wup_ref,
               cw_ref, cb_ref, wdn_ref, gf_ref, o_ref, tail_ref, carry_ref, *, seg, final_norm):
    dm = x_ref.shape[1]
    ua = _dot(ya_ref[...].astype(BF16), wa_ref[...])
    ub = _dot(yb_ref[...].astype(BF16), wb_ref[...])
    gates = gt_ref[...].astype(F32)
    m = (gates[:, :dm] * ua + gates[:, dm:] * ub).astype(BF16)
    x1 = x_ref[...] + _dot(m, wo_ref[...])
    h2 = _rmsnorm(x1, g2_ref[...], NORM_EPS).astype(BF16)
    up = _dot(h2, wup_ref[...])
    dff = up.shape[1] // 2
    ug = up[:, :dff]
    hist = _load_hist(hist_ref, carry_ref, seg)
    ugc = _causal_conv(ug, hist, seg, cw_ref, cb_ref)
    _store_tail(ug, tail_ref, carry_ref, seg)
    act = (_silu(ugc) * up[:, dff:]).astype(BF16)
    x2 = x1 + _dot(act, wdn_ref[...])
    if final_norm:
        x2 = _rmsnorm(x2, gf_ref[...], NORM_EPS)
    o_ref[...] = x2


def _const_spec(shape):
    nd = len(shape)
    return pl.BlockSpec(shape, lambda *_: (0,) * nd, pipeline_mode=pl.Buffered(1))


def _params(sem):
    return pltpu.CompilerParams(dimension_semantics=sem, vmem_limit_bytes=VMEM_LIMIT)


def _token_tiling(nseq, seqlen, tile=ROW_TILE):
    if seqlen == SUBLANES:
        per = tile // SUBLANES
        assert nseq % per == 0
        return per, SUBLANES, (nseq // per, 1)
    assert seqlen % tile == 0
    return 1, tile, (nseq, seqlen // tile)


def _row_spec(cols, lt, tile=ROW_TILE):
    return pl.BlockSpec((tile, cols), lambda i, l: (i * lt + l, 0))


def _hist_spec(per, cols):
    return pl.BlockSpec((per, SUBLANES, cols), lambda i, l: (i, 0, 0))


def _ssm_in(x, hist, p, nseq, seqlen, act_dtype):
    tile = IN_ROW_TILE if seqlen % IN_ROW_TILE == 0 else ROW_TILE
    per, seg, grid = _token_tiling(nseq, seqlen, tile)
    row_spec = functools.partial(_row_spec, tile=tile)
    t, dm = x.shape
    dz = p['wz'].shape[1]
    dc = p['wx'].shape[1]
    dd = p['wdt'].shape[1]
    lt = grid[1]
    return pl.pallas_call(
        functools.partial(_ssm_in_body, seg=seg),
        grid=grid,
        in_specs=[row_spec(dm, lt), _hist_spec(per, dc), _const_spec((1, dm)),
                  _const_spec(p['wz'].shape), _const_spec(p['wx'].shape), _const_spec(p['wdt'].shape),
                  _const_spec(p['conv_w'].shape), _const_spec((1, dc)), _const_spec((1, dd))],
        out_specs=[row_spec(dz, lt), row_spec(dc, lt), row_spec(dd, lt), _hist_spec(per, dc)],
        out_shape=[jax.ShapeDtypeStruct((t, dz), act_dtype), jax.ShapeDtypeStruct((t, dc), F32),
                   jax.ShapeDtypeStruct((t, dd), F32), jax.ShapeDtypeStruct((nseq, SUBLANES, dc), F32)],
        scratch_shapes=[pltpu.VMEM((SUBLANES, dc), F32)],
        compiler_params=_params(("arbitrary", "arbitrary")),
        name="ssm_in",
    )(x, hist, p['norm1_g'], p['wz'], p['wx'], p['wdt'], p['conv_w'], p['conv_b'], p['dt_bias'])


def _rwkv_in(x, hist, p, nseq, seqlen, act_dtype):
    tile = IN_ROW_TILE if seqlen % IN_ROW_TILE == 0 else ROW_TILE
    per, seg, grid = _token_tiling(nseq, seqlen, tile)
    row_spec = functools.partial(_row_spec, tile=tile)
    t, dm = x.shape
    dim = p['w0'].shape[1]
    drw = p['wrw'].shape[1]
    dg = p['wg'].shape[1]
    lt = grid[1]
    f32_out = jax.ShapeDtypeStruct((t, dim), F32)
    return pl.pallas_call(
        functools.partial(_rwkv_in_body, seg=seg, dim=dim),
        grid=grid,
        in_specs=[row_spec(dm, lt), _hist_spec(per, drw), _const_spec((1, dm)),
                  _const_spec(p['wrw'].shape), _const_spec(p['wg'].shape), _const_spec((1, drw)),
                  _const_spec(p['wlora'].shape), _const_spec((1, dim)), _const_spec((1, dim)),
                  _const_spec((1, dim)), _const_spec((1, dim)),
                  _const_spec(p['head_sum'].shape), _const_spec(p['head_expand'].shape)],
        out_specs=[row_spec(dim, lt)] * 7 + [row_spec(dg, lt), _hist_spec(per, drw)],
        out_shape=[f32_out] * 6 + [jax.ShapeDtypeStruct((t, dim), act_dtype),
                                   jax.ShapeDtypeStruct((t, dg), BF16),
                                   jax.ShapeDtypeStruct((nseq, SUBLANES, drw), F32)],
        scratch_shapes=[pltpu.VMEM((SUBLANES, drw), F32)],
        compiler_params=_params(("arbitrary", "arbitrary")),
        name="rwkv_in",
    )(x, hist, p['norm1_g'], p['wrw'], p['wg'], p['mu'], p['wlora'], p['w0'], p['a0'],
      p['k_k'], p['k_a'], p['head_sum'], p['head_expand'])


def _ssd(xc, dt, z, state, p, nseq, seqlen, act_dtype):
    t = xc.shape[0]
    _, n_heads, p_dim, state_n = state.shape
    groups = dt.shape[1] // LANES
    heads = n_heads // groups
    width = heads * p_dim
    d_inner = groups * width
    assert state_n == LANES and width % LANES == 0 and p_dim * 2 == LANES
    subs = 1
    if seqlen % SSD_CHUNK == 0:
        subs = SSD_SUBCHUNKS if seqlen % (SSD_SUBCHUNKS * SSD_CHUNK) == 0 else 1
        seqs, gps, rows, nc = 1, groups, subs * SSD_CHUNK, seqlen // (subs * SSD_CHUNK)
    elif seqlen == SUBLANES and nseq % (SSD_CHUNK // SUBLANES) == 0:
        seqs, gps, rows, nc = SSD_CHUNK // SUBLANES, 1, SSD_CHUNK, 1
    else:
        assert seqlen == SUBLANES
        seqs, gps, rows, nc = 1, groups, SUBLANES, 1
    gn = gps * state_n
    b_blk = d_inner // gn
    assert d_inner % gn == 0
    row_map = lambda b, g, c: (b * nc + c, g)
    st_spec = pl.BlockSpec((seqs, gps * heads, p_dim, state_n), lambda b, g, c: (b, g, 0, 0))
    return pl.pallas_call(
        functools.partial(_ssd_body, rows=rows, heads=heads, groups=gps, seqs=seqs, subs=subs),
        grid=(nseq // seqs, groups // gps, nc),
        in_specs=[pl.BlockSpec((rows, gps * width), row_map),
                  pl.BlockSpec((rows, gn), lambda b, g, c: (b * nc + c, b_blk + g)),
                  pl.BlockSpec((rows, gn), lambda b, g, c: (b * nc + c, b_blk + groups // gps + g)),
                  pl.BlockSpec((rows, gps * LANES), row_map),
                  pl.BlockSpec((rows, gps * width), row_map),
                  st_spec,
                  pl.BlockSpec((1, gps * LANES), lambda b, g, c: (0, g)),
                  pl.BlockSpec((1, gps * width), lambda b, g, c: (0, g)),
                  pl.BlockSpec((1, gps * width), lambda b, g, c: (0, g)),
                  _const_spec(p['tri'].shape), _const_spec(p['ssd_expand'].shape),
                  _const_spec(p['ssd_expand2'].shape)],
        out_specs=[pl.BlockSpec((rows, gps * width), row_map), st_spec],
        out_shape=[jax.ShapeDtypeStruct((t, d_inner), act_dtype),
                   jax.ShapeDtypeStruct(state.shape, F32)],
        scratch_shapes=[pltpu.VMEM((gps, state_n, width), F32)],
        compiler_params=_params(("arbitrary", "arbitrary", "arbitrary")),
        name="ssd",
    )(xc, xc, xc, dt, z, state, p['a_log'], p['d_exp'], p['ssm_norm_g'],
      p['tri'], p['ssd_expand'], p['ssd_expand2'])


def _wkv(r, lw, k, v, kk, bb, g, state, p, nseq, seqlen, act_dtype):
    dim = r.shape[1]
    _, n_heads, head_dim, _ = state.shape
    npair = dim // LANES
    assert n_heads == 2 * npair and 2 * head_dim == LANES
    if seqlen == SUBLANES:
        per, steps = WKV_SAMPLE_SEQS, SUBLANES
    else:
        per, steps = (WKV_PROMPT_SEQS if nseq % WKV_PROMPT_SEQS == 0 else 1), WKV_TBLOCK
    assert nseq % per == 0 and seqlen % steps == 0
    grid = (nseq // per, seqlen // steps)
    seq_spec = pl.BlockSpec((per, steps, dim), lambda i, tb: (i, tb, 0))
    st_spec = pl.BlockSpec((per, n_heads, head_dim, head_dim), lambda i, tb: (i, 0, 0, 0))
    as3 = lambda a: a.reshape(nseq, seqlen, dim)
    out, s_out = pl.pallas_call(
        functools.partial(_wkv_body, head_dim=head_dim, single=(seqlen == steps)),
        grid=grid,
        in_specs=[seq_spec] * 7 + [st_spec, _const_spec((1, dim)), _const_spec((1, dim)),
                                   _const_spec((1, dim)), _const_spec((LANES, LANES)),
                                   _const_spec((steps, steps))],
        out_specs=[seq_spec, st_spec],
        out_shape=[jax.ShapeDtypeStruct((nseq, seqlen, dim), act_dtype),
                   jax.ShapeDtypeStruct(state.shape, F32)],
        scratch_shapes=[pltpu.VMEM((per * npair, head_dim, LANES), F32)],
        compiler_params=_params(("arbitrary", "arbitrary")),
        name="wkv",
    )(as3(r), as3(lw), as3(k), as3(v), as3(kk), as3(bb), as3(g), state,
      p['r_k'], p['ln_w'], p['ln_b'], p['block_ones'], p['tri'][:steps, :steps])
    return out.reshape(nseq * seqlen, dim), s_out


def _post(x, ya, yb, gates, hist, p, nseq, seqlen, final_norm):
    per, seg, grid = _token_tiling(nseq, seqlen)
    t, dm = x.shape
    dff = p['wdn'].shape[0]
    lt = grid[1]
    return pl.pallas_call(
        functools.partial(_post_body, seg=seg, final_norm=final_norm),
        grid=grid,
        in_specs=[_row_spec(dm, lt), _row_spec(ya.shape[1], lt), _row_spec(yb.shape[1], lt),
                  _row_spec(gates.shape[1], lt), _hist_spec(per, dff),
                  _const_spec(p['wa'].shape), _const_spec(p['wb'].shape), _const_spec(p['wo'].shape),
                  _const_spec((1, dm)), _const_spec(p['wup'].shape), _const_spec(p['ffn_conv_w'].shape),
                  _const_spec((1, dff)), _const_spec(p['wdn'].shape), _const_spec((1, dm))],
        out_specs=[_row_spec(dm, lt), _hist_spec(per, dff)],
        out_shape=[jax.ShapeDtypeStruct((t, dm), F32), jax.ShapeDtypeStruct((nseq, SUBLANES, dff), F32)],
        scratch_shapes=[pltpu.VMEM((SUBLANES, dff), F32)],
        compiler_params=_params(("arbitrary", "arbitrary")),
        name="post",
    )(x, ya, yb, gates, hist, p['wa'], p['wb'], p['wo'], p['norm2_g'], p['wup'], p['ffn_conv_w'],
      p['ffn_conv_b'], p['wdn'], p['final_g'])


def _prep_layer(w, dims):
    d_inner, conv_dim, n_heads, shift_dim, groups, rwkv_dim, rwkv_heads = dims
    row = lambda a: a.reshape(1, -1).astype(F32)
    w_in = w['w_in']
    o1 = d_inner
    o2 = o1 + conv_dim
    o3 = o2 + n_heads
    o4 = o3 + shift_dim
    hpg = n_heads // groups

    def per_group(a):
        lead = a.shape[:-1]
        a = a.reshape(lead + (groups, hpg))
        a = jnp.pad(a, [(0, 0)] * len(lead) + [(0, 0), (0, LANES - hpg)])
        return a.reshape(lead + (groups * LANES,))

    p_dim = d_inner // n_heads
    head_dim = rwkv_dim // rwkv_heads
    lora_w = w['rwkv_w_up'].shape[0]
    lora_a = w['rwkv_a_up'].shape[0]
    lora_g = w['rwkv_g_up'].shape[0]
    assert lora_w == lora_a and lora_g == 2 * lora_w
    wlora = jnp.zeros((lora_w + lora_a + lora_g, 3 * rwkv_dim), F32)
    wlora = wlora.at[:lora_w, :rwkv_dim].set(w['rwkv_w_up'])
    wlora = wlora.at[lora_w:lora_w + lora_a, rwkv_dim:2 * rwkv_dim].set(w['rwkv_a_up'])
    wlora = wlora.at[lora_w + lora_a:, 2 * rwkv_dim:].set(w['rwkv_g_up'])

    ch = jnp.arange(rwkv_dim) // head_dim
    head_sum = (ch[:, None] == jnp.arange(LANES)[None, :]).astype(BF16)
    li = jnp.arange(LANES)
    width = hpg * p_dim
    ssd_expand = (li[:, None] == (jnp.arange(width) // p_dim)[None, :]).astype(BF16)
    ssd_expand2 = (li[:, None] == (jnp.arange(hpg * SSD_CHUNK) // SSD_CHUNK)[None, :]).astype(BF16)
    ci = jnp.arange(SSD_CHUNK)
    tri = (ci[:, None] >= ci[None, :]).astype(BF16)
    block_ones = ((li[:, None] // head_dim) == (li[None, :] // head_dim)).astype(BF16)

    return {
        'norm1_g': row(w['norm1_g']),
        'wz': w_in[:, :o1].astype(BF16),
        'wx': w_in[:, o1:o2].astype(BF16),
        'wdt': per_group(w_in[:, o2:o3]).astype(BF16),
        'wrw': w_in[:, o3:o4].astype(BF16),
        'wg': w_in[:, o4:].astype(BF16),
        'conv_w': w['ssm_conv_w'].astype(F32),
        'conv_b': row(w['ssm_conv_b']),
        'dt_bias': row(per_group(w['ssm_dt_bias'])),
        'a_log': row(per_group(w['ssm_a_log'])),
        'd_exp': row(jnp.repeat(w['ssm_d'], p_dim)),
        'ssm_norm_g': row(w['ssm_norm_g']),
        'wa': w['w_branch_a'].astype(BF16),
        'mu': row(w['rwkv_mu']),
        'wlora': wlora.astype(BF16),
        'w0': row(w['rwkv_w0']),
        'a0': row(w['rwkv_a0']),
        'k_k': row(w['rwkv_k_k']),
        'k_a': row(w['rwkv_k_a']),
        'r_k': row(w['rwkv_r_k']),
        'ln_w': row(w['rwkv_ln_w']),
        'ln_b': row(w['rwkv_ln_b']),
        'wb': w['w_branch_b'].astype(BF16),
        'wo': w['w_out'].astype(BF16),
        'norm2_g': row(w['norm2_g']),
        'wup': w['ffn_w_up'].astype(BF16),
        'ffn_conv_w': w['ffn_conv_w'].astype(F32),
        'ffn_conv_b': row(w['ffn_conv_b']),
        'wdn': w['ffn_w_down'].astype(BF16),
        'head_sum': head_sum,
        'head_expand': head_sum.T,
        'ssd_expand': ssd_expand,
        'ssd_expand2': ssd_expand2,
        'tri': tri,
        'block_ones': block_ones,
    }


def _hist8(state_rows):
    k = state_rows.shape[1]
    return jnp.pad(state_rows.astype(F32), ((0, 0), (SUBLANES - k, 0), (0, 0)))


def _layer(x, conv_buf, ssm_state, shift_buf, wkv_state, ffn_buf, p, final_norm):
    nseq, seqlen, dm = x.shape
    act_dtype = BF16 if seqlen % 16 == 0 else F32
    xf = x.reshape(nseq * seqlen, dm)
    z, xc, dt, conv_tail = _ssm_in(xf, _hist8(conv_buf), p, nseq, seqlen, act_dtype)
    r, lw, k, v, kk, bb, g, gates, shift_tail = _rwkv_in(xf, _hist8(shift_buf[:, None]), p, nseq, seqlen,
                                                         act_dtype)
    ya, new_ssm = _ssd(xc, dt, z, ssm_state.astype(F32), p, nseq, seqlen, act_dtype)
    yb, new_wkv = _wkv(r, lw, k, v, kk, bb, g, wkv_state.astype(F32), p, nseq, seqlen, act_dtype)
    out, ffn_tail = _post(xf, ya, yb, gates, _hist8(ffn_buf), p, nseq, seqlen, final_norm)
    new_conv = conv_tail[:, SUBLANES - conv_buf.shape[1]:]
    new_shift = shift_tail[:, SUBLANES - 1]
    new_ffn = ffn_tail[:, SUBLANES - ffn_buf.shape[1]:]
    return out.reshape(nseq, seqlen, dm), (new_conv, new_ssm, new_shift, new_wkv, new_ffn)


_LAYER_WEIGHTS = ('norm1_g', 'w_in', 'ssm_conv_w', 'ssm_conv_b', 'ssm_dt_bias', 'ssm_a_log', 'ssm_d',
                  'ssm_norm_g', 'w_branch_a', 'rwkv_mu', 'rwkv_w0', 'rwkv_w_up', 'rwkv_a0', 'rwkv_a_up',
                  'rwkv_g_up', 'rwkv_k_k', 'rwkv_k_a', 'rwkv_r_k', 'rwkv_ln_w', 'rwkv_ln_b', 'w_branch_b',
                  'w_out', 'norm2_g', 'ffn_w_up', 'ffn_conv_w', 'ffn_conv_b', 'ffn_w_down')


def kernel(x_prompt, x_sample, state_ssm_conv, state_ssm, state_rwkv_shift, state_rwkv, state_ffn_conv,
           norm1_g, w_in, ssm_conv_w, ssm_conv_b, ssm_dt_bias, ssm_a_log, ssm_d, ssm_norm_g, w_branch_a,
           rwkv_mu, rwkv_w0, rwkv_w_up, rwkv_a0, rwkv_a_up, rwkv_g_up, rwkv_k_k, rwkv_k_a, rwkv_r_k,
           rwkv_ln_w, rwkv_ln_b, w_branch_b, w_out, norm2_g, ffn_w_up, ffn_conv_w, ffn_conv_b, ffn_w_down,
           final_g):
    stacked = dict(zip(_LAYER_WEIGHTS, (
        norm1_g, w_in, ssm_conv_w, ssm_conv_b, ssm_dt_bias, ssm_a_log, ssm_d, ssm_norm_g, w_branch_a,
        rwkv_mu, rwkv_w0, rwkv_w_up, rwkv_a0, rwkv_a_up, rwkv_g_up, rwkv_k_k, rwkv_k_a, rwkv_r_k,
        rwkv_ln_w, rwkv_ln_b, w_branch_b, w_out, norm2_g, ffn_w_up, ffn_conv_w, ffn_conv_b, ffn_w_down)))
    depth = w_in.shape[0]
    _, _, n_heads, _, _ = state_ssm.shape
    conv_dim = state_ssm_conv.shape[-1]
    d_inner = w_branch_a.shape[1]
    shift_dim = state_rwkv_shift.shape[-1]
    rwkv_heads = state_rwkv.shape[2]
    rwkv_dim = w_branch_b.shape[1]
    groups = (conv_dim - d_inner) // (2 * state_ssm.shape[-1])
    dims = (d_inner, conv_dim, n_heads, shift_dim, groups, rwkv_dim, rwkv_heads)

    xp, xs = x_prompt, x_sample
    bp = xp.shape[0]
    new_p = ([], [], [], [], [])
    new_s = ([], [], [], [], [])
    for i in range(depth):
        p = _prep_layer({name: a[i] for name, a in stacked.items()}, dims)
        p['final_g'] = final_g.reshape(1, -1).astype(F32)
        last = i == depth - 1
        xp, sp = _layer(
            xp,
            jnp.zeros((bp,) + state_ssm_conv.shape[2:], F32),
            jnp.zeros((bp,) + state_ssm.shape[2:], F32),
            jnp.zeros((bp,) + state_rwkv_shift.shape[2:], F32),
            jnp.zeros((bp,) + state_rwkv.shape[2:], F32),
            jnp.zeros((bp,) + state_ffn_conv.shape[2:], F32),
            p, last)
        xs, ss = _layer(xs, state_ssm_conv[i], state_ssm[i], state_rwkv_shift[i], state_rwkv[i],
                        state_ffn_conv[i], p, last)
        for j in range(5):
            new_p[j].append(sp[j])
            new_s[j].append(ss[j])
    return (xp, xs,
            jnp.stack(new_p[0]), jnp.stack(new_p[1]), jnp.stack(new_p[2]), jnp.stack(new_p[3]),
            jnp.stack(new_p[4]),
            jnp.stack(new_s[0]), jnp.stack(new_s[1]), jnp.stack(new_s[2]), jnp.stack(new_s[3]),
            jnp.stack(new_s[4]))
```

```python
import functools

import jax
import jax.numpy as jnp
from jax import lax
from jax.experimental import pallas as pl
from jax.experimental.pallas import tpu as pltpu

F32 = jnp.float32
BF16 = jnp.bfloat16

NORM_EPS = 1e-5
GN_EPS = 64e-5

LANES = 128
SUBLANES = 8
ROW_TILE = 256
SSM_ROW_TILE = 512
COL_STRIP = 512
SSD_CHUNK = 128
SSD_SUBCHUNKS = 4
WKV_TBLOCK = 64
WKV_PROMPT_SEQS = 4
WKV_SAMPLE_SEQS = 8
VMEM_LIMIT = 56 * 1024 * 1024


def _dot(a, b):
    return jnp.dot(a, b, preferred_element_type=F32)


def _split(x, n):
    parts = []
    rem = x
    for i in range(n):
        p = rem.astype(BF16)
        parts.append(p)
        if i + 1 < n:
            rem = rem - p.astype(F32)
    return parts


def _dot_split_lhs(x, m, n):
    acc = None
    for p in _split(x, n):
        d = _dot(p, m)
        acc = d if acc is None else acc + d
    return acc


def _dot_split_rhs(m, x, n):
    acc = None
    for p in _split(x, n):
        d = _dot(m, p)
        acc = d if acc is None else acc + d
    return acc


def _sigmoid(x):
    return 0.5 * jnp.tanh(0.5 * x) + 0.5


def _silu(x):
    hx = 0.5 * x
    return hx * jnp.tanh(hx) + hx


def _softplus(x):
    return jnp.maximum(x, 0.0) + jnp.log1p(jnp.exp(-jnp.abs(x)))


def _rmsnorm(x, g, eps):
    ms = jnp.mean(x * x, axis=-1, keepdims=True)
    return x * lax.rsqrt(ms + eps) * g


def _shift_rows(u, hist, seg, j):
    rows = u.shape[0]
    ru = pltpu.roll(u, j, 0)
    if seg == SUBLANES:
        rh = pltpu.roll(hist, (rows - SUBLANES + j) % rows, 0)
        pos = lax.broadcasted_iota(jnp.int32, u.shape, 0) % SUBLANES
        return jnp.where(pos < j, rh, ru)
    assert seg == rows
    rh = pltpu.roll(hist, j, 0)
    pos = lax.broadcasted_iota(jnp.int32, rh.shape, 0)
    top = jnp.where(pos < j, rh, ru[:SUBLANES])
    return jnp.concatenate([top, ru[SUBLANES:]], axis=0)


def _causal_conv(u, hist, seg, w_ref, b_ref):
    taps = w_ref.shape[0]
    acc = u * w_ref[taps - 1:taps, :] + b_ref[...]
    for j in range(1, taps):
        acc = acc + _shift_rows(u, hist, seg, j) * w_ref[taps - 1 - j:taps - j, :]
    return acc


def _load_hist(hist_ref, carry_ref, seg):
    if seg == SUBLANES:
        nseq, _, c = hist_ref.shape
        return hist_ref[...].reshape(nseq * SUBLANES, c)

    @pl.when(pl.program_id(1) == 0)
    def _():
        carry_ref[...] = hist_ref[0]

    return carry_ref[...]


def _store_tail(u, tail_ref, carry_ref, seg):
    if seg == SUBLANES:
        tail_ref[...] = u.reshape(tail_ref.shape)
    else:
        last = u[u.shape[0] - SUBLANES:]
        carry_ref[...] = last
        tail_ref[0] = last


def _ssm_in_body(x_ref, hist_ref, g1_ref, wz_ref, wx_ref, wdt_ref, cw_ref, cb_ref, dtb_ref,
                 z_ref, xc_ref, dt_ref, tail_ref, carry_ref, *, seg):
    h = _rmsnorm(x_ref[...], g1_ref[...], NORM_EPS).astype(BF16)
    hist = _load_hist(hist_ref, carry_ref, seg)
    taps = cw_ref.shape[0]
    for lo in range(0, wx_ref.shape[1], COL_STRIP):
        sl = slice(lo, lo + COL_STRIP)
        u = _dot(h, wx_ref[:, sl])
        hs = hist[:, sl]
        acc = u * cw_ref[taps - 1:taps, sl] + cb_ref[:, sl]
        for j in range(1, taps):
            acc = acc + _shift_rows(u, hs, seg, j) * cw_ref[taps - 1 - j:taps - j, sl]
        xc_ref[:, sl] = _silu(acc)
        if seg == SUBLANES:
            tail_ref[:, :, sl] = u.reshape(tail_ref.shape[0], SUBLANES, COL_STRIP)
        else:
            last = u[u.shape[0] - SUBLANES:]
            carry_ref[:, sl] = last
            tail_ref[0, :, sl] = last
    for lo in range(0, wz_ref.shape[1], COL_STRIP):
        sl = slice(lo, lo + COL_STRIP)
        z_ref[:, sl] = _dot(h, wz_ref[:, sl]).astype(z_ref.dtype)
    dt_ref[...] = _softplus(_dot(h, wdt_ref[...]) + dtb_ref[...])


def _rwkv_in_body(x_ref, hist_ref, g1_ref, wrw_ref, wg_ref, mu_ref, wlora_ref, w0_ref, a0_ref,
                  kk_ref, ka_ref, seg_ref, exp_ref,
                  r_out, lw_out, k_out, v_out, kkn_out, bb_out, g_out, gates_out, tail_ref,
                  carry_ref, *, seg, dim):
    h = _rmsnorm(x_ref[...], g1_ref[...], NORM_EPS).astype(BF16)
    hist = _load_hist(hist_ref, carry_ref, seg)
    drw = wrw_ref.shape[1]

    def mixed(lo, hi):
        sl = slice(lo, hi)
        u = _dot(h, wrw_ref[:, sl])
        prev = _shift_rows(u, hist[:, sl], seg, 1)
        if seg == SUBLANES:
            tail_ref[:, :, sl] = u.reshape(tail_ref.shape[0], SUBLANES, hi - lo)
        else:
            last = u[u.shape[0] - SUBLANES:]
            carry_ref[:, sl] = last
            tail_ref[0, :, sl] = last
        return u + (prev - u) * mu_ref[:, sl]

    low = mixed(3 * dim, drw)
    lane = lax.broadcasted_iota(jnp.int32, low.shape, 1)
    lo_w = low.shape[1] // 4
    act = jnp.where(lane < lo_w, jnp.tanh(low), jnp.where(lane < 2 * lo_w, low, _sigmoid(low))).astype(BF16)
    r_out[...] = mixed(0, dim)
    wlog = -_softplus(-(w0_ref[...] + _dot(act, wlora_ref[:, 0:dim]))) - 0.5
    lw_out[...] = -jnp.exp(wlog)
    v_out[...] = mixed(2 * dim, 3 * dim)
    a = _sigmoid(a0_ref[...] + _dot(act, wlora_ref[:, dim:2 * dim]))
    k = mixed(dim, 2 * dim)
    kkr = k * kk_ref[...]
    ss = _dot_split_lhs(kkr * kkr, seg_ref[...], 2)
    inv = 1.0 / jnp.maximum(jnp.sqrt(ss), 1e-12)
    kkn = kkr * _dot_split_lhs(inv, exp_ref[...], 2)
    k_out[...] = k * (1.0 + (a - 1.0) * ka_ref[...])
    kkn_out[...] = kkn
    bb_out[...] = kkn * a
    g_out[...] = _dot(act, wlora_ref[:, 2 * dim:3 * dim]).astype(g_out.dtype)
    for lo in range(0, wg_ref.shape[1], COL_STRIP):
        sl = slice(lo, lo + COL_STRIP)
        gates_out[:, sl] = _sigmoid(_dot(h, wg_ref[:, sl])).astype(gates_out.dtype)


def _ssd_body(xm_ref, b_ref, c_ref, dt_ref, z_ref, st_ref, alog_ref, dexp_ref, ng_ref,
              tri_ref, e_ref, e2_ref, y_ref, so_ref, ht_ref, *, rows, heads, groups, seqs, subs):
    step = pl.program_id(2)
    n_steps = pl.num_programs(2)
    R = SSD_CHUNK
    width = xm_ref.shape[1] // groups
    state_n = b_ref.shape[1] // groups
    p_dim = width // heads
    gs = range(groups)
    units = [(sc, g) for sc in range(subs) for g in gs]
    us = range(len(units))
    steps = R // seqs

    def blk(ref, u, n):
        sc, g = units[u]
        v = ref[sc * R:(sc + 1) * R, g * n:(g + 1) * n] if subs > 1 else ref[:, g * n:(g + 1) * n]
        if v.shape[0] == R:
            return v
        return jnp.concatenate([v, jnp.zeros((R - v.shape[0], v.shape[1]), v.dtype)], axis=0)

    def gcols(ref, u, n):
        g = units[u][1]
        return ref[:, g * n:(g + 1) * n]

    row = lax.broadcasted_iota(jnp.int32, (R, R), 0)
    col = lax.broadcasted_iota(jnp.int32, (R, R), 1)
    lane = lax.broadcasted_iota(jnp.int32, (R, LANES), 1)
    expand = e_ref[...]
    expand2 = e2_ref[...]
    if seqs == 1:
        causal = row >= col
        tri = tri_ref[...]

        @pl.when(step == 0)
        def _():
            for g in gs:
                ht_ref[g] = st_ref[0, g * heads:(g + 1) * heads].reshape(width, state_n).T
    else:
        same = (row // steps) == (col // steps)
        causal = same & (row >= col)
        tri = causal.astype(BF16)

    xm = [blk(xm_ref, u, width) for u in us]
    bm = [blk(b_ref, u, state_n) for u in us]
    cm = [blk(c_ref, u, state_n) for u in us]
    cmb = [cm[u].astype(BF16) for u in us]
    dt = [blk(dt_ref, u, LANES) for u in us]
    a = [dt[u] * (-jnp.exp(gcols(alog_ref, u, LANES))) for u in us]
    cs = [_dot_split_rhs(tri, a[u], 3) for u in us]
    dt_e = [_dot_split_lhs(dt[u], expand, 2) for u in us]
    cs_e = [_dot_split_lhs(cs[u], expand, 3) for u in us]
    if seqs > 1:
        cs_e2 = [_dot_split_lhs(cs[u], expand2, 3) for u in us]
    bt = [bm[u].T for u in us]
    btb = [bt[u].astype(BF16) for u in us]
    cs_t = [cs[u].T for u in us]
    if seqs == 1:
        end_e = [cs_e[u][R - 1:R, :] for u in us]
    else:
        tot = [_dot_split_rhs(same.astype(BF16), a[u], 3) for u in us]
        end_e = [_dot_split_lhs(tot[u], expand, 3) for u in us]
    xdt = [xm[u] * dt_e[u] for u in us]
    xs = [(xdt[u] * jnp.exp(end_e[u] - cs_e[u])).astype(BF16) for u in us]
    cb = [_dot(cmb[u], btb[u]) for u in us]
    if seqs == 1:
        st_new = [_dot(btb[u], xs[u]) for u in us]
        ht = [ht_ref[g] for g in gs]
        y_off = []
        for u, (sc, g) in enumerate(units):
            y_off.append(_dot(cmb[u], ht[g].astype(BF16)))
            ht[g] = ht[g] * jnp.exp(end_e[u]) + st_new[u]
        for g in gs:
            ht_ref[g] = ht[g]
    else:
        y_off = []
        for u, (sc, g) in enumerate(units):
            tiles = []
            for s in range(seqs):
                r0 = s * steps
                ht0 = st_ref[s, g * heads:(g + 1) * heads].reshape(width, state_n).T
                tiles.append(_dot(cm[u][r0:r0 + steps].astype(BF16), ht0.astype(BF16)))
                bts = jnp.where((col // steps) == s, bt[u], 0.0).astype(BF16)
                ht1 = ht0 * jnp.exp(end_e[u][r0:r0 + 1]) + _dot(bts, xs[u])
                so_ref[s, g * heads:(g + 1) * heads] = ht1.T.reshape((heads,) + so_ref.shape[2:])
            y_off.append(jnp.concatenate(tiles, axis=0))
    ys = [[] for _ in us]
    for j in range(heads // 2):
        lhs, rhs = [], []
        for u in us:
            ms = []
            for hh in (2 * j, 2 * j + 1):
                if seqs == 1:
                    cs_col = jnp.broadcast_to(cs[u][:, hh:hh + 1], (R, R))
                else:
                    cs_col = cs_e2[u][:, hh * R:(hh + 1) * R]
                seg_ = cs_col - cs_t[u][hh:hh + 1, :]
                ms.append(jnp.where(causal, cb[u] * jnp.exp(jnp.where(causal, seg_, 0.0)), 0.0).astype(BF16))
            lhs.append(jnp.concatenate(ms, axis=1))
            xp = xdt[u][:, j * LANES:(j + 1) * LANES]
            rhs.append(jnp.concatenate([jnp.where(lane < p_dim, xp, 0.0), jnp.where(lane >= p_dim, xp, 0.0)],
                                       axis=0).astype(BF16))
        for u in us:
            ys[u].append(_dot(lhs[u], rhs[u]))
    for u, (sc, g) in enumerate(units):
        y = jnp.concatenate(ys[u], axis=1) + y_off[u] * jnp.exp(cs_e[u])
        y = y + gcols(dexp_ref, u, width) * xm[u]
        yz = y * _silu(blk(z_ref, u, width).astype(F32))
        yn = yz * lax.rsqrt(jnp.mean(yz * yz, axis=-1, keepdims=True) + NORM_EPS) * gcols(ng_ref, u, width)
        if subs > 1:
            y_ref[sc * R:(sc + 1) * R, g * width:(g + 1) * width] = yn.astype(y_ref.dtype)
        else:
            y_ref[:, g * width:(g + 1) * width] = yn[:rows].astype(y_ref.dtype)

    if seqs == 1:
        @pl.when(step == n_steps - 1)
        def _():
            for g in gs:
                so_ref[0, g * heads:(g + 1) * heads] = ht_ref[g].T.reshape((heads,) + so_ref.shape[2:])


def _wkv_body(r_ref, lw_ref, k_ref, v_ref, kk_ref, bb_ref, g_ref, s0_ref, rk_ref, lnw_ref, lnb_ref,
              bo_ref, tri_ref, o_ref, so_ref, s_ref, *, head_dim, single):
    nseq, steps, dim = r_ref.shape
    npair = dim // LANES
    nchunk = steps // SUBLANES
    block_ones = bo_ref[...]
    tri = tri_ref[...]

    def load_state(si, p):
        x = s0_ref[si, 2 * p:2 * p + 2].reshape(2 * head_dim, head_dim)
        xp = jnp.concatenate([x, jnp.zeros((2 * head_dim, LANES - head_dim), F32)], axis=1)
        return xp.T[:head_dim]

    def store_state(n, s):
        sp = jnp.concatenate([s, jnp.zeros((LANES - head_dim, LANES), F32)], axis=0)
        back = sp.T[:, :head_dim]
        so_ref[n // npair, 2 * (n % npair):2 * (n % npair) + 2] = back.reshape(2, head_dim, head_dim)

    if not single:
        @pl.when(pl.program_id(1) == 0)
        def _():
            for si in range(nseq):
                for p in range(npair):
                    s_ref[si * npair + p] = load_state(si, p)

    lane8 = lax.broadcasted_iota(jnp.int32, (SUBLANES, LANES), 1)
    row8 = lax.broadcasted_iota(jnp.int32, (SUBLANES, LANES), 0)
    head0 = lane8 < head_dim
    rowi = lax.broadcasted_iota(jnp.int32, (head_dim, LANES), 0)
    lanei = lax.broadcasted_iota(jnp.int32, (head_dim, LANES), 1)
    diag = (rowi == lanei % head_dim).astype(F32)

    def other_head(a):
        return pltpu.roll(a, head_dim, 1)

    def rows(a, c):
        return a[c * SUBLANES:(c + 1) * SUBLANES]

    def bc(tile, i):
        return jnp.broadcast_to(tile[i:i + 1], (SUBLANES, LANES))

    chains = []
    for si in range(nseq):
        cum_seq = _dot_split_rhs(tri, lw_ref[si], 3)
        for p in range(npair):
            sl = slice(p * LANES, (p + 1) * LANES)
            lw = lw_ref[si, :, sl]
            cum = cum_seq[:, sl]
            p_in = jnp.exp(cum)
            p_inv = jnp.exp(-cum)
            at = kk_ref[si, :, sl] * jnp.exp(cum - lw)
            rt = r_ref[si, :, sl] * p_in
            bt = bb_ref[si, :, sl] * p_inv
            kt = k_ref[si, :, sl] * p_inv
            at_o, rt_o, bt_o, kt_o = other_head(at), other_head(rt), other_head(bt), other_head(kt)
            x4 = jnp.concatenate([rows(a, c) for c in range(nchunk) for a in (bt, bt_o, kt, kt_o)], axis=0)
            if x4.shape[0] < LANES:
                x4 = jnp.concatenate([x4, jnp.zeros((LANES - x4.shape[0], LANES), F32)], axis=0)
            xt = x4.T[:head_dim].astype(BF16)
            chains.append(dict(si=si, sl=sl, at=at, rt=rt, bt=bt, kt=kt, at_o=at_o, rt_o=rt_o, xt=xt,
                               v=v_ref[si, :, sl], p_end=p_in[steps - 1:steps],
                               s=load_state(si, p) if single else s_ref[si * npair + p],
                               ys=[]))

    for c in range(nchunk):
        for ch in chains:
            at_c, rt_c, bt_c, kt_c = rows(ch['at'], c), rows(ch['rt'], c), rows(ch['bt'], c), rows(ch['kt'], c)
            tiles = []
            for i in range(SUBLANES):
                am = jnp.where(row8 > i, at_c, 0.0)
                rm = jnp.where(row8 >= i, rt_c, 0.0)
                bi, ki = bc(bt_c, i), bc(kt_c, i)
                tiles += [am * bi, am * ki, rm * bi, rm * ki]
            coef = _dot(jnp.concatenate(tiles, axis=0).astype(BF16), block_ones)
            ch['coef'] = [rows(coef, n) for n in range(4 * SUBLANES)]
            v_c = rows(ch['v'], c)
            va = None
            yv = None
            for i in range(SUBLANES):
                vi = bc(v_c, i)
                t_ak = ch['coef'][4 * i + 1] * vi
                t_rk = ch['coef'][4 * i + 3] * vi
                va = t_ak if va is None else va + t_ak
                yv = t_rk if yv is None else yv + t_rk
            ch['va'], ch['yv'], ch['v_c'] = va, yv, v_c
            lhs = jnp.concatenate([at_c[:, :head_dim], rows(ch['at_o'], c)[:, :head_dim],
                                   rt_c[:, :head_dim], rows(ch['rt_o'], c)[:, :head_dim]], axis=0)
            ch['lhs'] = lhs.astype(BF16)
        for ch in chains:
            ch['g'] = _dot(ch['lhs'], ch['s'].astype(BF16))
        for ch in chains:
            g = ch['g']
            g_a = jnp.where(head0, rows(g, 0), rows(g, 1))
            g_r = jnp.where(head0, rows(g, 2), rows(g, 3))
            sa = g_a + ch['va']
            y = g_r + ch['yv']
            for i in range(SUBLANES):
                sai = bc(sa, i)
                if i + 1 < SUBLANES:
                    sa = sa - ch['coef'][4 * i] * sai
                y = y - ch['coef'][4 * i + 2] * sai
            ch['ys'].append(y)
            v_c = ch['v_c']
            wd = jnp.concatenate([jnp.where(head0, -sa, 0.0), jnp.where(head0, 0.0, -sa),
                                  jnp.where(head0, v_c, 0.0), jnp.where(head0, 0.0, v_c)], axis=0)
            ch['wd'] = wd.astype(BF16)
        for ch in chains:
            cols = ch['xt'][:, c * 4 * SUBLANES:(c + 1) * 4 * SUBLANES]
            ch['s'] = ch['s'] + _dot(cols, ch['wd'])

    inv_n = 1.0 / head_dim
    cat = lambda key: jnp.concatenate([ch[key] for ch in chains], axis=0)
    part = lambda a, n, m: a[n * m:(n + 1) * m]
    for ch in chains:
        ch['pd'] = diag * ch['p_end']
        ch['y'] = jnp.concatenate(ch['ys'], axis=0)
        ch['rk'] = r_ref[ch['si'], :, ch['sl']] * k_ref[ch['si'], :, ch['sl']] * rk_ref[:, ch['sl']]
    p_col = _dot_split_lhs(cat('pd'), block_ones, 3)
    y_all = cat('y')
    mu = _dot_split_lhs(y_all, block_ones, 2) * inv_n
    bonus = _dot_split_lhs(cat('rk'), block_ones, 2)
    d_all = y_all - mu
    var = _dot_split_lhs(d_all * d_all, block_ones, 2) * inv_n
    for n, ch in enumerate(chains):
        si, sl = ch['si'], ch['sl']
        s_new = ch['s'] * part(p_col, n, head_dim)
        if single:
            store_state(n, s_new)
        else:
            s_ref[n] = s_new
            ch['s'] = s_new
        yn = part(d_all, n, steps) * lax.rsqrt(part(var, n, steps) + GN_EPS) * lnw_ref[:, sl] + lnb_ref[:, sl]
        out = (yn + part(bonus, n, steps) * ch['v']) * g_ref[si, :, sl].astype(F32)
        o_ref[si, :, sl] = out.astype(o_ref.dtype)

    if not single:
        @pl.when(pl.program_id(1) == pl.num_programs(1) - 1)
        def _():
            for n, ch in enumerate(chains):
                store_state(n, ch['s'])


def _post_body(x_ref, ya_ref, yb_ref, gt_ref, hist_ref, wa_ref, wb_ref, wo_ref, g2_ref, wup_ref,
               cw_ref, cb_ref, wdn_ref, gf_ref, o_ref, tail_ref, carry_ref, *, seg, final_norm):
    dm = x_ref.shape[1]
    ua = _dot(ya_ref[...].astype(BF16), wa_ref[...])
    ub = _dot(yb_ref[...].astype(BF16), wb_ref[...])
    gates = gt_ref[...].astype(F32)
    m = (gates[:, :dm] * ua + gates[:, dm:] * ub).astype(BF16)
    x1 = x_ref[...] + _dot(m, wo_ref[...])
    h2 = _rmsnorm(x1, g2_ref[...], NORM_EPS).astype(BF16)
    up = _dot(h2, wup_ref[...])
    dff = up.shape[1] // 2
    ug = up[:, :dff]
    hist = _load_hist(hist_ref, carry_ref, seg)
    ugc = _causal_conv(ug, hist, seg, cw_ref, cb_ref)
    _store_tail(ug, tail_ref, carry_ref, seg)
    act = (_silu(ugc) * up[:, dff:]).astype(BF16)
    x2 = x1 + _dot(act, wdn_ref[...])
    if final_norm:
        x2 = _rmsnorm(x2, gf_ref[...], NORM_EPS)
    o_ref[...] = x2


def _const_spec(shape):
    nd = len(shape)
    return pl.BlockSpec(shape, lambda *_: (0,) * nd, pipeline_mode=pl.Buffered(1))


def _params(sem):
    return pltpu.CompilerParams(dimension_semantics=sem, vmem_limit_bytes=VMEM_LIMIT)


def _token_tiling(nseq, seqlen, tile=ROW_TILE):
    if seqlen == SUBLANES:
        per = tile // SUBLANES
        assert nseq % per == 0
        return per, SUBLANES, (nseq // per, 1)
    assert seqlen % tile == 0
    return 1, tile, (nseq, seqlen // tile)


def _row_spec(cols, lt, tile=ROW_TILE):
    return pl.BlockSpec((tile, cols), lambda i, l: (i * lt + l, 0))


def _hist_spec(per, cols):
    return pl.BlockSpec((per, SUBLANES, cols), lambda i, l: (i, 0, 0))


def _ssm_in(x, hist, p, nseq, seqlen, act_dtype):
    tile = SSM_ROW_TILE if seqlen % SSM_ROW_TILE == 0 else ROW_TILE
    per, seg, grid = _token_tiling(nseq, seqlen, tile)
    row_spec = functools.partial(_row_spec, tile=tile)
    t, dm = x.shape
    dz = p['wz'].shape[1]
    dc = p['wx'].shape[1]
    dd = p['wdt'].shape[1]
    lt = grid[1]
    return pl.pallas_call(
        functools.partial(_ssm_in_body, seg=seg),
        grid=grid,
        in_specs=[row_spec(dm, lt), _hist_spec(per, dc), _const_spec((1, dm)),
                  _const_spec(p['wz'].shape), _const_spec(p['wx'].shape), _const_spec(p['wdt'].shape),
                  _const_spec(p['conv_w'].shape), _const_spec((1, dc)), _const_spec((1, dd))],
        out_specs=[row_spec(dz, lt), row_spec(dc, lt), row_spec(dd, lt), _hist_spec(per, dc)],
        out_shape=[jax.ShapeDtypeStruct((t, dz), act_dtype), jax.ShapeDtypeStruct((t, dc), F32),
                   jax.ShapeDtypeStruct((t, dd), F32), jax.ShapeDtypeStruct((nseq, SUBLANES, dc), F32)],
        scratch_shapes=[pltpu.VMEM((SUBLANES, dc), F32)],
        compiler_params=_params(("parallel", "arbitrary")),
        name="ssm_in",
    )(x, hist, p['norm1_g'], p['wz'], p['wx'], p['wdt'], p['conv_w'], p['conv_b'], p['dt_bias'])


def _rwkv_in(x, hist, p, nseq, seqlen, act_dtype):
    per, seg, grid = _token_tiling(nseq, seqlen)
    t, dm = x.shape
    dim = p['w0'].shape[1]
    drw = p['wrw'].shape[1]
    dg = p['wg'].shape[1]
    lt = grid[1]
    f32_out = jax.ShapeDtypeStruct((t, dim), F32)
    return pl.pallas_call(
        functools.partial(_rwkv_in_body, seg=seg, dim=dim),
        grid=grid,
        in_specs=[_row_spec(dm, lt), _hist_spec(per, drw), _const_spec((1, dm)),
                  _const_spec(p['wrw'].shape), _const_spec(p['wg'].shape), _const_spec((1, drw)),
                  _const_spec(p['wlora'].shape), _const_spec((1, dim)), _const_spec((1, dim)),
                  _const_spec((1, dim)), _const_spec((1, dim)),
                  _const_spec(p['head_sum'].shape), _const_spec(p['head_expand'].shape)],
        out_specs=[_row_spec(dim, lt)] * 7 + [_row_spec(dg, lt), _hist_spec(per, drw)],
        out_shape=[f32_out] * 6 + [jax.ShapeDtypeStruct((t, dim), act_dtype),
                                   jax.ShapeDtypeStruct((t, dg), BF16),
                                   jax.ShapeDtypeStruct((nseq, SUBLANES, drw), F32)],
        scratch_shapes=[pltpu.VMEM((SUBLANES, drw), F32)],
        compiler_params=_params(("parallel", "arbitrary")),
        name="rwkv_in",
    )(x, hist, p['norm1_g'], p['wrw'], p['wg'], p['mu'], p['wlora'], p['w0'], p['a0'],
      p['k_k'], p['k_a'], p['head_sum'], p['head_expand'])


def _ssd(xc, dt, z, state, p, nseq, seqlen, act_dtype):
    t = xc.shape[0]
    _, n_heads, p_dim, state_n = state.shape
    groups = dt.shape[1] // LANES
    heads = n_heads // groups
    width = heads * p_dim
    d_inner = groups * width
    assert state_n == LANES and width % LANES == 0 and p_dim * 2 == LANES
    subs = 1
    if seqlen % SSD_CHUNK == 0:
        subs = SSD_SUBCHUNKS if seqlen % (SSD_SUBCHUNKS * SSD_CHUNK) == 0 else 1
        seqs, gps, rows, nc = 1, groups, subs * SSD_CHUNK, seqlen // (subs * SSD_CHUNK)
    elif seqlen == SUBLANES and nseq % (SSD_CHUNK // SUBLANES) == 0:
        seqs, gps, rows, nc = SSD_CHUNK // SUBLANES, 1, SSD_CHUNK, 1
    else:
        assert seqlen == SUBLANES
        seqs, gps, rows, nc = 1, groups, SUBLANES, 1
    gn = gps * state_n
    b_blk = d_inner // gn
    assert d_inner % gn == 0
    row_map = lambda b, g, c: (b * nc + c, g)
    st_spec = pl.BlockSpec((seqs, gps * heads, p_dim, state_n), lambda b, g, c: (b, g, 0, 0))
    return pl.pallas_call(
        functools.partial(_ssd_body, rows=rows, heads=heads, groups=gps, seqs=seqs, subs=subs),
        grid=(nseq // seqs, groups // gps, nc),
        in_specs=[pl.BlockSpec((rows, gps * width), row_map),
                  pl.BlockSpec((rows, gn), lambda b, g, c: (b * nc + c, b_blk + g)),
                  pl.BlockSpec((rows, gn), lambda b, g, c: (b * nc + c, b_blk + groups // gps + g)),
                  pl.BlockSpec((rows, gps * LANES), row_map),
                  pl.BlockSpec((rows, gps * width), row_map),
                  st_spec,
                  pl.BlockSpec((1, gps * LANES), lambda b, g, c: (0, g)),
                  pl.BlockSpec((1, gps * width), lambda b, g, c: (0, g)),
                  pl.BlockSpec((1, gps * width), lambda b, g, c: (0, g)),
                  _const_spec(p['tri'].shape), _const_spec(p['ssd_expand'].shape),
                  _const_spec(p['ssd_expand2'].shape)],
        out_specs=[pl.BlockSpec((rows, gps * width), row_map), st_spec],
        out_shape=[jax.ShapeDtypeStruct((t, d_inner), act_dtype),
                   jax.ShapeDtypeStruct(state.shape, F32)],
        scratch_shapes=[pltpu.VMEM((gps, state_n, width), F32)],
        compiler_params=_params(("parallel", "parallel", "arbitrary")),
        name="ssd",
    )(xc, xc, xc, dt, z, state, p['a_log'], p['d_exp'], p['ssm_norm_g'],
      p['tri'], p['ssd_expand'], p['ssd_expand2'])


def _wkv(r, lw, k, v, kk, bb, g, state, p, nseq, seqlen, act_dtype):
    dim = r.shape[1]
    _, n_heads, head_dim, _ = state.shape
    npair = dim // LANES
    assert n_heads == 2 * npair and 2 * head_dim == LANES
    if seqlen == SUBLANES:
        per, steps = WKV_SAMPLE_SEQS, SUBLANES
    else:
        per, steps = (WKV_PROMPT_SEQS if nseq % WKV_PROMPT_SEQS == 0 else 1), WKV_TBLOCK
    assert nseq % per == 0 and seqlen % steps == 0
    grid = (nseq // per, seqlen // steps)
    seq_spec = pl.BlockSpec((per, steps, dim), lambda i, tb: (i, tb, 0))
    st_spec = pl.BlockSpec((per, n_heads, head_dim, head_dim), lambda i, tb: (i, 0, 0, 0))
    as3 = lambda a: a.reshape(nseq, seqlen, dim)
    out, s_out = pl.pallas_call(
        functools.partial(_wkv_body, head_dim=head_dim, single=(seqlen == steps)),
        grid=grid,
        in_specs=[seq_spec] * 7 + [st_spec, _const_spec((1, dim)), _const_spec((1, dim)),
                                   _const_spec((1, dim)), _const_spec((LANES, LANES)),
                                   _const_spec((steps, steps))],
        out_specs=[seq_spec, st_spec],
        out_shape=[jax.ShapeDtypeStruct((nseq, seqlen, dim), act_dtype),
                   jax.ShapeDtypeStruct(state.shape, F32)],
        scratch_shapes=[pltpu.VMEM((per * npair, head_dim, LANES), F32)],
        compiler_params=_params(("parallel", "arbitrary")),
        name="wkv",
    )(as3(r), as3(lw), as3(k), as3(v), as3(kk), as3(bb), as3(g), state,
      p['r_k'], p['ln_w'], p['ln_b'], p['block_ones'], p['tri'][:steps, :steps])
    return out.reshape(nseq * seqlen, dim), s_out


def _post(x, ya, yb, gates, hist, p, nseq, seqlen, final_norm):
    per, seg, grid = _token_tiling(nseq, seqlen)
    t, dm = x.shape
    dff = p['wdn'].shape[0]
    lt = grid[1]
    return pl.pallas_call(
        functools.partial(_post_body, seg=seg, final_norm=final_norm),
        grid=grid,
        in_specs=[_row_spec(dm, lt), _row_spec(ya.shape[1], lt), _row_spec(yb.shape[1], lt),
                  _row_spec(gates.shape[1], lt), _hist_spec(per, dff),
                  _const_spec(p['wa'].shape), _const_spec(p['wb'].shape), _const_spec(p['wo'].shape),
                  _const_spec((1, dm)), _const_spec(p['wup'].shape), _const_spec(p['ffn_conv_w'].shape),
                  _const_spec((1, dff)), _const_spec(p['wdn'].shape), _const_spec((1, dm))],
        out_specs=[_row_spec(dm, lt), _hist_spec(per, dff)],
        out_shape=[jax.ShapeDtypeStruct((t, dm), F32), jax.ShapeDtypeStruct((nseq, SUBLANES, dff), F32)],
        scratch_shapes=[pltpu.VMEM((SUBLANES, dff), F32)],
        compiler_params=_params(("parallel", "arbitrary")),
        name="post",
    )(x, ya, yb, gates, hist, p['wa'], p['wb'], p['wo'], p['norm2_g'], p['wup'], p['ffn_conv_w'],
      p['ffn_conv_b'], p['wdn'], p['final_g'])


def _prep_layer(w, dims):
    d_inner, conv_dim, n_heads, shift_dim, groups, rwkv_dim, rwkv_heads = dims
    row = lambda a: a.reshape(1, -1).astype(F32)
    w_in = w['w_in']
    o1 = d_inner
    o2 = o1 + conv_dim
    o3 = o2 + n_heads
    o4 = o3 + shift_dim
    hpg = n_heads // groups

    def per_group(a):
        lead = a.shape[:-1]
        a = a.reshape(lead + (groups, hpg))
        a = jnp.pad(a, [(0, 0)] * len(lead) + [(0, 0), (0, LANES - hpg)])
        return a.reshape(lead + (groups * LANES,))

    p_dim = d_inner // n_heads
    head_dim = rwkv_dim // rwkv_heads
    lora_w = w['rwkv_w_up'].shape[0]
    lora_a = w['rwkv_a_up'].shape[0]
    lora_g = w['rwkv_g_up'].shape[0]
    assert lora_w == lora_a and lora_g == 2 * lora_w
    wlora = jnp.zeros((lora_w + lora_a + lora_g, 3 * rwkv_dim), F32)
    wlora = wlora.at[:lora_w, :rwkv_dim].set(w['rwkv_w_up'])
    wlora = wlora.at[lora_w:lora_w + lora_a, rwkv_dim:2 * rwkv_dim].set(w['rwkv_a_up'])
    wlora = wlora.at[lora_w + lora_a:, 2 * rwkv_dim:].set(w['rwkv_g_up'])

    ch = jnp.arange(rwkv_dim) // head_dim
    head_sum = (ch[:, None] == jnp.arange(LANES)[None, :]).astype(BF16)
    li = jnp.arange(LANES)
    width = hpg * p_dim
    ssd_expand = (li[:, None] == (jnp.arange(width) // p_dim)[None, :]).astype(BF16)
    ssd_expand2 = (li[:, None] == (jnp.arange(hpg * SSD_CHUNK) // SSD_CHUNK)[None, :]).astype(BF16)
    ci = jnp.arange(SSD_CHUNK)
    tri = (ci[:, None] >= ci[None, :]).astype(BF16)
    block_ones = ((li[:, None] // head_dim) == (li[None, :] // head_dim)).astype(BF16)

    return {
        'norm1_g': row(w['norm1_g']),
        'wz': w_in[:, :o1].astype(BF16),
        'wx': w_in[:, o1:o2].astype(BF16),
        'wdt': per_group(w_in[:, o2:o3]).astype(BF16),
        'wrw': w_in[:, o3:o4].astype(BF16),
        'wg': w_in[:, o4:].astype(BF16),
        'conv_w': w['ssm_conv_w'].astype(F32),
        'conv_b': row(w['ssm_conv_b']),
        'dt_bias': row(per_group(w['ssm_dt_bias'])),
        'a_log': row(per_group(w['ssm_a_log'])),
        'd_exp': row(jnp.repeat(w['ssm_d'], p_dim)),
        'ssm_norm_g': row(w['ssm_norm_g']),
        'wa': w['w_branch_a'].astype(BF16),
        'mu': row(w['rwkv_mu']),
        'wlora': wlora.astype(BF16),
        'w0': row(w['rwkv_w0']),
        'a0': row(w['rwkv_a0']),
        'k_k': row(w['rwkv_k_k']),
        'k_a': row(w['rwkv_k_a']),
        'r_k': row(w['rwkv_r_k']),
        'ln_w': row(w['rwkv_ln_w']),
        'ln_b': row(w['rwkv_ln_b']),
        'wb': w['w_branch_b'].astype(BF16),
        'wo': w['w_out'].astype(BF16),
        'norm2_g': row(w['norm2_g']),
        'wup': w['ffn_w_up'].astype(BF16),
        'ffn_conv_w': w['ffn_conv_w'].astype(F32),
        'ffn_conv_b': row(w['ffn_conv_b']),
        'wdn': w['ffn_w_down'].astype(BF16),
        'head_sum': head_sum,
        'head_expand': head_sum.T,
        'ssd_expand': ssd_expand,
        'ssd_expand2': ssd_expand2,
        'tri': tri,
        'block_ones': block_ones,
    }


def _hist8(state_rows):
    k = state_rows.shape[1]
    return jnp.pad(state_rows.astype(F32), ((0, 0), (SUBLANES - k, 0), (0, 0)))


def _layer(x, conv_buf, ssm_state, shift_buf, wkv_state, ffn_buf, p, final_norm):
    nseq, seqlen, dm = x.shape
    act_dtype = BF16 if seqlen % 16 == 0 else F32
    xf = x.reshape(nseq * seqlen, dm)
    z, xc, dt, conv_tail = _ssm_in(xf, _hist8(conv_buf), p, nseq, seqlen, act_dtype)
    r, lw, k, v, kk, bb, g, gates, shift_tail = _rwkv_in(xf, _hist8(shift_buf[:, None]), p, nseq, seqlen,
                                                         act_dtype)
    ya, new_ssm = _ssd(xc, dt, z, ssm_state.astype(F32), p, nseq, seqlen, act_dtype)
    yb, new_wkv = _wkv(r, lw, k, v, kk, bb, g, wkv_state.astype(F32), p, nseq, seqlen, act_dtype)
    out, ffn_tail = _post(xf, ya, yb, gates, _hist8(ffn_buf), p, nseq, seqlen, final_norm)
    new_conv = conv_tail[:, SUBLANES - conv_buf.shape[1]:]
    new_shift = shift_tail[:, SUBLANES - 1]
    new_ffn = ffn_tail[:, SUBLANES - ffn_buf.shape[1]:]
    return out.reshape(nseq, seqlen, dm), (new_conv, new_ssm, new_shift, new_wkv, new_ffn)


_LAYER_WEIGHTS = ('norm1_g', 'w_in', 'ssm_conv_w', 'ssm_conv_b', 'ssm_dt_bias', 'ssm_a_log', 'ssm_d',
                  'ssm_norm_g', 'w_branch_a', 'rwkv_mu', 'rwkv_w0', 'rwkv_w_up', 'rwkv_a0', 'rwkv_a_up',
                  'rwkv_g_up', 'rwkv_k_k', 'rwkv_k_a', 'rwkv_r_k', 'rwkv_ln_w', 'rwkv_ln_b', 'w_branch_b',
                  'w_out', 'norm2_g', 'ffn_w_up', 'ffn_conv_w', 'ffn_conv_b', 'ffn_w_down')


def kernel(x_prompt, x_sample, state_ssm_conv, state_ssm, state_rwkv_shift, state_rwkv, state_ffn_conv,
           norm1_g, w_in, ssm_conv_w, ssm_conv_b, ssm_dt_bias, ssm_a_log, ssm_d, ssm_norm_g, w_branch_a,
           rwkv_mu, rwkv_w0, rwkv_w_up, rwkv_a0, rwkv_a_up, rwkv_g_up, rwkv_k_k, rwkv_k_a, rwkv_r_k,
           rwkv_ln_w, rwkv_ln_b, w_branch_b, w_out, norm2_g, ffn_w_up, ffn_conv_w, ffn_conv_b, ffn_w_down,
           final_g):
    stacked = dict(zip(_LAYER_WEIGHTS, (
        norm1_g, w_in, ssm_conv_w, ssm_conv_b, ssm_dt_bias, ssm_a_log, ssm_d, ssm_norm_g, w_branch_a,
        rwkv_mu, rwkv_w0, rwkv_w_up, rwkv_a0, rwkv_a_up, rwkv_g_up, rwkv_k_k, rwkv_k_a, rwkv_r_k,
        rwkv_ln_w, rwkv_ln_b, w_branch_b, w_out, norm2_g, ffn_w_up, ffn_conv_w, ffn_conv_b, ffn_w_down)))
    depth = w_in.shape[0]
    _, _, n_heads, _, _ = state_ssm.shape
    conv_dim = state_ssm_conv.shape[-1]
    d_inner = w_branch_a.shape[1]
    shift_dim = state_rwkv_shift.shape[-1]
    rwkv_heads = state_rwkv.shape[2]
    rwkv_dim = w_branch_b.shape[1]
    groups = (conv_dim - d_inner) // (2 * state_ssm.shape[-1])
    dims = (d_inner, conv_dim, n_heads, shift_dim, groups, rwkv_dim, rwkv_heads)

    xp, xs = x_prompt, x_sample
    bp = xp.shape[0]
    new_p = ([], [], [], [], [])
    new_s = ([], [], [], [], [])
    for i in range(depth):
        p = _prep_layer({name: a[i] for name, a in stacked.items()}, dims)
        p['final_g'] = final_g.reshape(1, -1).astype(F32)
        last = i == depth - 1
        xp, sp = _layer(
            xp,
            jnp.zeros((bp,) + state_ssm_conv.shape[2:], F32),
            jnp.zeros((bp,) + state_ssm.shape[2:], F32),
            jnp.zeros((bp,) + state_rwkv_shift.shape[2:], F32),
            jnp.zeros((bp,) + state_rwkv.shape[2:], F32),
            jnp.zeros((bp,) + state_ffn_conv.shape[2:], F32),
            p, last)
        xs, ss = _layer(xs, state_ssm_conv[i], state_ssm[i], state_rwkv_shift[i], state_rwkv[i],
                        state_ffn_conv[i], p, last)
        for j in range(5):
            new_p[j].append(sp[j])
            new_s[j].append(ss[j])
    return (xp, xs,
            jnp.stack(new_p[0]), jnp.stack(new_p[1]), jnp.stack(new_p[2]), jnp.stack(new_p[3]),
            jnp.stack(new_p[4]),
            jnp.stack(new_s[0]), jnp.stack(new_s[1]), jnp.stack(new_s[2]), jnp.stack(new_s[3]),
            jnp.stack(new_s[4]))
```
